```python
import jax, jax.numpy as jnp
from jax import lax
import numpy as np

D_MODEL = 1024
BATCH = 16
SEQ = 256
DEPTH = 2
DEC_BATCH = 4
DEC_SEQ = 2048
PAST_LEN = 512

GRID_W = 64
M_HEADS = 4
M_HEAD_DIM = 256
M_WIDTH = M_HEADS * M_HEAD_DIM
CHUNK = 128
CONV_W = 3
N_HEADS = 16
N_HEAD_DIM = 64
N_WIDTH = N_HEADS * N_HEAD_DIM
MAX_WIN_H = 8
WIN_W = 16
N_EXPERTS = 16
EXPERT_FF = 2048
CAPACITY_FACTOR = 2
ROPE_BASE = 10000.0
Q_BLOCK = 128
EPS = 1e-6
IN_SIZES = [M_WIDTH] * 4 + [4 * M_HEADS] + [N_WIDTH] * 3 + [D_MODEL] * 2
IN_WIDTH = sum(IN_SIZES)

kernel_name = 'hybrid_mlstm_natten_ec_diffusion_step'


def rmsnorm(x, g):
    xf = x.astype(jnp.float32)
    y = xf * lax.rsqrt(jnp.mean(xf * xf, axis=-1, keepdims=True) + EPS)
    return (y * g.astype(jnp.float32)).astype(x.dtype)


def adaln(cond, w_ada, b_ada):
    m = jax.nn.silu(cond) @ w_ada + b_ada
    return jnp.split(m, 6, axis=-1)


def split_in(proj):
    offs = np.cumsum(IN_SIZES)[:-1].tolist()
    return jnp.split(proj, offs, axis=-1)


def short_conv(x, w):
    T = x.shape[1]
    pad = CONV_W // 2
    xp = jnp.pad(x, ((0, 0), (pad, pad), (0, 0)))
    out = xp[:, 0:T] * w[0]
    for j in range(1, CONV_W):
        out = out + xp[:, j:j + T] * w[j]
    return out


def axial_rope(x):
    B, T, H, D = x.shape
    pos = jnp.arange(T)
    rows = (pos // GRID_W).astype(jnp.float32)
    cols = (pos % GRID_W).astype(jnp.float32)
    half = D // 2
    nfreq = half // 2
    inv = ROPE_BASE ** (-jnp.arange(nfreq, dtype=jnp.float32) / nfreq)
    xf = x.astype(jnp.float32)

    def rot(xs, p):
        ang = p[:, None] * inv
        cos = jnp.cos(ang)[None, :, None, :]
        sin = jnp.sin(ang)[None, :, None, :]
        x1, x2 = xs[..., :nfreq], xs[..., nfreq:]
        return jnp.concatenate([x1 * cos - x2 * sin, x1 * sin + x2 * cos], axis=-1)

    return jnp.concatenate([rot(xf[..., :half], rows), rot(xf[..., half:], cols)], axis=-1).astype(x.dtype)


def mlstm_scan(q, k, v, ig, lf, C0, n0, m0):
    B, T, H, D = q.shape
    nc = T // CHUNK

    def chunks(a):
        a = a.astype(jnp.float32).reshape((B, nc, CHUNK) + a.shape[2:])
        return jnp.swapaxes(jnp.moveaxis(a, 1, 0), 2, 3)

    tril = jnp.tril(jnp.ones((CHUNK, CHUNK), dtype=bool))

    def body(carry, xs):
        C, n, m = carry
        qc, kc, vc, ic, fc = xs
        b = jnp.cumsum(fc, axis=-1)
        log_d = b[..., :, None] - b[..., None, :] + ic[..., None, :]
        log_d = jnp.where(tril, log_d, -jnp.inf)
        inter = b + m[..., None]
        m_row = jnp.maximum(inter, jnp.max(log_d, axis=-1))
        dmat = jnp.exp(log_d - m_row[..., None])
        s_inter = jnp.exp(inter - m_row)
        s = jnp.einsum('bhjd,bhsd->bhjs', qc, kc) * dmat
        num = jnp.einsum('bhjs,bhse->bhje', s, vc) + s_inter[..., None] * jnp.einsum('bhjd,bhde->bhje', qc, C)
        den = jnp.sum(s, axis=-1) + s_inter * jnp.einsum('bhjd,bhd->bhj', qc, n)
        h = num / jnp.maximum(jnp.abs(den), jnp.exp(-m_row))[..., None]
        b_last = b[..., -1]
        log_w = b_last[..., None] - b + ic
        m_new = jnp.maximum(b_last + m, jnp.max(log_w, axis=-1))
        w = jnp.exp(log_w - m_new[..., None])
        decay = jnp.exp(b_last + m - m_new)
        C_new = decay[..., None, None] * C + jnp.einsum('bhs,bhsd,bhse->bhde', w, kc, vc)
        n_new = decay[..., None] * n + jnp.einsum('bhs,bhsd->bhd', w, kc)
        return (C_new, n_new, m_new), h

    carry0 = (C0.astype(jnp.float32), n0.astype(jnp.float32), m0.astype(jnp.float32))
    (C, n, m), h = lax.scan(body, carry0, (chunks(q), chunks(k), chunks(v), chunks(ig), chunks(lf)))
    h = jnp.moveaxis(jnp.swapaxes(h, 2, 3), 0, 1).reshape(B, T, H, D)
    return h, C, n, m


def mlstm_bidirectional(q, k, v, ig, lf, C0, n0, m0):
    hf, Cf, nf, mf = mlstm_scan(q, k, v, ig[:, :, 0], lf[:, :, 0], C0[:, 0], n0[:, 0], m0[:, 0])
    fl = lambda a: jnp.flip(a, axis=1)
    hb, Cb, nb, mb = mlstm_scan(fl(q), fl(k), fl(v), fl(ig[:, :, 1]), fl(lf[:, :, 1]), C0[:, 1], n0[:, 1], m0[:, 1])
    return hf + fl(hb), jnp.stack([Cf, Cb], axis=1), jnp.stack([nf, nb], axis=1), jnp.stack([mf, mb], axis=1)


def dense_attention(q, k, v):
    B, T, H, hd = q.shape
    nb = T // Q_BLOCK
    scale = hd ** -0.5
    qb = jnp.moveaxis(q.reshape(B, nb, Q_BLOCK, H, hd), 1, 0)

    def blk(qi):
        s = jnp.einsum('bqhd,bkhd->bhqk', qi, k).astype(jnp.float32) * scale
        p = jax.nn.softmax(s, axis=-1)
        return jnp.einsum('bhqk,bkhd->bqhd', p, v).astype(q.dtype)

    out = lax.map(blk, qb)
    return jnp.moveaxis(out, 0, 1).reshape(B, T, H * hd)


def neighbourhood_attention(q, k, v, kc, vc, rpb):
    B, T, H, hd = q.shape
    rows = T // GRID_W
    wh = min(MAX_WIN_H, rows)
    scale = hd ** -0.5
    qg = q.reshape(B, rows, GRID_W, H, hd)
    kg = k.reshape(B, rows, GRID_W, H, hd)
    vg = v.reshape(B, rows, GRID_W, H, hd)
    col = jnp.arange(GRID_W)
    cstart = jnp.clip(col - WIN_W // 2, 0, GRID_W - WIN_W)
    cidx = cstart[:, None] + jnp.arange(WIN_W)[None, :]
    ci = (cidx - col[:, None] + WIN_W - 1)[:, None, :]

    def row_fn(r):
        rstart = jnp.clip(r - wh // 2, 0, rows - wh)
        qr = lax.dynamic_index_in_dim(qg, r, axis=1, keepdims=False)
        kb = lax.dynamic_slice_in_dim(kg, rstart, wh, axis=1)
        vb = lax.dynamic_slice_in_dim(vg, rstart, wh, axis=1)
        kn = kb[:, :, cidx]
        vn = vb[:, :, cidx]
        ri = (rstart + jnp.arange(wh) - r + MAX_WIN_H - 1)[None, :, None]
        bias = rpb[:, ri, ci].astype(jnp.float32)
        s_loc = jnp.einsum('bqhd,brqwhd->bhqrw', qr, kn).astype(jnp.float32) * scale + bias[None]
        s_loc = s_loc.reshape(B, H, GRID_W, wh * WIN_W)
        s_ctx = jnp.einsum('bqhd,bchd->bhqc', qr, kc).astype(jnp.float32) * scale
        p = jax.nn.softmax(jnp.concatenate([s_loc, s_ctx], axis=-1), axis=-1)
        p_loc = p[..., :wh * WIN_W].reshape(B, H, GRID_W, wh, WIN_W)
        p_ctx = p[..., wh * WIN_W:]
        o = jnp.einsum('bhqrw,brqwhd->bqhd', p_loc, vn) + jnp.einsum('bhqc,bchd->bqhd', p_ctx, vc)
        return o.astype(q.dtype)

    out = lax.map(row_fn, jnp.arange(rows))
    return jnp.moveaxis(out, 0, 1).reshape(B, T, H * hd)


def mixing_sublayer(h, latent, kc, vc, C0, n0, m0, w_in, conv_qk, gate_bias, head_norm, rpb,
                    w_branch_a, w_branch_b, w_out):
    B, T, _ = h.shape
    qm, km, vm, om, gates, qn, kn, vn, ga, gb = split_in(h @ w_in)
    qk = jax.nn.silu(short_conv(jnp.concatenate([qm, km], axis=-1), conv_qk))
    q, k = [a.reshape(B, T, M_HEADS, M_HEAD_DIM) for a in jnp.split(qk, 2, axis=-1)]
    if latent:
        q, k = axial_rope(q), axial_rope(k)
    k = k * (M_HEAD_DIM ** -0.5)
    v = vm.reshape(B, T, M_HEADS, M_HEAD_DIM)
    g = (gates.astype(jnp.float32) + gate_bias.astype(jnp.float32)).reshape(B, T, 4, M_HEADS)
    ig = g[:, :, 0:2]
    lf = jax.nn.log_sigmoid(g[:, :, 2:4])
    hm, C, n, m = mlstm_bidirectional(q, k, v, ig, lf, C0, n0, m0)
    hm = hm * lax.rsqrt(jnp.mean(hm * hm, axis=-1, keepdims=True) + EPS)
    hm = (hm.reshape(B, T, M_WIDTH) * head_norm.astype(jnp.float32)).astype(h.dtype)
    out_a = jax.nn.sigmoid(om) * hm
    qn = qn.reshape(B, T, N_HEADS, N_HEAD_DIM)
    kn = kn.reshape(B, T, N_HEADS, N_HEAD_DIM)
    vn = vn.reshape(B, T, N_HEADS, N_HEAD_DIM)
    if latent:
        out_b = neighbourhood_attention(qn, kn, vn, kc, vc, rpb)
    else:
        out_b = dense_attention(qn, kn, vn)
    merged = jax.nn.sigmoid(ga) * (out_a @ w_branch_a) + jax.nn.sigmoid(gb) * (out_b @ w_branch_b)
    return merged @ w_out, kn, vn, C, n, m


def expert_choice_ffn(h, w_router, w_gate, w_up, w_down):
    B, T, D = h.shape
    N = B * T
    hf = h.reshape(N, D)
    aff = jax.nn.softmax(hf.astype(jnp.float32) @ w_router.astype(jnp.float32), axis=-1)
    cap = max(1, CAPACITY_FACTOR * N // N_EXPERTS)
    gval, idx = lax.top_k(aff.T, cap)
    xs = hf[idx]
    hid = jax.nn.silu(jnp.einsum('ecd,edf->ecf', xs, w_gate)) * jnp.einsum('ecd,edf->ecf', xs, w_up)
    ys = jnp.einsum('ecf,efd->ecd', hid, w_down) * gval[..., None].astype(h.dtype)
    out = jnp.zeros_like(hf).at[idx.reshape(-1)].add(ys.reshape(-1, D).astype(hf.dtype))
    return out.reshape(B, T, D)


def setup_inputs(seed: int = 0) -> dict:
    key = jax.random.key(seed)
    ks = jax.random.split(key, 32)
    D = D_MODEL

    def nrm(k, shape, s):
        return jax.random.normal(k, shape, jnp.float32) * s

    gate_bias = jnp.concatenate([nrm(ks[0], (DEPTH, 2 * M_HEADS), 0.1),
                                 3.0 + 3.0 * jax.random.uniform(ks[1], (DEPTH, 2 * M_HEADS), jnp.float32)], axis=-1)
    return {
        'x_prompt': nrm(ks[2], (BATCH, SEQ, D), 1.0),
        'x_sample': nrm(ks[3], (DEC_BATCH, DEC_SEQ, D), 1.0),
        'cache_na_k': nrm(ks[4], (DEC_BATCH, DEPTH, PAST_LEN, N_HEADS, N_HEAD_DIM), 1.0),
        'cache_na_v': nrm(ks[5], (DEC_BATCH, DEPTH, PAST_LEN, N_HEADS, N_HEAD_DIM), 1.0),
        'state_mlstm_C': nrm(ks[6], (DEC_BATCH, DEPTH, 2, M_HEADS, M_HEAD_DIM, M_HEAD_DIM), 0.5),
        'state_mlstm_n': nrm(ks[7], (DEC_BATCH, DEPTH, 2, M_HEADS, M_HEAD_DIM), 0.5),
        'state_mlstm_m': nrm(ks[8], (DEC_BATCH, DEPTH, 2, M_HEADS), 1.0),
        'c': nrm(ks[9], (DEC_BATCH, D), 1.0),
        'c_ctx': nrm(ks[10], (D,), 1.0),
        'w_ada': nrm(ks[11], (DEPTH, D, 6 * D), 0.5 * D ** -0.5),
        'b_ada': nrm(ks[12], (DEPTH, 6 * D), 0.01),
        'norm_mix': 1.0 + nrm(ks[13], (DEPTH, D), 0.05),
        'norm_ffn': 1.0 + nrm(ks[14], (DEPTH, D), 0.05),
        'w_in': nrm(ks[15], (DEPTH, D, IN_WIDTH), D ** -0.5),
        'conv_qk': nrm(ks[16], (DEPTH, CONV_W, 2 * M_WIDTH), CONV_W ** -0.5),
        'mlstm_gate_bias': gate_bias,
        'mlstm_head_norm': 1.0 + nrm(ks[17], (DEPTH, M_WIDTH), 0.05),
        'na_rpb': nrm(ks[18], (DEPTH, N_HEADS, 2 * MAX_WIN_H - 1, 2 * WIN_W - 1), 0.1),
        'w_branch_a': nrm(ks[19], (DEPTH, M_WIDTH, D), M_WIDTH ** -0.5),
        'w_branch_b': nrm(ks[20], (DEPTH, N_WIDTH, D), N_WIDTH ** -0.5),
        'w_out': nrm(ks[21], (DEPTH, D, D), D ** -0.5),
        'w_router': nrm(ks[22], (DEPTH, D, N_EXPERTS), D ** -0.5),
        'w_expert_gate': nrm(ks[23], (DEPTH, N_EXPERTS, D, EXPERT_FF), D ** -0.5),
        'w_expert_up': nrm(ks[24], (DEPTH, N_EXPERTS, D, EXPERT_FF), D ** -0.5),
        'w_expert_down': nrm(ks[25], (DEPTH, N_EXPERTS, EXPERT_FF, D), EXPERT_FF ** -0.5),
        'norm_final': 1.0 + nrm(ks[26], (D,), 0.05),
    }


def reference(x_prompt, x_sample, cache_na_k, cache_na_v, state_mlstm_C, state_mlstm_n, state_mlstm_m,
              c, c_ctx, w_ada, b_ada, norm_mix, norm_ffn, w_in, conv_qk, mlstm_gate_bias, mlstm_head_norm,
              na_rpb, w_branch_a, w_branch_b, w_out, w_router, w_expert_gate, w_expert_up, w_expert_down,
              norm_final):
    x = x_prompt
    Bp = x_prompt.shape[0]
    ks_, vs_, Cs_, ns_, ms_ = [], [], [], [], []
    for l in range(DEPTH):
        sh1, sc1, g1, sh2, sc2, g2 = adaln(c_ctx, w_ada[l], b_ada[l])
        h = rmsnorm(x, norm_mix[l]) * (1.0 + sc1) + sh1
        C0 = jnp.zeros((Bp, 2, M_HEADS, M_HEAD_DIM, M_HEAD_DIM), jnp.float32)
        n0 = jnp.zeros((Bp, 2, M_HEADS, M_HEAD_DIM), jnp.float32)
        m0 = jnp.zeros((Bp, 2, M_HEADS), jnp.float32)
        y, kn, vn, C, n, m = mixing_sublayer(h, False, None, None, C0, n0, m0, w_in[l], conv_qk[l],
                                             mlstm_gate_bias[l], mlstm_head_norm[l], na_rpb[l],
                                             w_branch_a[l], w_branch_b[l], w_out[l])
        x = x + g1 * y
        h = rmsnorm(x, norm_ffn[l]) * (1.0 + sc2) + sh2
        x = x + g2 * expert_choice_ffn(h, w_router[l], w_expert_gate[l], w_expert_up[l], w_expert_down[l])
        ks_.append(kn)
        vs_.append(vn)
        Cs_.append(C)
        ns_.append(n)
        ms_.append(m)
    y_prompt = rmsnorm(x, norm_final)

    x = x_sample
    for l in range(DEPTH):
        sh1, sc1, g1, sh2, sc2, g2 = [a[:, None, :] for a in adaln(c, w_ada[l], b_ada[l])]
        h = rmsnorm(x, norm_mix[l]) * (1.0 + sc1) + sh1
        y, _, _, _, _, _ = mixing_sublayer(h, True, cache_na_k[:, l], cache_na_v[:, l], state_mlstm_C[:, l],
                                           state_mlstm_n[:, l], state_mlstm_m[:, l], w_in[l], conv_qk[l],
                                           mlstm_gate_bias[l], mlstm_head_norm[l], na_rpb[l],
                                           w_branch_a[l], w_branch_b[l], w_out[l])
        x = x + g1 * y
        h = rmsnorm(x, norm_ffn[l]) * (1.0 + sc2) + sh2
        x = x + g2 * expert_choice_ffn(h, w_router[l], w_expert_gate[l], w_expert_up[l], w_expert_down[l])
    y_sample = rmsnorm(x, norm_final)

    new_na_k = jnp.stack(ks_, axis=1)
    new_na_v = jnp.stack(vs_, axis=1)
    new_mlstm_C = jnp.stack(Cs_, axis=1)
    new_mlstm_n = jnp.stack(ns_, axis=1)
    new_mlstm_m = jnp.stack(ms_, axis=1)
    return (y_prompt, y_sample, new_na_k, new_na_v, new_mlstm_C, new_mlstm_n, new_mlstm_m)
```

```python
import functools

import numpy as np
import jax
import jax.numpy as jnp
from jax import lax
from jax.experimental import pallas as pl
from jax.experimental.pallas import tpu as pltpu

F32 = jnp.float32
BF16 = jnp.bfloat16
I32 = jnp.int32

D = 1024
BATCH, SEQ = 16, 256
DEC_BATCH, DEC_SEQ = 4, 2048
PAST_LEN = 512
NP_TOK = BATCH * SEQ
NS_TOK = DEC_BATCH * DEC_SEQ
N_TOK = NP_TOK + NS_TOK
GRID_W = 64
GRID_H = DEC_SEQ // GRID_W
M_HEADS, M_HD = 4, 256
CHUNK = 128
N_HEADS, N_HD = 16, 64
MAX_WIN_H, WIN_W = 8, 16
N_EXPERTS, EXPERT_FF = 16, 2048
CAP_P = 2 * NP_TOK // N_EXPERTS
CAP_S = 2 * NS_TOK // N_EXPERTS
CAP_TOT = CAP_P + CAP_S
ROPE_BASE = 10000.0
EPS = 1e-6
NEG = -1e30

TM = 256
N_TILES = N_TOK // TM
NP_TILES = NP_TOK // TM
LANES = 128
NA_QROWS = 4
NA_KROWS = 12
FF_CHUNK = 512
SLAB = D // LANES


def _cp(sem, vmem_mb):
    return pltpu.CompilerParams(dimension_semantics=sem, vmem_limit_bytes=vmem_mb * 2 ** 20)


def _dot(a, b):
    return jnp.dot(a, b, preferred_element_type=F32)


def _dot_nt(a, b):
    return lax.dot_general(a, b, (((1,), (1,)), ((), ())), preferred_element_type=F32)


def _dot_tn(a, b):
    return lax.dot_general(a, b, (((0,), (0,)), ((), ())), preferred_element_type=F32)


def _split3(x):
    hi = x.astype(BF16)
    r = x - hi.astype(F32)
    mid = r.astype(BF16)
    lo = (r - mid.astype(F32)).astype(BF16)
    return hi, mid, lo


def _dot_x3(a_bf, x):
    hi, mid, lo = _split3(x)
    return _dot(a_bf, hi) + _dot(a_bf, mid) + _dot(a_bf, lo)


def _sigmoid(x):
    return 1.0 / (1.0 + jnp.exp(-x))


def _rms(x):
    return x * lax.rsqrt(jnp.mean(x * x, axis=-1, keepdims=True) + EPS)


def _mod_row(i):
    return jnp.where(i < NP_TILES, 0, 1 + (i - NP_TILES) // (DEC_SEQ // TM))


def _mod_spec(kind):
    return pl.BlockSpec((1, 1, D), lambda i: (_mod_row(i), 0, kind))


def _adaln_body(c_ref, w_ref, b_ref, o_ref):
    c = c_ref[...]
    s = (c * _sigmoid(c)).astype(BF16)
    o_ref[0] = _dot(s, w_ref[0].astype(BF16)) + b_ref[0]


def _adaln(cond8, w_ada, b_ada):
    depth = w_ada.shape[0]
    tn = 1536
    return pl.pallas_call(
        _adaln_body,
        grid=(depth, 6 * D // tn),
        in_specs=[pl.BlockSpec((8, D), lambda l, j: (0, 0)),
                  pl.BlockSpec((1, D, tn), lambda l, j: (l, 0, j)),
                  pl.BlockSpec((1, 1, tn), lambda l, j: (l, 0, j))],
        out_specs=pl.BlockSpec((1, 8, tn), lambda l, j: (l, 0, j)),
        out_shape=jax.ShapeDtypeStruct((depth, 8, 6 * D), F32),
        compiler_params=_cp(("arbitrary", "arbitrary"), 40),
        name="adaln",
    )(cond8, w_ada, b_ada.reshape(depth, 1, 6 * D))


def _inproj_body(x_ref, g_ref, sc_ref, sh_ref, w_ref, wg_ref, big_ref, gates_ref, kv_ref):
    i = pl.program_id(0)
    h = (_rms(x_ref[...]) * g_ref[...]) * (1.0 + sc_ref[0]) + sh_ref[0]
    hb = h.astype(BF16)
    gates_ref[...] = _dot(hb, wg_ref[...])
    for c in range(9):
        r = _dot(hb, w_ref[:, c * D:(c + 1) * D])
        big_ref[:, c * D:(c + 1) * D] = r.astype(BF16)
        if c in (5, 6):
            @pl.when(i < NP_TILES)
            def _():
                kv_ref[:, (c - 5) * D:(c - 4) * D] = r

    @pl.when(i == NP_TILES)
    def _():
        kv_ref[...] = jnp.zeros_like(kv_ref)


def _inproj(x, norm_g, mod, w_big, w_gates):
    return pl.pallas_call(
        _inproj_body,
        grid=(N_TILES,),
        in_specs=[pl.BlockSpec((TM, D), lambda i: (i, 0)),
                  pl.BlockSpec((1, D), lambda i: (0, 0)),
                  _mod_spec(1), _mod_spec(0),
                  pl.BlockSpec((D, 9 * D), lambda i: (0, 0), pipeline_mode=pl.Buffered(1)),
                  pl.BlockSpec((D, LANES), lambda i: (0, 0))],
        out_specs=[pl.BlockSpec((TM, 9 * D), lambda i: (i, 0)),
                   pl.BlockSpec((TM, LANES), lambda i: (i, 0)),
                   pl.BlockSpec((TM, 2 * D), lambda i: (jnp.minimum(i, NP_TILES), 0))],
        out_shape=[jax.ShapeDtypeStruct((N_TOK, 9 * D), BF16),
                   jax.ShapeDtypeStruct((N_TOK, LANES), F32),
                   jax.ShapeDtypeStruct((NP_TOK + TM, 2 * D), F32)],
        compiler_params=_cp(("arbitrary",), 56),
        name="inproj",
    )(x, norm_g, mod, mod, w_big, w_gates)


def _conv_body(x_ref, w_ref, cos_ref, sin_ref, o_ref, *, rope, T):
    j = pl.program_id(1)
    x = x_ref[...].astype(F32)
    w = w_ref[...]
    row = lax.broadcasted_iota(I32, (T, 1), 0)
    xp = jnp.where(row == 0, 0.0, pltpu.roll(x, 1, 0))
    xn = jnp.where(row == T - 1, 0.0, pltpu.roll(x, T - 1, 0))
    y = xp * w[0:1] + x * w[1:2] + xn * w[2:3]
    y = y * _sigmoid(y)
    scale = jnp.where(j >= M_HEADS, M_HD ** -0.5, 1.0)
    for hlf in range(2):
        sl = slice(hlf * LANES, (hlf + 1) * LANES)
        yh = y[:, sl]
        if rope:
            yh = yh * cos_ref[:, sl] + pltpu.roll(yh, LANES // 2, 1) * sin_ref[:, sl]
        o_ref[:, sl] = (yh * scale).astype(BF16)


def _convprep(big, conv_w, cos_t, sin_t, *, rope, nb, T, row0):
    return pl.pallas_call(
        functools.partial(_conv_body, rope=rope, T=T),
        grid=(nb, 2 * M_HEADS),
        in_specs=[pl.BlockSpec((T, M_HD), lambda b, j: (row0 + b, j)),
                  pl.BlockSpec((3, M_HD), lambda b, j: (0, j)),
                  pl.BlockSpec((T, M_HD), lambda b, j: (0, 0)),
                  pl.BlockSpec((T, M_HD), lambda b, j: (0, 0))],
        out_specs=pl.BlockSpec((T, M_HD), lambda b, j: (b, j)),
        out_shape=jax.ShapeDtypeStruct((nb * T, 2 * M_HEADS * M_HD), BF16),
        compiler_params=_cp(("arbitrary", "arbitrary"), 48),
        name="convprep",
    )(big, conv_w, cos_t, sin_t)


def _rope_tables():
    pos = np.arange(DEC_SEQ)
    rows = (pos // GRID_W).astype(np.float32)
    cols = (pos % GRID_W).astype(np.float32)
    nfreq = M_HD // 4
    inv = (ROPE_BASE ** (-np.arange(nfreq, dtype=np.float32) / nfreq)).astype(np.float32)
    d = np.arange(M_HD)
    p = np.where(d[None, :] < M_HD // 2, rows[:, None], cols[:, None]).astype(np.float32)
    ang = jnp.asarray(p * inv[d % nfreq][None, :], F32)
    sign = np.where((d % (M_HD // 2)) < nfreq, -1.0, 1.0).astype(np.float32)
    return jnp.cos(ang), jnp.sin(ang) * sign[None, :]


def _mlstm_body(*refs, T, has_state, emit_state):
    q_ref, k_ref, v_ref, om_ref, g_ref, gb_ref, hn_ref = refs[:7]
    pos = 7
    if has_state:
        c0_ref, n0_ref, m0_ref = refs[pos:pos + 3]
        pos += 3
    oa_ref = refs[pos]
    pos += 1
    if emit_state:
        co_ref, no_ref, mo_ref = refs[pos:pos + 3]
        pos += 3
    hs_ref, c_ref, n_ref = refs[pos:pos + 3]

    hd = pl.program_id(1)
    nc = T // CHUNK
    lane = lax.broadcasted_iota(I32, (1, LANES), 1)
    r_i = lax.broadcasted_iota(I32, (CHUNK, CHUNK), 0)
    c_i = lax.broadcasted_iota(I32, (CHUNK, CHUNK), 1)
    eye = r_i == c_i

    def to_row(col):
        return jnp.sum(jnp.where(eye, col, 0.0), axis=0, keepdims=True)

    def pick(mat, colidx):
        return jnp.sum(jnp.where(lane == colidx, mat, 0.0), axis=1, keepdims=True)

    for d in range(2):
        causal = (r_i >= c_i) if d == 0 else (r_i <= c_i)
        cum = jnp.where(causal, 1.0, 0.0).astype(BF16)
        if has_state:
            c_ref[...] = c0_ref[0, d, 0]
            n_ref[...] = n0_ref[0, d, 0]
            m_init = m0_ref[0, d, 0]
        else:
            c_ref[...] = jnp.zeros_like(c_ref)
            n_ref[...] = jnp.zeros_like(n_ref)
            m_init = jnp.zeros((1, 1), F32)

        def body(ci, m_prev, d=d, causal=causal, cum=cum):
            c = ci if d == 0 else nc - 1 - ci
            rows = pl.ds(pl.multiple_of(c * CHUNK, CHUNK), CHUNK)
            g = g_ref[rows, :] + gb_ref[...]
            lf = jnp.minimum(g, 0.0) - jnp.log(1.0 + jnp.exp(-jnp.abs(g)))
            bmat = _dot_x3(cum, lf)
            ig_col = pick(g, d * M_HEADS + hd)
            b_col = pick(bmat, 2 * M_HEADS + d * M_HEADS + hd)
            ig_row = to_row(ig_col)
            b_row = to_row(b_col)
            b_last = b_row[:, CHUNK - 1:CHUNK] if d == 0 else b_row[:, 0:1]
            logd = jnp.where(causal, b_col - b_row + ig_row, -jnp.inf)
            inter = b_col + m_prev
            m_row = jnp.maximum(inter, jnp.max(logd, axis=1, keepdims=True))
            dmat = jnp.exp(logd - m_row)
            s_inter = jnp.exp(inter - m_row)
            q = q_ref[rows, :]
            k = k_ref[rows, :]
            v = v_ref[rows, :]
            s = _dot_nt(q, k) * dmat
            num = _dot(s.astype(BF16), v) + s_inter * _dot(q, c_ref[...].astype(BF16))
            den = (jnp.sum(s, axis=1, keepdims=True)
                   + s_inter * jnp.sum(q.astype(F32) * n_ref[...], axis=1, keepdims=True))
            hh = num / jnp.maximum(jnp.abs(den), jnp.exp(-m_row))
            if d == 0:
                hs_ref[rows, :] = hh
            else:
                hs_ref[rows, :] = hs_ref[rows, :] + hh
            log_w = b_last - b_col + ig_col
            m_new = jnp.maximum(b_last + m_prev, jnp.max(log_w, axis=0, keepdims=True))
            w = jnp.exp(log_w - m_new)
            decay = jnp.exp(b_last + m_prev - m_new)
            kw = k.astype(F32) * w
            c_ref[...] = decay * c_ref[...] + _dot_tn(kw.astype(BF16), v)
            n_ref[...] = decay * n_ref[...] + jnp.sum(kw, axis=0, keepdims=True)
            return m_new

        m_fin = lax.fori_loop(0, nc, body, m_init)
        if emit_state:
            co_ref[0, d, 0] = c_ref[...]
            no_ref[0, d, 0] = n_ref[...]
            mo_ref[0, d, 0] = m_fin

    hm = _rms(hs_ref[...]) * hn_ref[...]
    oa_ref[...] = (_sigmoid(om_ref[...].astype(F32)) * hm).astype(BF16)


def _mlstm(qk, big, gates, gate_bias, head_norm, state, *, nb, T, row0, emit_state):
    has_state = state is not None
    in_specs = [pl.BlockSpec((T, M_HD), lambda b, h: (b, h)),
                pl.BlockSpec((T, M_HD), lambda b, h: (b, M_HEADS + h)),
                pl.BlockSpec((T, M_HD), lambda b, h: (row0 + b, 2 * M_HEADS + h)),
                pl.BlockSpec((T, M_HD), lambda b, h: (row0 + b, 3 * M_HEADS + h)),
                pl.BlockSpec((T, LANES), lambda b, h: (row0 + b, 0)),
                pl.BlockSpec((1, LANES), lambda b, h: (0, 0)),
                pl.BlockSpec((1, M_HD), lambda b, h: (0, h))]
    args = [qk, qk, big, big, gates, gate_bias, head_norm]
    if has_state:
        c0, n0, m0 = state
        in_specs += [pl.BlockSpec((1, 2, 1, M_HD, M_HD), lambda b, h: (b, 0, h, 0, 0)),
                     pl.BlockSpec((1, 2, 1, 1, M_HD), lambda b, h: (b, 0, h, 0, 0)),
                     pl.BlockSpec((1, 2, 1, 1, 1), lambda b, h: (b, 0, h, 0, 0))]
        args += [c0, n0, m0]
    out_specs = [pl.BlockSpec((T, M_HD), lambda b, h: (b, h))]
    out_shape = [jax.ShapeDtypeStruct((nb * T, M_HEADS * M_HD), BF16)]
    if emit_state:
        out_specs += [pl.BlockSpec((1, 2, 1, M_HD, M_HD), lambda b, h: (b, 0, h, 0, 0)),
                      pl.BlockSpec((1, 2, 1, 1, M_HD), lambda b, h: (b, 0, h, 0, 0)),
                      pl.BlockSpec((1, 2, 1, 1, 1), lambda b, h: (b, 0, h, 0, 0))]
        out_shape += [jax.ShapeDtypeStruct((nb, 2, M_HEADS, M_HD, M_HD), F32),
                      jax.ShapeDtypeStruct((nb, 2, M_HEADS, 1, M_HD), F32),
                      jax.ShapeDtypeStruct((nb, 2, M_HEADS, 1, 1), F32)]
    return pl.pallas_call(
        functools.partial(_mlstm_body, T=T, has_state=has_state, emit_state=emit_state),
        grid=(nb, M_HEADS),
        in_specs=in_specs,
        out_specs=out_specs,
        out_shape=out_shape,
        scratch_shapes=[pltpu.VMEM((T, M_HD), F32), pltpu.VMEM((M_HD, M_HD), F32), pltpu.VMEM((1, M_HD), F32)],
        compiler_params=_cp(("arbitrary", "arbitrary"), 40),
        name="mlstm",
    )(*args)


def _pair_masks():
    lane = lax.broadcasted_iota(I32, (1, LANES), 1)
    first = lane < N_HD
    return first, jnp.logical_not(first)


def _attn_body(q_ref, k_ref, v_ref, o_ref):
    q = q_ref[...]
    k = k_ref[...]
    v = v_ref[...]
    masks = _pair_masks()
    outs = []
    for msk in masks:
        qm = jnp.where(msk, q, jnp.zeros_like(q))
        s = _dot_nt(qm, k) * (N_HD ** -0.5)
        e = jnp.exp(s - jnp.max(s, axis=-1, keepdims=True))
        outs.append(_dot(e.astype(BF16), v) / jnp.sum(e, axis=-1, keepdims=True))
    o_ref[...] = jnp.where(masks[0], outs[0], outs[1]).astype(BF16)


def _dense_attention(big):
    cb = D // LANES
    return pl.pallas_call(
        _attn_body,
        grid=(BATCH, N_HEADS // 2),
        in_specs=[pl.BlockSpec((SEQ, LANES), lambda b, p: (b, 4 * cb + p)),
                  pl.BlockSpec((SEQ, LANES), lambda b, p: (b, 5 * cb + p)),
                  pl.BlockSpec((SEQ, LANES), lambda b, p: (b, 6 * cb + p))],
        out_specs=pl.BlockSpec((SEQ, LANES), lambda b, p: (b, p)),
        out_shape=jax.ShapeDtypeStruct((NP_TOK, D), BF16),
        compiler_params=_cp(("arbitrary", "arbitrary"), 32),
        name="dense_attn",
    )(big, big, big)


def _natten_body(q_ref, k_ref, v_ref, kc_ref, vc_ref, bias_ref, o_ref):
    rb = pl.program_id(2)
    ks = pl.multiple_of(jnp.clip(NA_QROWS * rb - MAX_WIN_H // 2, 0, GRID_H - NA_KROWS) * GRID_W, GRID_W)
    q = q_ref[...]
    kl = k_ref[pl.ds(ks, NA_KROWS * GRID_W), :]
    vl = v_ref[pl.ds(ks, NA_KROWS * GRID_W), :]
    kc = kc_ref[0, 0].astype(BF16)
    vc = vc_ref[0, 0].astype(BF16)
    masks = _pair_masks()
    outs = []
    for par, msk in enumerate(masks):
        qm = jnp.where(msk, q, jnp.zeros_like(q))
        sl = _dot_nt(qm, kl) * (N_HD ** -0.5) + bias_ref[0, par]
        sc = _dot_nt(qm, kc) * (N_HD ** -0.5)
        mx = jnp.maximum(jnp.max(sl, axis=-1, keepdims=True), jnp.max(sc, axis=-1, keepdims=True))
        el = jnp.exp(sl - mx)
        ec = jnp.exp(sc - mx)
        den = jnp.sum(el, axis=-1, keepdims=True) + jnp.sum(ec, axis=-1, keepdims=True)
        outs.append((_dot(el.astype(BF16), vl) + _dot(ec.astype(BF16), vc)) / den)
    o_ref[...] = jnp.where(masks[0], outs[0], outs[1]).astype(BF16)


def _na_pattern(rb):
    nrb = GRID_H // NA_QROWS
    return jnp.where(rb == 0, 0, jnp.where(rb == nrb - 1, 2, 1))


def _natten(big, cache_k, cache_v, bias_tab, layer):
    cb = D // LANES
    nrb = GRID_H // NA_QROWS
    qrows = NA_QROWS * GRID_W
    q0 = NP_TOK // qrows
    b0 = NP_TOK // DEC_SEQ
    return pl.pallas_call(
        _natten_body,
        grid=(DEC_BATCH, N_HEADS // 2, nrb),
        in_specs=[pl.BlockSpec((qrows, LANES), lambda b, p, r: (q0 + b * nrb + r, 4 * cb + p)),
                  pl.BlockSpec((DEC_SEQ, LANES), lambda b, p, r: (b0 + b, 5 * cb + p)),
                  pl.BlockSpec((DEC_SEQ, LANES), lambda b, p, r: (b0 + b, 6 * cb + p)),
                  pl.BlockSpec((1, 1, PAST_LEN, LANES), lambda b, p, r: (b, layer, 0, p)),
                  pl.BlockSpec((1, 1, PAST_LEN, LANES), lambda b, p, r: (b, layer, 0, p)),
                  pl.BlockSpec((1, 2, qrows, NA_KROWS * GRID_W), lambda b, p, r: (_na_pattern(r), p, 0, 0))],
        out_specs=pl.BlockSpec((qrows, LANES), lambda b, p, r: (b * nrb + r, p)),
        out_shape=jax.ShapeDtypeStruct((NS_TOK, D), BF16),
        compiler_params=_cp(("arbitrary", "arbitrary", "arbitrary"), 40),
        name="natten",
    )(big, big, big, cache_k, cache_v, bias_tab)


def _na_bias_table(rpb):
    nrb = GRID_H // NA_QROWS
    tabs = []
    for rb in (0, 1, nrb - 1):
        ks = int(np.clip(NA_QROWS * rb - MAX_WIN_H // 2, 0, GRID_H - NA_KROWS))
        qi = np.arange(NA_QROWS * GRID_W)
        ki = np.arange(NA_KROWS * GRID_W)
        qrow = (NA_QROWS * rb + qi // GRID_W)[:, None]
        qcol = (qi % GRID_W)[:, None]
        krow = (ks + ki // GRID_W)[None, :]
        kcol = (ki % GRID_W)[None, :]
        rstart = np.clip(qrow - MAX_WIN_H // 2, 0, GRID_H - MAX_WIN_H)
        cstart = np.clip(qcol - WIN_W // 2, 0, GRID_W - WIN_W)
        valid = (krow >= rstart) & (krow < rstart + MAX_WIN_H) & (kcol >= cstart) & (kcol < cstart + WIN_W)
        dr = np.clip(krow - qrow + MAX_WIN_H - 1, 0, 2 * MAX_WIN_H - 2)
        dc = np.clip(kcol - qcol + WIN_W - 1, 0, 2 * WIN_W - 2)
        dr, dc = np.broadcast_arrays(dr, dc)
        tabs.append(jnp.where(valid[None], rpb[:, dr, dc].astype(F32), NEG))
    return jnp.stack(tabs)


def _postmix_body(oa_ref, ob_ref, ga_ref, gb_ref, x_ref, g1_ref, nf_ref, sc2_ref, sh2_ref,
                  wa_ref, wb_ref, wo_ref, wr_ref, xo_ref, h2_ref, aff_ref):
    a = _dot(oa_ref[...], wa_ref[...])
    b = _dot(ob_ref[...], wb_ref[...])
    merged = _sigmoid(ga_ref[...].astype(F32)) * a + _sigmoid(gb_ref[...].astype(F32)) * b
    xn = x_ref[...] + g1_ref[0] * _dot(merged.astype(BF16), wo_ref[...])
    xo_ref[...] = xn
    h2 = (_rms(xn) * nf_ref[...]) * (1.0 + sc2_ref[0]) + sh2_ref[0]
    for s in range(SLAB):
        h2_ref[:, s, :] = h2[:, s * LANES:(s + 1) * LANES]
    h1, h2m, h3 = _split3(h2)
    w1, w2, w3 = _split3(wr_ref[...])
    lg = (_dot_nt(w1, h1) + _dot_nt(w1, h2m) + _dot_nt(w2, h1)
          + _dot_nt(w1, h3) + _dot_nt(w3, h1) + _dot_nt(w2, h2m))
    e = jnp.exp(lg - jnp.max(lg, axis=0, keepdims=True))
    aff_ref[...] = e / jnp.sum(e, axis=0, keepdims=True)


def _postmix(out_a, out_b, big, x, mod, norm_ffn, wa, wb, wo, wr_t):
    row = lambda i: (i, 0)
    const = lambda i: (0, 0)
    return pl.pallas_call(
        _postmix_body,
        grid=(N_TILES,),
        in_specs=[pl.BlockSpec((TM, D), row), pl.BlockSpec((TM, D), row),
                  pl.BlockSpec((TM, D), lambda i: (i, 7)), pl.BlockSpec((TM, D), lambda i: (i, 8)),
                  pl.BlockSpec((TM, D), row),
                  _mod_spec(2), pl.BlockSpec((1, D), const), _mod_spec(4), _mod_spec(3),
                  pl.BlockSpec((D, D), const), pl.BlockSpec((D, D), const), pl.BlockSpec((D, D), const),
                  pl.BlockSpec((N_EXPERTS, D), const)],
        out_specs=[pl.BlockSpec((TM, D), row), pl.BlockSpec((TM, SLAB, LANES), lambda i: (i, 0, 0)),
                   pl.BlockSpec((N_EXPERTS, TM), lambda i: (0, i))],
        out_shape=[jax.ShapeDtypeStruct((N_TOK, D), F32), jax.ShapeDtypeStruct((N_TOK, SLAB, LANES), F32),
                   jax.ShapeDtypeStruct((N_EXPERTS, N_TOK), F32)],
        compiler_params=_cp(("arbitrary",), 48),
        name="postmix",
    )(out_a, out_b, big, big, x, mod, norm_ffn, mod, mod, wa, wb, wo, wr_t)


def _route_body(a_ref, idx_ref, gv_ref, off_ref, pos_ref, *, R, cap):
    a = a_ref[...]
    bits = pltpu.bitcast(a, I32)
    capf = float(cap)

    def count(mask):
        c = jnp.sum(jnp.where(mask, 1.0, 0.0), axis=1, keepdims=True)
        return jnp.sum(c, axis=2, keepdims=True)

    def search(i, lo):
        cand = lo | jnp.left_shift(jnp.int32(1), 30 - i)
        return jnp.where(count(bits >= cand) >= capf, cand, lo)

    tau = lax.fori_loop(0, 31, search, jnp.zeros((N_EXPERTS, 1, 1), I32))
    need = capf - count(bits > tau)

    l0 = lax.broadcasted_iota(I32, (LANES, LANES), 0)
    l1 = lax.broadcasted_iota(I32, (LANES, LANES), 1)
    upper = jnp.where(l0 <= l1, 1.0, 0.0).astype(BF16)
    r0 = lax.broadcasted_iota(I32, (R, R), 0)
    r1 = lax.broadcasted_iota(I32, (R, R), 1)
    below = jnp.where(r1 < r0, 1.0, 0.0).astype(BF16)
    eye_r = r0 == r1
    r_row = lax.broadcasted_iota(I32, (1, R), 1).astype(F32)
    lane_row = lax.broadcasted_iota(I32, (1, LANES), 1).astype(F32)
    jcol = lax.broadcasted_iota(I32, (cap, 1), 0).astype(F32)

    def prefix(x):
        within = _dot(x.astype(BF16), upper)
        tot = jnp.broadcast_to(within[:, LANES - 1:LANES], (R, LANES))
        offs = _dot(below, tot.astype(BF16))
        return within + offs, offs

    for e in range(N_EXPERTS):
        gt = jnp.where(bits[e] > tau[e], 1.0, 0.0)
        eq = jnp.where(bits[e] == tau[e], 1.0, 0.0)
        cin_eq, _ = prefix(eq)
        sel = gt + eq * jnp.where(cin_eq - eq < need[e], 1.0, 0.0)
        cin, offs = prefix(sel)
        cend_row = jnp.sum(jnp.where(eye_r, cin[:, LANES - 1:LANES], 0.0), axis=0, keepdims=True)
        bcol = jnp.sum(jnp.where(cend_row <= jcol, 1.0, 0.0), axis=1, keepdims=True)
        onehot = jnp.where(bcol == r_row, 1.0, 0.0).astype(BF16)
        lcol = jnp.sum(jnp.where(_dot_x3(onehot, cin) <= jcol, 1.0, 0.0), axis=1, keepdims=True)
        idx_ref[e] = (bcol * LANES + lcol).astype(I32)
        gv_ref[e] = jnp.sum(jnp.where(lane_row == lcol, _dot_x3(onehot, a[e]), 0.0), axis=1, keepdims=True)
        off_ref[e] = offs[:, 0:1]
        pos_ref[e] = jnp.where(sel > 0.0, cin - 1.0, -1.0)


def _route(aff3, cap):
    R = aff3.shape[1]
    full = lambda s: pl.BlockSpec(s, lambda i: (0, 0, 0))
    return pl.pallas_call(
        functools.partial(_route_body, R=R, cap=cap),
        grid=(1,),
        in_specs=[full((N_EXPERTS, R, LANES))],
        out_specs=[full((N_EXPERTS, cap, 1)), full((N_EXPERTS, cap, 1)), full((N_EXPERTS, R, 1)),
                   full((N_EXPERTS, R, LANES))],
        out_shape=[jax.ShapeDtypeStruct((N_EXPERTS, cap, 1), I32),
                   jax.ShapeDtypeStruct((N_EXPERTS, cap, 1), F32),
                   jax.ShapeDtypeStruct((N_EXPERTS, R, 1), F32),
                   jax.ShapeDtypeStruct((N_EXPERTS, R, LANES), F32)],
        compiler_params=_cp(("arbitrary",), 48),
        name="route",
    )(aff3)


def _gather_copy(h_hbm, xs_ref, sem, src_row, dst_row, nrows):
    return pltpu.make_async_copy(h_hbm.at[pl.ds(src_row, nrows)], xs_ref.at[pl.ds(dst_row, nrows)], sem)


def _ffn_body(idx_ref, h_hbm, wg_ref, wu_ref, wd_ref, gv_ref, ys_ref, xs_ref, xb_ref, sem):
    e = pl.program_id(0)
    f = pl.program_id(1)

    @pl.when(f == 0)
    def _():
        def issue(j, carry):
            _gather_copy(h_hbm, xs_ref, sem, idx_ref[e * CAP_TOT + j], j, 1).start()
            return carry
        lax.fori_loop(0, CAP_TOT, issue, 0)
        _gather_copy(h_hbm, xs_ref, sem, 0, 0, CAP_TOT).wait()
        for s in range(SLAB):
            xb_ref[:, s * LANES:(s + 1) * LANES] = xs_ref[:, s, :].astype(BF16)

    xb = xb_ref[...]
    g = _dot(xb, wg_ref[0].astype(BF16))
    u = _dot(xb, wu_ref[0].astype(BF16))
    hid = (g * _sigmoid(g) * u).astype(BF16)
    y = _dot(hid, wd_ref[0].astype(BF16))

    @pl.when(f == 0)
    def _():
        _slab_accumulate(ys_ref, y, first=True)

    @pl.when(f > 0)
    def _():
        _slab_accumulate(ys_ref, y, first=False)

    @pl.when(f == pl.num_programs(1) - 1)
    def _():
        gv = gv_ref[0]
        for s in range(SLAB):
            ys_ref[0, :, s, :] = ys_ref[0, :, s, :] * gv


def _slab_accumulate(ys_ref, y, *, first):
    for s in range(SLAB):
        ys = y[:, s * LANES:(s + 1) * LANES]
        ys_ref[0, :, s, :] = ys if first else ys_ref[0, :, s, :] + ys


def _ffn(idx_flat, h2, wg, wu, wd, gv):
    nf = EXPERT_FF // FF_CHUNK
    return pl.pallas_call(
        _ffn_body,
        grid_spec=pltpu.PrefetchScalarGridSpec(
            num_scalar_prefetch=1,
            grid=(N_EXPERTS, nf),
            in_specs=[pl.BlockSpec(memory_space=pl.ANY),
                      pl.BlockSpec((1, D, FF_CHUNK), lambda e, f, idx: (e, 0, f)),
                      pl.BlockSpec((1, D, FF_CHUNK), lambda e, f, idx: (e, 0, f)),
                      pl.BlockSpec((1, FF_CHUNK, D), lambda e, f, idx: (e, f, 0)),
                      pl.BlockSpec((1, CAP_TOT, 1), lambda e, f, idx: (e, 0, 0))],
            out_specs=pl.BlockSpec((1, CAP_TOT, SLAB, LANES), lambda e, f, idx: (e, 0, 0, 0)),
            scratch_shapes=[pltpu.VMEM((CAP_TOT, SLAB, LANES), F32), pltpu.VMEM((CAP_TOT, D), BF16),
                            pltpu.SemaphoreType.DMA(())]),
        out_shape=jax.ShapeDtypeStruct((N_EXPERTS, CAP_TOT, SLAB, LANES), F32),
        compiler_params=_cp(("arbitrary", "arbitrary"), 58),
        name="expert_ffn",
    )(idx_flat, h2, wg, wu, wd, gv)


MAX_PAIRS = N_EXPERTS * TM
SEG_BITS = TM.bit_length()


def _seg_copy(ys_hbm, w_ref, sem, e, src_row, dst_row, nrows):
    return pltpu.make_async_copy(ys_hbm.at[e, pl.ds(src_row, nrows)], w_ref.at[pl.ds(dst_row, nrows)], sem)


def _combine_body(lo_ref, hi_ref, ys_hbm, pos_ref, x_ref, g2_ref, nf_ref, o_ref, w_ref, acc_ref, sem, *, last):
    b = pl.program_id(0)
    lane = lax.broadcasted_iota(I32, (1, LANES), 1)
    off = jnp.int32(0)
    shift = jnp.zeros((1, LANES), F32)
    for e in range(N_EXPERTS):
        lo = lo_ref[e * N_TILES + b]
        n = hi_ref[e * N_TILES + b] - lo
        for bit in range(SEG_BITS - 1, -1, -1):
            done = (n >> (bit + 1)) << (bit + 1)

            @pl.when((n & (1 << bit)) != 0)
            def _(e=e, lo=lo, off=off, done=done, bit=bit):
                _seg_copy(ys_hbm, w_ref, sem, e, lo + done, off + done, 1 << bit).start()
        shift = jnp.where(lane == e, (off - lo).astype(F32), shift)
        off = off + n
    npairs = off
    for bit in range(MAX_PAIRS.bit_length() - 1, -1, -1):
        @pl.when((npairs & (1 << bit)) != 0)
        def _(bit=bit):
            _seg_copy(ys_hbm, w_ref, sem, 0, 0, 0, 1 << bit).wait()

    acc_ref[...] = jnp.zeros_like(acc_ref)
    pos = pos_ref[...]
    prow = jnp.where(pos >= 0.0, pos + shift[:, 0:N_EXPERTS], -1.0)
    sub = lax.broadcasted_iota(I32, (TM, 1), 0)
    pair = lax.broadcasted_iota(I32, (1, TM), 1).astype(F32)

    def body(c, carry):
        rows = pl.ds(pl.multiple_of(c * TM, TM), TM)
        valid = (c * TM + sub) < npairs
        local = prow - (c * TM).astype(F32)
        onehot = jnp.zeros((TM, TM), F32)
        for e in range(N_EXPERTS):
            onehot = onehot + jnp.where(local[:, e:e + 1] == pair, 1.0, 0.0)
        onehot = onehot.astype(BF16)
        for s in range(SLAB):
            data = jnp.where(valid, w_ref[rows, s, :], 0.0)
            sl = slice(s * LANES, (s + 1) * LANES)
            acc_ref[:, sl] = acc_ref[:, sl] + _dot_x3(onehot, data)
        return carry

    lax.fori_loop(0, (npairs + TM - 1) // TM, body, 0)
    xn = x_ref[...] + g2_ref[0] * acc_ref[...]
    if last:
        xn = _rms(xn) * nf_ref[...]
    o_ref[...] = xn


def _combine(seg_lo, seg_hi, ys, pos_t, x, mod, norm_final, *, last):
    return pl.pallas_call(
        functools.partial(_combine_body, last=last),
        grid_spec=pltpu.PrefetchScalarGridSpec(
            num_scalar_prefetch=2,
            grid=(N_TILES,),
            in_specs=[pl.BlockSpec(memory_space=pl.ANY),
                      pl.BlockSpec((TM, N_EXPERTS), lambda i, lo, hi: (i, 0)),
                      pl.BlockSpec((TM, D), lambda i, lo, hi: (i, 0)),
                      pl.BlockSpec((1, 1, D), lambda i, lo, hi: (_mod_row(i), 0, 5)),
                      pl.BlockSpec((1, D), lambda i, lo, hi: (0, 0))],
            out_specs=pl.BlockSpec((TM, D), lambda i, lo, hi: (i, 0)),
            scratch_shapes=[pltpu.VMEM((MAX_PAIRS, SLAB, LANES), F32), pltpu.VMEM((TM, D), F32),
                            pltpu.SemaphoreType.DMA(())]),
        out_shape=jax.ShapeDtypeStruct((N_TOK, D), F32),
        compiler_params=_cp(("arbitrary",), 48),
        name="combine",
    )(seg_lo, seg_hi, ys, pos_t, x, mod, norm_final)


def _tile_segments(off_p, off_s):
    per = TM // LANES
    lo_p = off_p[:, ::per, 0].astype(I32)
    lo_s = off_s[:, ::per, 0].astype(I32) + CAP_P
    hi_p = jnp.concatenate([lo_p[:, 1:], jnp.full((N_EXPERTS, 1), CAP_P, I32)], axis=1)
    hi_s = jnp.concatenate([lo_s[:, 1:], jnp.full((N_EXPERTS, 1), CAP_TOT, I32)], axis=1)
    lo = jnp.concatenate([lo_p, lo_s], axis=1)
    hi = jnp.concatenate([hi_p, hi_s], axis=1)
    return lo.reshape(-1), hi.reshape(-1)


def kernel(x_prompt, x_sample, cache_na_k, cache_na_v, state_mlstm_C, state_mlstm_n, state_mlstm_m, c, c_ctx, w_ada, b_ada, norm_mix, norm_ffn, w_in, conv_qk, mlstm_gate_bias, mlstm_head_norm, na_rpb, w_branch_a, w_branch_b, w_out, w_router, w_expert_gate, w_expert_up, w_expert_down, norm_final):
    depth = w_in.shape[0]
    m_width = M_HEADS * M_HD
    n_gates = 4 * M_HEADS

    x = jnp.concatenate([x_prompt.reshape(NP_TOK, D), x_sample.reshape(NS_TOK, D)], axis=0)
    cond8 = jnp.concatenate([c_ctx[None], c, jnp.zeros((8 - 1 - DEC_BATCH, D), F32)], axis=0)
    mods = _adaln(cond8, w_ada, b_ada).reshape(depth, 8, 1, 6 * D)
    cos_t, sin_t = _rope_tables()
    cache_k = cache_na_k.reshape(DEC_BATCH, depth, PAST_LEN, D)
    cache_v = cache_na_v.reshape(DEC_BATCH, depth, PAST_LEN, D)
    norm_final2 = norm_final.reshape(1, D)

    ks_, vs_, cs_, ns_, ms_ = [], [], [], [], []
    for l in range(depth):
        mod = mods[l]
        w_big = jnp.concatenate([w_in[l][:, :4 * m_width], w_in[l][:, 4 * m_width + n_gates:]], axis=1).astype(BF16)
        w_gates = jnp.pad(w_in[l][:, 4 * m_width:4 * m_width + n_gates], ((0, 0), (0, LANES - n_gates))).astype(BF16)
        gate_bias = jnp.pad(mlstm_gate_bias[l], (0, LANES - n_gates)).reshape(1, LANES)
        head_norm = mlstm_head_norm[l].reshape(1, m_width)

        big, gates, kv = _inproj(x, norm_mix[l].reshape(1, D), mod, w_big, w_gates)

        qk_p = _convprep(big, conv_qk[l], cos_t, sin_t, rope=False, nb=BATCH, T=SEQ, row0=0)
        qk_s = _convprep(big, conv_qk[l], cos_t, sin_t, rope=True, nb=DEC_BATCH, T=DEC_SEQ, row0=NP_TOK // DEC_SEQ)
        oa_p, c_new, n_new, m_new = _mlstm(qk_p, big, gates, gate_bias, head_norm, None,
                                           nb=BATCH, T=SEQ, row0=0, emit_state=True)
        state = (state_mlstm_C[:, l], state_mlstm_n[:, l].reshape(DEC_BATCH, 2, M_HEADS, 1, M_HD),
                 state_mlstm_m[:, l].reshape(DEC_BATCH, 2, M_HEADS, 1, 1))
        (oa_s,) = _mlstm(qk_s, big, gates, gate_bias, head_norm, state,
                         nb=DEC_BATCH, T=DEC_SEQ, row0=NP_TOK // DEC_SEQ, emit_state=False)
        out_a = jnp.concatenate([oa_p, oa_s], axis=0)

        ob_p = _dense_attention(big)
        ob_s = _natten(big, cache_k, cache_v, _na_bias_table(na_rpb[l]), l)
        out_b = jnp.concatenate([ob_p, ob_s], axis=0)

        x, h2, aff_t = _postmix(out_a, out_b, big, x, mod, norm_ffn[l].reshape(1, D),
                                w_branch_a[l].astype(BF16), w_branch_b[l].astype(BF16), w_out[l].astype(BF16),
                                w_router[l].T)

        idx_p, gv_p, off_p, pos_p = _route(aff_t[:, :NP_TOK].reshape(N_EXPERTS, NP_TOK // LANES, LANES), CAP_P)
        idx_s, gv_s, off_s, pos_s = _route(aff_t[:, NP_TOK:].reshape(N_EXPERTS, NS_TOK // LANES, LANES), CAP_S)
        idx = jnp.concatenate([idx_p, idx_s + NP_TOK], axis=1)
        gv = jnp.concatenate([gv_p, gv_s], axis=1)
        pos_s = jnp.where(pos_s >= 0.0, pos_s + CAP_P, pos_s)
        pos_t = jnp.concatenate([pos_p.reshape(N_EXPERTS, NP_TOK), pos_s.reshape(N_EXPERTS, NS_TOK)], axis=1).T
        ys = _ffn(idx.reshape(-1), h2, w_expert_gate[l], w_expert_up[l], w_expert_down[l], gv)
        seg_lo, seg_hi = _tile_segments(off_p, off_s)
        x = _combine(seg_lo, seg_hi, ys, pos_t, x, mod, norm_final2, last=(l == depth - 1))

        ks_.append(kv[:NP_TOK, :D].reshape(BATCH, SEQ, N_HEADS, N_HD))
        vs_.append(kv[:NP_TOK, D:].reshape(BATCH, SEQ, N_HEADS, N_HD))
        cs_.append(c_new)
        ns_.append(n_new.reshape(BATCH, 2, M_HEADS, M_HD))
        ms_.append(m_new.reshape(BATCH, 2, M_HEADS))

    y_prompt = x[:NP_TOK].reshape(BATCH, SEQ, D)
    y_sample = x[NP_TOK:].reshape(DEC_BATCH, DEC_SEQ, D)
    return (y_prompt, y_sample, jnp.stack(ks_, axis=1), jnp.stack(vs_, axis=1),
            jnp.stack(cs_, axis=1), jnp.stack(ns_, axis=1), jnp.stack(ms_, axis=1))
```

```python
import functools

import numpy as np
import jax
import jax.numpy as jnp
from jax import lax
from jax.experimental import pallas as pl
from jax.experimental.pallas import tpu as pltpu

F32 = jnp.float32
BF16 = jnp.bfloat16
I32 = jnp.int32

D = 1024
BATCH, SEQ = 16, 256
DEC_BATCH, DEC_SEQ = 4, 2048
PAST_LEN = 512
NP_TOK = BATCH * SEQ
NS_TOK = DEC_BATCH * DEC_SEQ
N_TOK = NP_TOK + NS_TOK
GRID_W = 64
GRID_H = DEC_SEQ // GRID_W
M_HEADS, M_HD = 4, 256
CHUNK = 128
N_HEADS, N_HD = 16, 64
MAX_WIN_H, WIN_W = 8, 16
N_EXPERTS, EXPERT_FF = 16, 2048
CAP_P = 2 * NP_TOK // N_EXPERTS
CAP_S = 2 * NS_TOK // N_EXPERTS
CAP_TOT = CAP_P + CAP_S
ROPE_BASE = 10000.0
EPS = 1e-6
NEG = -1e30
MIN_NORMAL_BITS = 0x00800000

TM = 256
N_TILES = N_TOK // TM
NP_TILES = NP_TOK // TM
LANES = 128
NA_QROWS = 4
NA_KROWS = 12
FF_CHUNK = 512
SLAB = D // LANES


def _cp(sem, vmem_mb):
    return pltpu.CompilerParams(dimension_semantics=sem, vmem_limit_bytes=vmem_mb * 2 ** 20)


def _dot(a, b):
    return jnp.dot(a, b, preferred_element_type=F32)


def _dot_nt(a, b):
    return lax.dot_general(a, b, (((1,), (1,)), ((), ())), preferred_element_type=F32)


def _dot_tn(a, b):
    return lax.dot_general(a, b, (((0,), (0,)), ((), ())), preferred_element_type=F32)


def _split3(x):
    hi = x.astype(BF16)
    r = x - hi.astype(F32)
    mid = r.astype(BF16)
    lo = (r - mid.astype(F32)).astype(BF16)
    return hi, mid, lo


def _dot_x3(a_bf, x):
    hi, mid, lo = _split3(x)
    return _dot(a_bf, hi) + _dot(a_bf, mid) + _dot(a_bf, lo)


def _sigmoid(x):
    return 1.0 / (1.0 + jnp.exp(-x))


def _rms(x):
    return x * lax.rsqrt(jnp.mean(x * x, axis=-1, keepdims=True) + EPS)


def _mod_row(i):
    return jnp.where(i < NP_TILES, 0, 1 + (i - NP_TILES) // (DEC_SEQ // TM))


def _mod_spec(kind):
    return pl.BlockSpec((1, 1, D), lambda i: (_mod_row(i), 0, kind))


def _adaln_body(c_ref, w_ref, b_ref, o_ref):
    c = c_ref[...]
    s = (c * _sigmoid(c)).astype(BF16)
    o_ref[0] = _dot(s, w_ref[0].astype(BF16)) + b_ref[0]


def _adaln(cond8, w_ada, b_ada):
    depth = w_ada.shape[0]
    tn = 1536
    return pl.pallas_call(
        _adaln_body,
        grid=(depth, 6 * D // tn),
        in_specs=[pl.BlockSpec((8, D), lambda l, j: (0, 0)),
                  pl.BlockSpec((1, D, tn), lambda l, j: (l, 0, j)),
                  pl.BlockSpec((1, 1, tn), lambda l, j: (l, 0, j))],
        out_specs=pl.BlockSpec((1, 8, tn), lambda l, j: (l, 0, j)),
        out_shape=jax.ShapeDtypeStruct((depth, 8, 6 * D), F32),
        compiler_params=_cp(("arbitrary", "arbitrary"), 40),
        name="adaln",
    )(cond8, w_ada, b_ada.reshape(depth, 1, 6 * D))


N_GATES = 4 * M_HEADS
GATE_COL = 4 * M_HEADS * M_HD


def _pack_body(w_ref, big_ref, gates_ref):
    w = w_ref[...]
    big_ref[:, 0:GATE_COL] = w[:, 0:GATE_COL].astype(BF16)
    big_ref[:, GATE_COL:9 * D] = w[:, GATE_COL + N_GATES:9 * D + N_GATES].astype(BF16)
    lane = lax.broadcasted_iota(I32, (1, LANES), 1)
    gates_ref[...] = jnp.where(lane < N_GATES, w[:, GATE_COL:GATE_COL + LANES], 0.0).astype(BF16)


def _pack_w_in(w):
    tr = 128
    return pl.pallas_call(
        _pack_body,
        grid=(D // tr,),
        in_specs=[pl.BlockSpec((tr, w.shape[1]), lambda i: (i, 0))],
        out_specs=[pl.BlockSpec((tr, 9 * D), lambda i: (i, 0)), pl.BlockSpec((tr, LANES), lambda i: (i, 0))],
        out_shape=[jax.ShapeDtypeStruct((D, 9 * D), BF16), jax.ShapeDtypeStruct((D, LANES), BF16)],
        compiler_params=_cp(("arbitrary",), 48),
        name="pack_w_in",
    )(w)


def _inproj_body(x_ref, g_ref, sc_ref, sh_ref, w_ref, wg_ref, big_ref, gates_ref, kv_ref):
    i = pl.program_id(0)
    h = (_rms(x_ref[...]) * g_ref[...]) * (1.0 + sc_ref[0]) + sh_ref[0]
    hb = h.astype(BF16)
    gates_ref[...] = _dot(hb, wg_ref[...])
    for c in range(9):
        r = _dot(hb, w_ref[:, c * D:(c + 1) * D])
        big_ref[:, c * D:(c + 1) * D] = r.astype(BF16)
        if c in (5, 6):
            @pl.when(i < NP_TILES)
            def _():
                kv_ref[:, (c - 5) * D:(c - 4) * D] = r

    @pl.when(i == NP_TILES)
    def _():
        kv_ref[...] = jnp.zeros_like(kv_ref)


def _inproj(x, norm_g, mod, w_big, w_gates):
    return pl.pallas_call(
        _inproj_body,
        grid=(N_TILES,),
        in_specs=[pl.BlockSpec((TM, D), lambda i: (i, 0)),
                  pl.BlockSpec((1, D), lambda i: (0, 0)),
                  _mod_spec(1), _mod_spec(0),
                  pl.BlockSpec((D, 9 * D), lambda i: (0, 0), pipeline_mode=pl.Buffered(1)),
                  pl.BlockSpec((D, LANES), lambda i: (0, 0))],
        out_specs=[pl.BlockSpec((TM, 9 * D), lambda i: (i, 0)),
                   pl.BlockSpec((TM, LANES), lambda i: (i, 0)),
                   pl.BlockSpec((TM, 2 * D), lambda i: (jnp.minimum(i, NP_TILES), 0))],
        out_shape=[jax.ShapeDtypeStruct((N_TOK, 9 * D), BF16),
                   jax.ShapeDtypeStruct((N_TOK, LANES), F32),
                   jax.ShapeDtypeStruct((NP_TOK + TM, 2 * D), F32)],
        compiler_params=_cp(("arbitrary",), 56),
        name="inproj",
    )(x, norm_g, mod, mod, w_big, w_gates)


def _conv_body(x_ref, w_ref, cos_ref, sin_ref, o_ref, *, rope, T):
    j = pl.program_id(1)
    x = x_ref[...].astype(F32)
    w = w_ref[...]
    row = lax.broadcasted_iota(I32, (T, 1), 0)
    xp = jnp.where(row == 0, 0.0, pltpu.roll(x, 1, 0))
    xn = jnp.where(row == T - 1, 0.0, pltpu.roll(x, T - 1, 0))
    y = xp * w[0:1] + x * w[1:2] + xn * w[2:3]
    y = y * _sigmoid(y)
    scale = jnp.where(j >= M_HEADS, M_HD ** -0.5, 1.0)
    for hlf in range(2):
        sl = slice(hlf * LANES, (hlf + 1) * LANES)
        yh = y[:, sl]
        if rope:
            yh = yh * cos_ref[:, sl] + pltpu.roll(yh, LANES // 2, 1) * sin_ref[:, sl]
        o_ref[:, sl] = (yh * scale).astype(BF16)


def _convprep(big, conv_w, cos_t, sin_t, *, rope, nb, T, row0):
    return pl.pallas_call(
        functools.partial(_conv_body, rope=rope, T=T),
        grid=(nb, 2 * M_HEADS),
        in_specs=[pl.BlockSpec((T, M_HD), lambda b, j: (row0 + b, j)),
                  pl.BlockSpec((3, M_HD), lambda b, j: (0, j)),
                  pl.BlockSpec((T, M_HD), lambda b, j: (0, 0)),
                  pl.BlockSpec((T, M_HD), lambda b, j: (0, 0))],
        out_specs=pl.BlockSpec((T, M_HD), lambda b, j: (b, j)),
        out_shape=jax.ShapeDtypeStruct((nb * T, 2 * M_HEADS * M_HD), BF16),
        compiler_params=_cp(("arbitrary", "arbitrary"), 48),
        name="convprep",
    )(big, conv_w, cos_t, sin_t)


def _rope_tables():
    pos = np.arange(DEC_SEQ)
    rows = (pos // GRID_W).astype(np.float32)
    cols = (pos % GRID_W).astype(np.float32)
    nfreq = M_HD // 4
    inv = (ROPE_BASE ** (-np.arange(nfreq, dtype=np.float32) / nfreq)).astype(np.float32)
    d = np.arange(M_HD)
    p = np.where(d[None, :] < M_HD // 2, rows[:, None], cols[:, None]).astype(np.float32)
    ang = (p * inv[d % nfreq][None, :]).astype(np.float32)
    sign = np.where((d % (M_HD // 2)) < nfreq, -1.0, 1.0).astype(np.float32)
    return jnp.asarray(np.cos(ang), F32), jnp.asarray(np.sin(ang) * sign[None, :], F32)


def _mlstm_body(*refs, T, has_state, emit_state):
    q_ref, k_ref, v_ref, om_ref, g_ref, gb_ref, hn_ref = refs[:7]
    pos = 7
    if has_state:
        c0_ref, n0_ref, m0_ref = refs[pos:pos + 3]
        pos += 3
    oa_ref = refs[pos]
    pos += 1
    if emit_state:
        co_ref, no_ref, mo_ref = refs[pos:pos + 3]
        pos += 3
    hs_ref, c_ref, n_ref = refs[pos:pos + 3]

    hd = pl.program_id(1)
    nc = T // CHUNK
    lane = lax.broadcasted_iota(I32, (1, LANES), 1)
    r_i = lax.broadcasted_iota(I32, (CHUNK, CHUNK), 0)
    c_i = lax.broadcasted_iota(I32, (CHUNK, CHUNK), 1)
    eye = r_i == c_i

    def to_row(col):
        return jnp.sum(jnp.where(eye, col, 0.0), axis=0, keepdims=True)

    def pick(mat, colidx):
        return jnp.sum(jnp.where(lane == colidx, mat, 0.0), axis=1, keepdims=True)

    for d in range(2):
        causal = (r_i >= c_i) if d == 0 else (r_i <= c_i)
        cum = jnp.where(causal, 1.0, 0.0).astype(BF16)
        if has_state:
            c_ref[...] = c0_ref[0, d, 0]
            n_ref[...] = n0_ref[0, d, 0]
            m_init = m0_ref[0, d, 0]
        else:
            c_ref[...] = jnp.zeros_like(c_ref)
            n_ref[...] = jnp.zeros_like(n_ref)
            m_init = jnp.zeros((1, 1), F32)

        def body(ci, m_prev, d=d, causal=causal, cum=cum):
            c = ci if d == 0 else nc - 1 - ci
            rows = pl.ds(pl.multiple_of(c * CHUNK, CHUNK), CHUNK)
            g = g_ref[rows, :] + gb_ref[...]
            lf = jnp.minimum(g, 0.0) - jnp.log(1.0 + jnp.exp(-jnp.abs(g)))
            bmat = _dot_x3(cum, lf)
            ig_col = pick(g, d * M_HEADS + hd)
            b_col = pick(bmat, 2 * M_HEADS + d * M_HEADS + hd)
            ig_row = to_row(ig_col)
            b_row = to_row(b_col)
            b_last = b_row[:, CHUNK - 1:CHUNK] if d == 0 else b_row[:, 0:1]
            logd = jnp.where(causal, b_col - b_row + ig_row, -jnp.inf)
            inter = b_col + m_prev
            m_row = jnp.maximum(inter, jnp.max(logd, axis=1, keepdims=True))
            dmat = jnp.exp(logd - m_row)
            s_inter = jnp.exp(inter - m_row)
            q = q_ref[rows, :]
            k = k_ref[rows, :]
            v = v_ref[rows, :]
            s = _dot_nt(q, k) * dmat
            num = _dot(s.astype(BF16), v) + s_inter * _dot(q, c_ref[...].astype(BF16))
            den = (jnp.sum(s, axis=1, keepdims=True)
                   + s_inter * jnp.sum(q.astype(F32) * n_ref[...], axis=1, keepdims=True))
            hh = num / jnp.maximum(jnp.abs(den), jnp.exp(-m_row))
            if d == 0:
                hs_ref[rows, :] = hh
            else:
                hs_ref[rows, :] = hs_ref[rows, :] + hh
            log_w = b_last - b_col + ig_col
            m_new = jnp.maximum(b_last + m_prev, jnp.max(log_w, axis=0, keepdims=True))
            w = jnp.exp(log_w - m_new)
            decay = jnp.exp(b_last + m_prev - m_new)
            kw = k.astype(F32) * w
            c_ref[...] = decay * c_ref[...] + _dot_tn(kw.astype(BF16), v)
            n_ref[...] = decay * n_ref[...] + jnp.sum(kw, axis=0, keepdims=True)
            return m_new

        m_fin = lax.fori_loop(0, nc, body, m_init)
        if emit_state:
            co_ref[0, d, 0] = c_ref[...]
            no_ref[0, d, 0] = n_ref[...]
            mo_ref[0, d, 0] = m_fin

    hm = _rms(hs_ref[...]) * hn_ref[...]
    oa_ref[...] = (_sigmoid(om_ref[...].astype(F32)) * hm).astype(BF16)


def _mlstm(qk, big, gates, gate_bias, head_norm, state, *, nb, T, row0, emit_state):
    has_state = state is not None
    in_specs = [pl.BlockSpec((T, M_HD), lambda b, h: (b, h)),
                pl.BlockSpec((T, M_HD), lambda b, h: (b, M_HEADS + h)),
                pl.BlockSpec((T, M_HD), lambda b, h: (row0 + b, 2 * M_HEADS + h)),
                pl.BlockSpec((T, M_HD), lambda b, h: (row0 + b, 3 * M_HEADS + h)),
                pl.BlockSpec((T, LANES), lambda b, h: (row0 + b, 0)),
                pl.BlockSpec((1, LANES), lambda b, h: (0, 0)),
                pl.BlockSpec((1, M_HD), lambda b, h: (0, h))]
    args = [qk, qk, big, big, gates, gate_bias, head_norm]
    if has_state:
        c0, n0, m0 = state
        in_specs += [pl.BlockSpec((1, 2, 1, M_HD, M_HD), lambda b, h: (b, 0, h, 0, 0)),
                     pl.BlockSpec((1, 2, 1, 1, M_HD), lambda b, h: (b, 0, h, 0, 0)),
                     pl.BlockSpec((1, 2, 1, 1, 1), lambda b, h: (b, 0, h, 0, 0))]
        args += [c0, n0, m0]
    out_specs = [pl.BlockSpec((T, M_HD), lambda b, h: (b, h))]
    out_shape = [jax.ShapeDtypeStruct((nb * T, M_HEADS * M_HD), BF16)]
    if emit_state:
        out_specs += [pl.BlockSpec((1, 2, 1, M_HD, M_HD), lambda b, h: (b, 0, h, 0, 0)),
                      pl.BlockSpec((1, 2, 1, 1, M_HD), lambda b, h: (b, 0, h, 0, 0)),
                      pl.BlockSpec((1, 2, 1, 1, 1), lambda b, h: (b, 0, h, 0, 0))]
        out_shape += [jax.ShapeDtypeStruct((nb, 2, M_HEADS, M_HD, M_HD), F32),
                      jax.ShapeDtypeStruct((nb, 2, M_HEADS, 1, M_HD), F32),
                      jax.ShapeDtypeStruct((nb, 2, M_HEADS, 1, 1), F32)]
    return pl.pallas_call(
        functools.partial(_mlstm_body, T=T, has_state=has_state, emit_state=emit_state),
        grid=(nb, M_HEADS),
        in_specs=in_specs,
        out_specs=out_specs,
        out_shape=out_shape,
        scratch_shapes=[pltpu.VMEM((T, M_HD), F32), pltpu.VMEM((M_HD, M_HD), F32), pltpu.VMEM((1, M_HD), F32)],
        compiler_params=_cp(("arbitrary", "arbitrary"), 40),
        name="mlstm",
    )(*args)


def _pair_masks():
    lane = lax.broadcasted_iota(I32, (1, LANES), 1)
    first = lane < N_HD
    return first, jnp.logical_not(first)


def _attn_body(q_ref, k_ref, v_ref, o_ref):
    q = q_ref[...]
    k = k_ref[...]
    v = v_ref[...]
    masks = _pair_masks()
    outs = []
    for msk in masks:
        qm = jnp.where(msk, q, jnp.zeros_like(q))
        s = _dot_nt(qm, k) * (N_HD ** -0.5)
        e = jnp.exp(s - jnp.max(s, axis=-1, keepdims=True))
        outs.append(_dot(e.astype(BF16), v) / jnp.sum(e, axis=-1, keepdims=True))
    o_ref[...] = jnp.where(masks[0], outs[0], outs[1]).astype(BF16)


def _dense_attention(big):
    cb = D // LANES
    return pl.pallas_call(
        _attn_body,
        grid=(BATCH, N_HEADS // 2),
        in_specs=[pl.BlockSpec((SEQ, LANES), lambda b, p: (b, 4 * cb + p)),
                  pl.BlockSpec((SEQ, LANES), lambda b, p: (b, 5 * cb + p)),
                  pl.BlockSpec((SEQ, LANES), lambda b, p: (b, 6 * cb + p))],
        out_specs=pl.BlockSpec((SEQ, LANES), lambda b, p: (b, p)),
        out_shape=jax.ShapeDtypeStruct((NP_TOK, D), BF16),
        compiler_params=_cp(("arbitrary", "arbitrary"), 32),
        name="dense_attn",
    )(big, big, big)


def _natten_body(q_ref, k_ref, v_ref, kc_ref, vc_ref, bias_ref, o_ref):
    rb = pl.program_id(2)
    ks = pl.multiple_of(jnp.clip(NA_QROWS * rb - MAX_WIN_H // 2, 0, GRID_H - NA_KROWS) * GRID_W, GRID_W)
    q = q_ref[...]
    kl = k_ref[pl.ds(ks, NA_KROWS * GRID_W), :]
    vl = v_ref[pl.ds(ks, NA_KROWS * GRID_W), :]
    kc = kc_ref[0, 0].astype(BF16)
    vc = vc_ref[0, 0].astype(BF16)
    masks = _pair_masks()
    outs = []
    for par, msk in enumerate(masks):
        qm = jnp.where(msk, q, jnp.zeros_like(q))
        sl = _dot_nt(qm, kl) * (N_HD ** -0.5) + bias_ref[0, par]
        sc = _dot_nt(qm, kc) * (N_HD ** -0.5)
        mx = jnp.maximum(jnp.max(sl, axis=-1, keepdims=True), jnp.max(sc, axis=-1, keepdims=True))
        el = jnp.exp(sl - mx)
        ec = jnp.exp(sc - mx)
        den = jnp.sum(el, axis=-1, keepdims=True) + jnp.sum(ec, axis=-1, keepdims=True)
        outs.append((_dot(el.astype(BF16), vl) + _dot(ec.astype(BF16), vc)) / den)
    o_ref[...] = jnp.where(masks[0], outs[0], outs[1]).astype(BF16)


def _na_pattern(rb):
    nrb = GRID_H // NA_QROWS
    return jnp.where(rb == 0, 0, jnp.where(rb == nrb - 1, 2, 1))


def _natten(big, cache_k, cache_v, bias_tab, layer):
    cb = D // LANES
    nrb = GRID_H // NA_QROWS
    qrows = NA_QROWS * GRID_W
    q0 = NP_TOK // qrows
    b0 = NP_TOK // DEC_SEQ
    return pl.pallas_call(
        _natten_body,
        grid=(DEC_BATCH, N_HEADS // 2, nrb),
        in_specs=[pl.BlockSpec((qrows, LANES), lambda b, p, r: (q0 + b * nrb + r, 4 * cb + p)),
                  pl.BlockSpec((DEC_SEQ, LANES), lambda b, p, r: (b0 + b, 5 * cb + p)),
                  pl.BlockSpec((DEC_SEQ, LANES), lambda b, p, r: (b0 + b, 6 * cb + p)),
                  pl.BlockSpec((1, 1, PAST_LEN, LANES), lambda b, p, r: (b, layer, 0, p)),
                  pl.BlockSpec((1, 1, PAST_LEN, LANES), lambda b, p, r: (b, layer, 0, p)),
                  pl.BlockSpec((1, 2, qrows, NA_KROWS * GRID_W), lambda b, p, r: (_na_pattern(r), p, 0, 0))],
        out_specs=pl.BlockSpec((qrows, LANES), lambda b, p, r: (b * nrb + r, p)),
        out_shape=jax.ShapeDtypeStruct((NS_TOK, D), BF16),
        compiler_params=_cp(("arbitrary", "arbitrary", "arbitrary"), 40),
        name="natten",
    )(big, big, big, cache_k, cache_v, bias_tab)


def _na_bias_table(rpb):
    nrb = GRID_H // NA_QROWS
    n = GRID_W
    n_dr = 2 * MAX_WIN_H - 1
    padl = n - WIN_W
    row = jnp.pad(rpb.astype(F32), ((0, 0), (0, 0), (padl, 2 * n - 1 - padl - (2 * WIN_W - 1))), constant_values=NEG)
    flat = jnp.broadcast_to(row[:, :, None, :], (N_HEADS, n_dr, n, 2 * n - 1)).reshape(N_HEADS, n_dr, n * (2 * n - 1))
    toep = flat[:, :, n - 1:n - 1 + n * (2 * n - 2)].reshape(N_HEADS, n_dr, n, 2 * n - 2)[..., :n]
    toep = jnp.pad(toep, ((0, 0), (NA_KROWS, NA_KROWS), (0, 0), (0, 0)), constant_values=NEG)
    tabs = []
    for rb in (0, 1, nrb - 1):
        ks = int(np.clip(NA_QROWS * rb - MAX_WIN_H // 2, 0, GRID_H - NA_KROWS))
        qi = np.arange(NA_QROWS * GRID_W)
        ki = np.arange(NA_KROWS * GRID_W)
        qrow = (NA_QROWS * rb + qi // GRID_W)[:, None]
        qcol = (qi % GRID_W)[:, None]
        krow = (ks + ki // GRID_W)[None, :]
        kcol = (ki % GRID_W)[None, :]
        rstart = np.clip(qrow - MAX_WIN_H // 2, 0, GRID_H - MAX_WIN_H)
        cstart = np.clip(qcol - WIN_W // 2, 0, GRID_W - WIN_W)
        valid = (krow >= rstart) & (krow < rstart + MAX_WIN_H) & (kcol >= cstart) & (kcol < cstart + WIN_W)
        blocks = []
        for qr in range(NA_QROWS):
            dr0 = NA_KROWS + ks - (NA_QROWS * rb + qr) + MAX_WIN_H - 1
            blk = toep[:, dr0:dr0 + NA_KROWS]
            blocks.append(jnp.transpose(blk, (0, 2, 1, 3)).reshape(N_HEADS, n, NA_KROWS * n))
        tab = jnp.stack(blocks, axis=1).reshape(N_HEADS, NA_QROWS * n, NA_KROWS * n)
        tabs.append(jnp.where(valid[None], tab, NEG))
    return jnp.stack(tabs)


def _postmix_body(oap_ref, oas_ref, obp_ref, obs_ref, ga_ref, gb_ref, x_ref, g1_ref, nf_ref, sc2_ref, sh2_ref,
                  wa_ref, wb_ref, wo_ref, wr_ref, xo_ref, h2_ref, aff_ref):
    prompt = pl.program_id(0) < NP_TILES
    a = _dot(jnp.where(prompt, oap_ref[...], oas_ref[...]), wa_ref[...])
    b = _dot(jnp.where(prompt, obp_ref[...], obs_ref[...]), wb_ref[...])
    merged = _sigmoid(ga_ref[...].astype(F32)) * a + _sigmoid(gb_ref[...].astype(F32)) * b
    xn = x_ref[...] + g1_ref[0] * _dot(merged.astype(BF16), wo_ref[...])
    xo_ref[...] = xn
    h2 = (_rms(xn) * nf_ref[...]) * (1.0 + sc2_ref[0]) + sh2_ref[0]
    h2_ref[...] = h2
    h1, h2m, h3 = _split3(h2)
    w1, w2, w3 = _split3(wr_ref[...])
    lg = (_dot_nt(w1, h1) + _dot_nt(w1, h2m) + _dot_nt(w2, h1)
          + _dot_nt(w1, h3) + _dot_nt(w3, h1) + _dot_nt(w2, h2m))
    e = jnp.exp(lg - jnp.max(lg, axis=0, keepdims=True))
    aff_ref[...] = e / jnp.sum(e, axis=0, keepdims=True)


def _postmix(oa_p, oa_s, ob_p, ob_s, big, x, mod, norm_ffn, wa, wb, wo, wr_t):
    row = lambda i: (i, 0)
    const = lambda i: (0, 0)
    prow = lambda i: (jnp.minimum(i, NP_TILES - 1), 0)
    srow = lambda i: (jnp.maximum(i - NP_TILES, 0), 0)
    return pl.pallas_call(
        _postmix_body,
        grid=(N_TILES,),
        in_specs=[pl.BlockSpec((TM, D), prow), pl.BlockSpec((TM, D), srow),
                  pl.BlockSpec((TM, D), prow), pl.BlockSpec((TM, D), srow),
                  pl.BlockSpec((TM, D), lambda i: (i, 7)), pl.BlockSpec((TM, D), lambda i: (i, 8)),
                  pl.BlockSpec((TM, D), row),
                  _mod_spec(2), pl.BlockSpec((1, D), const), _mod_spec(4), _mod_spec(3),
                  pl.BlockSpec((D, D), const), pl.BlockSpec((D, D), const), pl.BlockSpec((D, D), const),
                  pl.BlockSpec((N_EXPERTS, D), const)],
        out_specs=[pl.BlockSpec((TM, D), row), pl.BlockSpec((TM, D), row),
                   pl.BlockSpec((N_EXPERTS, TM), lambda i: (0, i))],
        out_shape=[jax.ShapeDtypeStruct((N_TOK, D), F32), jax.ShapeDtypeStruct((N_TOK, D), F32),
                   jax.ShapeDtypeStruct((N_EXPERTS, N_TOK), F32)],
        compiler_params=_cp(("arbitrary",), 48),
        name="postmix",
    )(oa_p, oa_s, ob_p, ob_s, big, big, x, mod, norm_ffn, mod, mod, wa, wb, wo, wr_t)


def _route_body(a_ref, idx_ref, gv_ref, off_ref, pos_ref, *, R, cap):
    a = a_ref[...]
    capf = float(cap)

    def count_ge(thr):
        c = jnp.sum(jnp.where(a >= thr, 1.0, 0.0), axis=1, keepdims=True)
        return jnp.sum(c, axis=2, keepdims=True)

    def search(i, lo_bits):
        cand = lo_bits | jnp.left_shift(jnp.int32(1), 30 - i)
        return jnp.where(count_ge(pltpu.bitcast(cand, F32)) >= capf, cand, lo_bits)

    tau_bits = lax.fori_loop(0, 31, search, jnp.zeros((N_EXPERTS, 1, 1), I32))
    lo = pltpu.bitcast(tau_bits, F32)
    hi = pltpu.bitcast(jnp.maximum(tau_bits + 1, jnp.int32(MIN_NORMAL_BITS)), F32)

    def refine(i, lh):
        lo, hi = lh
        mid = lo + (hi - lo) * 0.5
        ok = count_ge(mid) >= capf
        return jnp.where(ok, mid, lo), jnp.where(ok, hi, mid)

    lo, hi = lax.fori_loop(0, 32, refine, (lo, hi))
    above = jnp.where(a >= hi, 1.0, 0.0)
    ties = jnp.where(a >= lo, 1.0, 0.0) - above
    need = capf - jnp.sum(jnp.sum(above, axis=1, keepdims=True), axis=2, keepdims=True)

    l0 = lax.broadcasted_iota(I32, (LANES, LANES), 0)
    l1 = lax.broadcasted_iota(I32, (LANES, LANES), 1)
    upper = jnp.where(l0 <= l1, 1.0, 0.0).astype(BF16)
    r0 = lax.broadcasted_iota(I32, (R, R), 0)
    r1 = lax.broadcasted_iota(I32, (R, R), 1)
    below = jnp.where(r1 < r0, 1.0, 0.0).astype(BF16)
    eye_r = r0 == r1
    r_row = lax.broadcasted_iota(I32, (1, R), 1).astype(F32)
    lane_row = lax.broadcasted_iota(I32, (1, LANES), 1).astype(F32)
    jcol = lax.broadcasted_iota(I32, (cap, 1), 0).astype(F32)

    def prefix(x):
        within = _dot(x.astype(BF16), upper)
        tot = jnp.broadcast_to(within[:, LANES - 1:LANES], (R, LANES))
        offs = _dot(below, tot.astype(BF16))
        return within + offs, offs

    for e in range(N_EXPERTS):
        eq = ties[e]
        cin_eq, _ = prefix(eq)
        sel = above[e] + eq * jnp.where(cin_eq - eq < need[e], 1.0, 0.0)
        cin, offs = prefix(sel)
        cend_row = jnp.sum(jnp.where(eye_r, cin[:, LANES - 1:LANES], 0.0), axis=0, keepdims=True)
        bcol = jnp.sum(jnp.where(cend_row <= jcol, 1.0, 0.0), axis=1, keepdims=True)
        onehot = jnp.where(bcol == r_row, 1.0, 0.0).astype(BF16)
        lcol = jnp.sum(jnp.where(_dot_x3(onehot, cin) <= jcol, 1.0, 0.0), axis=1, keepdims=True)
        idx_ref[e] = (bcol * LANES + lcol).astype(I32)
        gv_ref[e] = jnp.sum(jnp.where(lane_row == lcol, _dot_x3(onehot, a[e]), 0.0), axis=1, keepdims=True)
        off_ref[e] = offs[:, 0:1]
        pos_ref[e] = jnp.where(sel > 0.0, cin - 1.0, -1.0)


def _route(aff3, cap):
    R = aff3.shape[1]
    full = lambda s: pl.BlockSpec(s, lambda i: (0, 0, 0))
    return pl.pallas_call(
        functools.partial(_route_body, R=R, cap=cap),
        grid=(1,),
        in_specs=[full((N_EXPERTS, R, LANES))],
        out_specs=[full((N_EXPERTS, cap, 1)), full((N_EXPERTS, cap, 1)), full((N_EXPERTS, R, 1)),
                   full((N_EXPERTS, R, LANES))],
        out_shape=[jax.ShapeDtypeStruct((N_EXPERTS, cap, 1), I32),
                   jax.ShapeDtypeStruct((N_EXPERTS, cap, 1), F32),
                   jax.ShapeDtypeStruct((N_EXPERTS, R, 1), F32),
                   jax.ShapeDtypeStruct((N_EXPERTS, R, LANES), F32)],
        compiler_params=_cp(("arbitrary",), 48),
        name="route",
    )(aff3)


GATHER_UNROLL = 8


def _gather_copy(h_hbm, xs_ref, sem, src_row, dst_row, nrows):
    return pltpu.make_async_copy(h_hbm.at[pl.ds(src_row, nrows), :], xs_ref.at[pl.ds(dst_row, nrows), :], sem)


def _slab_copy(y_ref, ys_hbm, sem, e, s):
    return pltpu.make_async_copy(y_ref.at[:, s * LANES:(s + 1) * LANES], ys_hbm.at[e, :, s, :], sem)


def _ffn_body(idx_ref, h_hbm, wg_ref, wu_ref, wd_ref, gv_ref, ys_hbm, xs_ref, xb_ref, y_ref, gsem, osem):
    e = pl.program_id(0)
    f = pl.program_id(1)
    last_e = pl.num_programs(0) - 1

    @pl.when(f == 0)
    def _():
        def issue(j, carry):
            _gather_copy(h_hbm, xs_ref, gsem, idx_ref[e * CAP_TOT + j], j, 1).start()
            return carry
        lax.fori_loop(0, CAP_TOT, issue, 0, unroll=GATHER_UNROLL)

        @pl.when(e > 0)
        def _():
            for s in range(SLAB):
                _slab_copy(y_ref, ys_hbm, osem, e - 1, s).wait()

        _gather_copy(h_hbm, xs_ref, gsem, 0, 0, CAP_TOT).wait()
        xb_ref[...] = xs_ref[...].astype(BF16)

    xb = xb_ref[...]
    g = _dot(xb, wg_ref[0].astype(BF16))
    u = _dot(xb, wu_ref[0].astype(BF16))
    hid = (g * _sigmoid(g) * u).astype(BF16)
    y = _dot(hid, wd_ref[0].astype(BF16))

    @pl.when(f == 0)
    def _():
        y_ref[...] = y

    @pl.when(f > 0)
    def _():
        y_ref[...] = y_ref[...] + y

    @pl.when(f == pl.num_programs(1) - 1)
    def _():
        y_ref[...] = y_ref[...] * gv_ref[0]
        for s in range(SLAB):
            _slab_copy(y_ref, ys_hbm, osem, e, s).start()

        @pl.when(e == last_e)
        def _():
            for s in range(SLAB):
                _slab_copy(y_ref, ys_hbm, osem, e, s).wait()


def _ffn(idx_flat, h2, wg, wu, wd, gv):
    nf = EXPERT_FF // FF_CHUNK
    return pl.pallas_call(
        _ffn_body,
        grid_spec=pltpu.PrefetchScalarGridSpec(
            num_scalar_prefetch=1,
            grid=(N_EXPERTS, nf),
            in_specs=[pl.BlockSpec(memory_space=pl.ANY),
                      pl.BlockSpec((1, D, FF_CHUNK), lambda e, f, idx: (e, 0, f)),
                      pl.BlockSpec((1, D, FF_CHUNK), lambda e, f, idx: (e, 0, f)),
                      pl.BlockSpec((1, FF_CHUNK, D), lambda e, f, idx: (e, f, 0)),
                      pl.BlockSpec((1, CAP_TOT, 1), lambda e, f, idx: (e, 0, 0))],
            out_specs=pl.BlockSpec(memory_space=pl.ANY),
            scratch_shapes=[pltpu.VMEM((CAP_TOT, D), F32), pltpu.VMEM((CAP_TOT, D), BF16),
                            pltpu.VMEM((CAP_TOT, D), F32),
                            pltpu.SemaphoreType.DMA(()), pltpu.SemaphoreType.DMA(())]),
        out_shape=jax.ShapeDtypeStruct((N_EXPERTS, CAP_TOT, SLAB, LANES), F32),
        compiler_params=_cp(("arbitrary", "arbitrary"), 58),
        name="expert_ffn",
    )(idx_flat, h2, wg, wu, wd, gv)


MAX_PAIRS = N_EXPERTS * TM
SEG_BITS = TM.bit_length()


def _seg_copy(ys_hbm, w_ref, sem, e, src_row, dst_row, nrows):
    src = pl.multiple_of(src_row * SLAB, SLAB)
    dst = pl.multiple_of(dst_row * SLAB, SLAB)
    return pltpu.make_async_copy(ys_hbm.at[e, pl.ds(src, nrows * SLAB), :], w_ref.at[pl.ds(dst, nrows * SLAB), :], sem)


def _combine_body(lo_ref, hi_ref, ys_hbm, pos_ref, x_ref, g2_ref, nf_ref, o_ref, w_ref, acc_ref, sem, *, last):
    b = pl.program_id(0)
    lane = lax.broadcasted_iota(I32, (1, LANES), 1)
    off = jnp.int32(0)
    shift = jnp.zeros((1, LANES), F32)
    for e in range(N_EXPERTS):
        lo = lo_ref[e * N_TILES + b]
        n = hi_ref[e * N_TILES + b] - lo
        for bit in range(SEG_BITS - 1, -1, -1):
            done = (n >> (bit + 1)) << (bit + 1)

            @pl.when((n & (1 << bit)) != 0)
            def _(e=e, lo=lo, off=off, done=done, bit=bit):
                _seg_copy(ys_hbm, w_ref, sem, e, lo + done, off + done, 1 << bit).start()
        shift = jnp.where(lane == e, (off - lo).astype(F32), shift)
        off = off + n
    npairs = off
    for bit in range(MAX_PAIRS.bit_length() - 1, -1, -1):
        @pl.when((npairs & (1 << bit)) != 0)
        def _(bit=bit):
            _seg_copy(ys_hbm, w_ref, sem, 0, 0, 0, 1 << bit).wait()

    acc_ref[...] = jnp.zeros_like(acc_ref)
    pos = pos_ref[...]
    prow = jnp.where(pos >= 0.0, pos + shift[:, 0:N_EXPERTS], -1.0)
    sub = lax.broadcasted_iota(I32, (TM, 1), 0)
    pair = lax.broadcasted_iota(I32, (1, TM), 1).astype(F32)

    def body(c, carry):
        valid = (c * TM + sub) < npairs
        local = prow - (c * TM).astype(F32)
        onehot = jnp.zeros((TM, TM), F32)
        for e in range(N_EXPERTS):
            onehot = onehot + jnp.where(local[:, e:e + 1] == pair, 1.0, 0.0)
        onehot = onehot.astype(BF16)
        for s in range(SLAB):
            data = jnp.where(valid, w_ref[pl.ds(c * TM * SLAB + s, TM, stride=SLAB), :], 0.0)
            sl = slice(s * LANES, (s + 1) * LANES)
            acc_ref[:, sl] = acc_ref[:, sl] + _dot_x3(onehot, data)
        return carry

    lax.fori_loop(0, (npairs + TM - 1) // TM, body, 0)
    xn = x_ref[...] + g2_ref[0] * acc_ref[...]
    if last:
        xn = _rms(xn) * nf_ref[...]
    o_ref[...] = xn


def _combine(seg_lo, seg_hi, ys, pos_t, x, mod, norm_final, *, last):
    return pl.pallas_call(
        functools.partial(_combine_body, last=last),
        grid_spec=pltpu.PrefetchScalarGridSpec(
            num_scalar_prefetch=2,
            grid=(N_TILES,),
            in_specs=[pl.BlockSpec(memory_space=pl.ANY),
                      pl.BlockSpec((TM, N_EXPERTS), lambda i, lo, hi: (i, 0)),
                      pl.BlockSpec((TM, D), lambda i, lo, hi: (i, 0)),
                      pl.BlockSpec((1, 1, D), lambda i, lo, hi: (_mod_row(i), 0, 5)),
                      pl.BlockSpec((1, D), lambda i, lo, hi: (0, 0))],
            out_specs=pl.BlockSpec((TM, D), lambda i, lo, hi: (i, 0)),
            scratch_shapes=[pltpu.VMEM((MAX_PAIRS * SLAB, LANES), F32), pltpu.VMEM((TM, D), F32),
                            pltpu.SemaphoreType.DMA(())]),
        out_shape=jax.ShapeDtypeStruct((N_TOK, D), F32),
        compiler_params=_cp(("arbitrary",), 48),
        name="combine",
    )(seg_lo, seg_hi, ys, pos_t, x, mod, norm_final)


def _tile_segments(off_p, off_s):
    per = TM // LANES
    lo_p = off_p[:, ::per, 0].astype(I32)
    lo_s = off_s[:, ::per, 0].astype(I32) + CAP_P
    hi_p = jnp.concatenate([lo_p[:, 1:], jnp.full((N_EXPERTS, 1), CAP_P, I32)], axis=1)
    hi_s = jnp.concatenate([lo_s[:, 1:], jnp.full((N_EXPERTS, 1), CAP_TOT, I32)], axis=1)
    lo = jnp.concatenate([lo_p, lo_s], axis=1)
    hi = jnp.concatenate([hi_p, hi_s], axis=1)
    return lo.reshape(-1), hi.reshape(-1)


def kernel(x_prompt, x_sample, cache_na_k, cache_na_v, state_mlstm_C, state_mlstm_n, state_mlstm_m, c, c_ctx, w_ada, b_ada, norm_mix, norm_ffn, w_in, conv_qk, mlstm_gate_bias, mlstm_head_norm, na_rpb, w_branch_a, w_branch_b, w_out, w_router, w_expert_gate, w_expert_up, w_expert_down, norm_final):
    depth = w_in.shape[0]
    m_width = M_HEADS * M_HD
    n_gates = 4 * M_HEADS

    x = jnp.concatenate([x_prompt.reshape(NP_TOK, D), x_sample.reshape(NS_TOK, D)], axis=0)
    cond8 = jnp.concatenate([c_ctx[None], c, jnp.zeros((8 - 1 - DEC_BATCH, D), F32)], axis=0)
    mods = _adaln(cond8, w_ada, b_ada).reshape(depth, 8, 1, 6 * D)
    cos_t, sin_t = _rope_tables()
    cache_k = cache_na_k.reshape(DEC_BATCH, depth, PAST_LEN, D)
    cache_v = cache_na_v.reshape(DEC_BATCH, depth, PAST_LEN, D)
    norm_final2 = norm_final.reshape(1, D)

    ks_, vs_, cs_, ns_, ms_ = [], [], [], [], []
    for l in range(depth):
        mod = mods[l]
        w_big, w_gates = _pack_w_in(w_in[l])
        gate_bias = jnp.pad(mlstm_gate_bias[l], (0, LANES - n_gates)).reshape(1, LANES)
        head_norm = mlstm_head_norm[l].reshape(1, m_width)

        big, gates, kv = _inproj(x, norm_mix[l].reshape(1, D), mod, w_big, w_gates)

        qk_p = _convprep(big, conv_qk[l], cos_t, sin_t, rope=False, nb=BATCH, T=SEQ, row0=0)
        qk_s = _convprep(big, conv_qk[l], cos_t, sin_t, rope=True, nb=DEC_BATCH, T=DEC_SEQ, row0=NP_TOK // DEC_SEQ)
        oa_p, c_new, n_new, m_new = _mlstm(qk_p, big, gates, gate_bias, head_norm, None,
                                           nb=BATCH, T=SEQ, row0=0, emit_state=True)
        state = (state_mlstm_C[:, l], state_mlstm_n[:, l].reshape(DEC_BATCH, 2, M_HEADS, 1, M_HD),
                 state_mlstm_m[:, l].reshape(DEC_BATCH, 2, M_HEADS, 1, 1))
        (oa_s,) = _mlstm(qk_s, big, gates, gate_bias, head_norm, state,
                         nb=DEC_BATCH, T=DEC_SEQ, row0=NP_TOK // DEC_SEQ, emit_state=False)

        ob_p = _dense_attention(big)
        ob_s = _natten(big, cache_k, cache_v, _na_bias_table(na_rpb[l]), l)

        x, h2, aff_t = _postmix(oa_p, oa_s, ob_p, ob_s, big, x, mod, norm_ffn[l].reshape(1, D),
                                w_branch_a[l].astype(BF16), w_branch_b[l].astype(BF16), w_out[l].astype(BF16),
                                w_router[l].T)

        idx_p, gv_p, off_p, pos_p = _route(aff_t[:, :NP_TOK].reshape(N_EXPERTS, NP_TOK // LANES, LANES), CAP_P)
        idx_s, gv_s, off_s, pos_s = _route(aff_t[:, NP_TOK:].reshape(N_EXPERTS, NS_TOK // LANES, LANES), CAP_S)
        idx = jnp.concatenate([idx_p, idx_s + NP_TOK], axis=1)
        gv = jnp.concatenate([gv_p, gv_s], axis=1)
        pos_s = jnp.where(pos_s >= 0.0, pos_s + CAP_P, pos_s)
        pos_t = jnp.concatenate([pos_p.reshape(N_EXPERTS, NP_TOK), pos_s.reshape(N_EXPERTS, NS_TOK)], axis=1).T
        ys = _ffn(idx.reshape(-1), h2, w_expert_gate[l], w_expert_up[l], w_expert_down[l], gv)
        ys = ys.reshape(N_EXPERTS, CAP_TOT * SLAB, LANES)
        seg_lo, seg_hi = _tile_segments(off_p, off_s)
        x = _combine(seg_lo, seg_hi, ys, pos_t, x, mod, norm_final2, last=(l == depth - 1))

        ks_.append(kv[:NP_TOK, :D].reshape(BATCH, SEQ, N_HEADS, N_HD))
        vs_.append(kv[:NP_TOK, D:].reshape(BATCH, SEQ, N_HEADS, N_HD))
        cs_.append(c_new)
        ns_.append(n_new.reshape(BATCH, 2, M_HEADS, M_HD))
        ms_.append(m_new.reshape(BATCH, 2, M_HEADS))

    y_prompt = x[:NP_TOK].reshape(BATCH, SEQ, D)
    y_sample = x[NP_TOK:].reshape(DEC_BATCH, DEC_SEQ, D)
    return (y_prompt, y_sample, jnp.stack(ks_, axis=1), jnp.stack(vs_, axis=1),
            jnp.stack(cs_, axis=1), jnp.stack(ns_, axis=1), jnp.stack(ms_, axis=1))
```

```python
import functools

import numpy as np
import jax
import jax.numpy as jnp
from jax import lax
from jax.experimental import pallas as pl
from jax.experimental.pallas import tpu as pltpu

F32 = jnp.float32
BF16 = jnp.bfloat16
I32 = jnp.int32

D = 1024
BATCH, SEQ = 16, 256
DEC_BATCH, DEC_SEQ = 4, 2048
PAST_LEN = 512
NP_TOK = BATCH * SEQ
NS_TOK = DEC_BATCH * DEC_SEQ
N_TOK = NP_TOK + NS_TOK
GRID_W = 64
GRID_H = DEC_SEQ // GRID_W
M_HEADS, M_HD = 4, 256
CHUNK = 128
N_HEADS, N_HD = 16, 64
MAX_WIN_H, WIN_W = 8, 16
N_EXPERTS, EXPERT_FF = 16, 2048
CAP_P = 2 * NP_TOK // N_EXPERTS
CAP_S = 2 * NS_TOK // N_EXPERTS
CAP_TOT = CAP_P + CAP_S
ROPE_BASE = 10000.0
EPS = 1e-6
NEG = -1e30
MIN_NORMAL_BITS = 0x00800000

TM = 256
N_TILES = N_TOK // TM
NP_TILES = NP_TOK // TM
LANES = 128
NA_QROWS = 4
NA_KROWS = 12
FF_CHUNK = 512
SLAB = D // LANES


def _cp(sem, vmem_mb):
    return pltpu.CompilerParams(dimension_semantics=sem, vmem_limit_bytes=vmem_mb * 2 ** 20)


def _dot(a, b):
    return jnp.dot(a, b, preferred_element_type=F32)


def _dot_nt(a, b):
    return lax.dot_general(a, b, (((1,), (1,)), ((), ())), preferred_element_type=F32)


def _dot_tn(a, b):
    return lax.dot_general(a, b, (((0,), (0,)), ((), ())), preferred_element_type=F32)


def _split3(x):
    hi = x.astype(BF16)
    r = x - hi.astype(F32)
    mid = r.astype(BF16)
    lo = (r - mid.astype(F32)).astype(BF16)
    return hi, mid, lo


def _dot_x3(a_bf, x):
    hi, mid, lo = _split3(x)
    return _dot(a_bf, hi) + _dot(a_bf, mid) + _dot(a_bf, lo)


def _sigmoid(x):
    return 1.0 / (1.0 + jnp.exp(-x))


def _rms(x):
    return x * lax.rsqrt(jnp.mean(x * x, axis=-1, keepdims=True) + EPS)


def _mod_row(i):
    return jnp.where(i < NP_TILES, 0, 1 + (i - NP_TILES) // (DEC_SEQ // TM))


def _mod_spec(kind):
    return pl.BlockSpec((1, 1, D), lambda i: (_mod_row(i), 0, kind))


def _adaln_body(c_ref, w_ref, b_ref, o_ref):
    c = c_ref[...]
    s = (c * _sigmoid(c)).astype(BF16)
    o_ref[0] = _dot(s, w_ref[0].astype(BF16)) + b_ref[0]


def _adaln(cond8, w_ada, b_ada):
    depth = w_ada.shape[0]
    tn = 1536
    return pl.pallas_call(
        _adaln_body,
        grid=(depth, 6 * D // tn),
        in_specs=[pl.BlockSpec((8, D), lambda l, j: (0, 0)),
                  pl.BlockSpec((1, D, tn), lambda l, j: (l, 0, j)),
                  pl.BlockSpec((1, 1, tn), lambda l, j: (l, 0, j))],
        out_specs=pl.BlockSpec((1, 8, tn), lambda l, j: (l, 0, j)),
        out_shape=jax.ShapeDtypeStruct((depth, 8, 6 * D), F32),
        compiler_params=_cp(("arbitrary", "arbitrary"), 40),
        name="adaln",
    )(cond8, w_ada, b_ada.reshape(depth, 1, 6 * D))


N_GATES = 4 * M_HEADS
GATE_COL = 4 * M_HEADS * M_HD


def _pack_body(w_ref, big_ref, gates_ref):
    w = w_ref[0]
    big_ref[:, 0:GATE_COL] = w[:, 0:GATE_COL].astype(BF16)
    big_ref[:, GATE_COL:9 * D] = w[:, GATE_COL + N_GATES:9 * D + N_GATES].astype(BF16)
    lane = lax.broadcasted_iota(I32, (1, LANES), 1)
    gates_ref[...] = jnp.where(lane < N_GATES, w[:, GATE_COL:GATE_COL + LANES], 0.0).astype(BF16)


def _pack_w_in(w, layer):
    tr = 128
    return pl.pallas_call(
        _pack_body,
        grid=(D // tr,),
        in_specs=[pl.BlockSpec((1, tr, w.shape[2]), lambda i: (layer, i, 0))],
        out_specs=[pl.BlockSpec((tr, 9 * D), lambda i: (i, 0)), pl.BlockSpec((tr, LANES), lambda i: (i, 0))],
        out_shape=[jax.ShapeDtypeStruct((D, 9 * D), BF16), jax.ShapeDtypeStruct((D, LANES), BF16)],
        compiler_params=_cp(("arbitrary",), 48),
        name="pack_w_in",
    )(w)


def _inproj_body(x_ref, g_ref, sc_ref, sh_ref, w_ref, wg_ref, big_ref, gates_ref, kv_ref):
    i = pl.program_id(0)
    h = (_rms(x_ref[...]) * g_ref[...]) * (1.0 + sc_ref[0]) + sh_ref[0]
    hb = h.astype(BF16)
    gates_ref[...] = _dot(hb, wg_ref[...])
    for c in range(9):
        r = _dot(hb, w_ref[:, c * D:(c + 1) * D])
        big_ref[:, c * D:(c + 1) * D] = r.astype(BF16)
        if c in (5, 6):
            @pl.when(i < NP_TILES)
            def _():
                kv_ref[:, (c - 5) * D:(c - 4) * D] = r

    @pl.when(i == NP_TILES)
    def _():
        kv_ref[...] = jnp.zeros_like(kv_ref)


def _inproj(x, norm_g, mod, w_big, w_gates):
    return pl.pallas_call(
        _inproj_body,
        grid=(N_TILES,),
        in_specs=[pl.BlockSpec((TM, D), lambda i: (i, 0)),
                  pl.BlockSpec((1, D), lambda i: (0, 0)),
                  _mod_spec(1), _mod_spec(0),
                  pl.BlockSpec((D, 9 * D), lambda i: (0, 0), pipeline_mode=pl.Buffered(1)),
                  pl.BlockSpec((D, LANES), lambda i: (0, 0))],
        out_specs=[pl.BlockSpec((TM, 9 * D), lambda i: (i, 0)),
                   pl.BlockSpec((TM, LANES), lambda i: (i, 0)),
                   pl.BlockSpec((TM, 2 * D), lambda i: (jnp.minimum(i, NP_TILES), 0))],
        out_shape=[jax.ShapeDtypeStruct((N_TOK, 9 * D), BF16),
                   jax.ShapeDtypeStruct((N_TOK, LANES), F32),
                   jax.ShapeDtypeStruct((NP_TOK + TM, 2 * D), F32)],
        compiler_params=_cp(("arbitrary",), 56),
        name="inproj",
    )(x, norm_g, mod, mod, w_big, w_gates)


def _conv_body(x_ref, w_ref, cos_ref, sin_ref, o_ref, *, rope, T):
    j = pl.program_id(1)
    x = x_ref[...].astype(F32)
    w = w_ref[...]
    row = lax.broadcasted_iota(I32, (T, 1), 0)
    xp = jnp.where(row == 0, 0.0, pltpu.roll(x, 1, 0))
    xn = jnp.where(row == T - 1, 0.0, pltpu.roll(x, T - 1, 0))
    y = xp * w[0:1] + x * w[1:2] + xn * w[2:3]
    y = y * _sigmoid(y)
    scale = jnp.where(j >= M_HEADS, M_HD ** -0.5, 1.0)
    for hlf in range(2):
        sl = slice(hlf * LANES, (hlf + 1) * LANES)
        yh = y[:, sl]
        if rope:
            yh = yh * cos_ref[:, sl] + pltpu.roll(yh, LANES // 2, 1) * sin_ref[:, sl]
        o_ref[:, sl] = (yh * scale).astype(BF16)


def _convprep(big, conv_w, cos_t, sin_t, *, rope, nb, T, row0):
    return pl.pallas_call(
        functools.partial(_conv_body, rope=rope, T=T),
        grid=(nb, 2 * M_HEADS),
        in_specs=[pl.BlockSpec((T, M_HD), lambda b, j: (row0 + b, j)),
                  pl.BlockSpec((3, M_HD), lambda b, j: (0, j)),
                  pl.BlockSpec((T, M_HD), lambda b, j: (0, 0)),
                  pl.BlockSpec((T, M_HD), lambda b, j: (0, 0))],
        out_specs=pl.BlockSpec((T, M_HD), lambda b, j: (b, j)),
        out_shape=jax.ShapeDtypeStruct((nb * T, 2 * M_HEADS * M_HD), BF16),
        compiler_params=_cp(("arbitrary", "arbitrary"), 48),
        name="convprep",
    )(big, conv_w, cos_t, sin_t)


def _rope_tables():
    pos = np.arange(DEC_SEQ)
    rows = (pos // GRID_W).astype(np.float32)
    cols = (pos % GRID_W).astype(np.float32)
    nfreq = M_HD // 4
    inv = (ROPE_BASE ** (-np.arange(nfreq, dtype=np.float32) / nfreq)).astype(np.float32)
    d = np.arange(M_HD)
    p = np.where(d[None, :] < M_HD // 2, rows[:, None], cols[:, None]).astype(np.float32)
    ang = (p * inv[d % nfreq][None, :]).astype(np.float32)
    sign = np.where((d % (M_HD // 2)) < nfreq, -1.0, 1.0).astype(np.float32)
    return jnp.asarray(np.cos(ang), F32), jnp.asarray(np.sin(ang) * sign[None, :], F32)


def _mlstm_body(*refs, T, has_state, emit_state):
    q_ref, k_ref, v_ref, om_ref, g_ref, gb_ref, hn_ref = refs[:7]
    pos = 7
    if has_state:
        c0_ref, n0_ref, m0_ref = refs[pos:pos + 3]
        pos += 3
    oa_ref = refs[pos]
    pos += 1
    if emit_state:
        co_ref, no_ref, mo_ref = refs[pos:pos + 3]
        pos += 3
    hs_refs = refs[pos:pos + 2]
    c_refs = refs[pos + 2:pos + 4]
    n_refs = refs[pos + 4:pos + 6]

    hd = pl.program_id(1)
    nc = T // CHUNK
    lane = lax.broadcasted_iota(I32, (1, LANES), 1)
    r_i = lax.broadcasted_iota(I32, (CHUNK, CHUNK), 0)
    c_i = lax.broadcasted_iota(I32, (CHUNK, CHUNK), 1)
    eye = r_i == c_i
    causal = (r_i >= c_i, r_i <= c_i)
    cum = tuple(jnp.where(m, 1.0, 0.0).astype(BF16) for m in causal)

    def to_row(col):
        return jnp.sum(jnp.where(eye, col, 0.0), axis=0, keepdims=True)

    def pick(mat, colidx):
        return jnp.sum(jnp.where(lane == colidx, mat, 0.0), axis=1, keepdims=True)

    def chunk_step(d, c, m_prev):
        hs_ref, c_ref, n_ref = hs_refs[d], c_refs[d], n_refs[d]
        rows = pl.ds(pl.multiple_of(c * CHUNK, CHUNK), CHUNK)
        g = g_ref[rows, :] + gb_ref[...]
        lf = jnp.minimum(g, 0.0) - jnp.log(1.0 + jnp.exp(-jnp.abs(g)))
        bmat = _dot_x3(cum[d], lf)
        ig_col = pick(g, d * M_HEADS + hd)
        b_col = pick(bmat, 2 * M_HEADS + d * M_HEADS + hd)
        ig_row = to_row(ig_col)
        b_row = to_row(b_col)
        b_last = b_row[:, CHUNK - 1:CHUNK] if d == 0 else b_row[:, 0:1]
        logd = jnp.where(causal[d], b_col - b_row + ig_row, -jnp.inf)
        inter = b_col + m_prev
        m_row = jnp.maximum(inter, jnp.max(logd, axis=1, keepdims=True))
        dmat = jnp.exp(logd - m_row)
        s_inter = jnp.exp(inter - m_row)
        q = q_ref[rows, :]
        k = k_ref[rows, :]
        v = v_ref[rows, :]
        s = _dot_nt(q, k) * dmat
        num = _dot(s.astype(BF16), v) + s_inter * _dot(q, c_ref[...].astype(BF16))
        den = (jnp.sum(s, axis=1, keepdims=True)
               + s_inter * jnp.sum(q.astype(F32) * n_ref[...], axis=1, keepdims=True))
        hs_ref[rows, :] = num / jnp.maximum(jnp.abs(den), jnp.exp(-m_row))
        log_w = b_last - b_col + ig_col
        m_new = jnp.maximum(b_last + m_prev, jnp.max(log_w, axis=0, keepdims=True))
        w = jnp.exp(log_w - m_new)
        decay = jnp.exp(b_last + m_prev - m_new)
        kw = k.astype(F32) * w
        c_ref[...] = decay * c_ref[...] + _dot_tn(kw.astype(BF16), v)
        n_ref[...] = decay * n_ref[...] + jnp.sum(kw, axis=0, keepdims=True)
        return m_new

    m_init = []
    for d in range(2):
        if has_state:
            c_refs[d][...] = c0_ref[0, d, 0]
            n_refs[d][...] = n0_ref[0, d, 0]
            m_init.append(m0_ref[0, d, 0])
        else:
            c_refs[d][...] = jnp.zeros_like(c_refs[d])
            n_refs[d][...] = jnp.zeros_like(n_refs[d])
            m_init.append(jnp.zeros((1, 1), F32))

    def body(ci, ms):
        return chunk_step(0, ci, ms[0]), chunk_step(1, nc - 1 - ci, ms[1])

    m_fin = lax.fori_loop(0, nc, body, tuple(m_init))
    if emit_state:
        for d in range(2):
            co_ref[0, d, 0] = c_refs[d][...]
            no_ref[0, d, 0] = n_refs[d][...]
            mo_ref[0, d, 0] = m_fin[d]

    hm = _rms(hs_refs[0][...] + hs_refs[1][...]) * hn_ref[...]
    oa_ref[...] = (_sigmoid(om_ref[...].astype(F32)) * hm).astype(BF16)


def _mlstm(qk, big, gates, gate_bias, head_norm, state, *, nb, T, row0, emit_state):
    has_state = state is not None
    in_specs = [pl.BlockSpec((T, M_HD), lambda b, h: (b, h)),
                pl.BlockSpec((T, M_HD), lambda b, h: (b, M_HEADS + h)),
                pl.BlockSpec((T, M_HD), lambda b, h: (row0 + b, 2 * M_HEADS + h)),
                pl.BlockSpec((T, M_HD), lambda b, h: (row0 + b, 3 * M_HEADS + h)),
                pl.BlockSpec((T, LANES), lambda b, h: (row0 + b, 0)),
                pl.BlockSpec((1, LANES), lambda b, h: (0, 0)),
                pl.BlockSpec((1, M_HD), lambda b, h: (0, h))]
    args = [qk, qk, big, big, gates, gate_bias, head_norm]
    if has_state:
        c0, n0, m0 = state
        in_specs += [pl.BlockSpec((1, 2, 1, M_HD, M_HD), lambda b, h: (b, 0, h, 0, 0)),
                     pl.BlockSpec((1, 2, 1, 1, M_HD), lambda b, h: (b, 0, h, 0, 0)),
                     pl.BlockSpec((1, 2, 1, 1, 1), lambda b, h: (b, 0, h, 0, 0))]
        args += [c0, n0, m0]
    out_specs = [pl.BlockSpec((T, M_HD), lambda b, h: (b, h))]
    out_shape = [jax.ShapeDtypeStruct((nb * T, M_HEADS * M_HD), BF16)]
    if emit_state:
        out_specs += [pl.BlockSpec((1, 2, 1, M_HD, M_HD), lambda b, h: (b, 0, h, 0, 0)),
                      pl.BlockSpec((1, 2, 1, 1, M_HD), lambda b, h: (b, 0, h, 0, 0)),
                      pl.BlockSpec((1, 2, 1, 1, 1), lambda b, h: (b, 0, h, 0, 0))]
        out_shape += [jax.ShapeDtypeStruct((nb, 2, M_HEADS, M_HD, M_HD), F32),
                      jax.ShapeDtypeStruct((nb, 2, M_HEADS, 1, M_HD), F32),
                      jax.ShapeDtypeStruct((nb, 2, M_HEADS, 1, 1), F32)]
    return pl.pallas_call(
        functools.partial(_mlstm_body, T=T, has_state=has_state, emit_state=emit_state),
        grid=(nb, M_HEADS),
        in_specs=in_specs,
        out_specs=out_specs,
        out_shape=out_shape,
        scratch_shapes=[pltpu.VMEM((T, M_HD), F32)] * 2 + [pltpu.VMEM((M_HD, M_HD), F32)] * 2
        + [pltpu.VMEM((1, M_HD), F32)] * 2,
        compiler_params=_cp(("arbitrary", "arbitrary"), 40),
        name="mlstm",
    )(*args)


def _pair_masks():
    lane = lax.broadcasted_iota(I32, (1, LANES), 1)
    first = lane < N_HD
    return first, jnp.logical_not(first)


def _attn_body(q_ref, k_ref, v_ref, o_ref):
    q = q_ref[...]
    k = k_ref[...]
    v = v_ref[...]
    masks = _pair_masks()
    outs = []
    for msk in masks:
        qm = jnp.where(msk, q, jnp.zeros_like(q))
        s = _dot_nt(qm, k) * (N_HD ** -0.5)
        e = jnp.exp(s - jnp.max(s, axis=-1, keepdims=True))
        outs.append(_dot(e.astype(BF16), v) / jnp.sum(e, axis=-1, keepdims=True))
    o_ref[...] = jnp.where(masks[0], outs[0], outs[1]).astype(BF16)


def _dense_attention(big):
    cb = D // LANES
    return pl.pallas_call(
        _attn_body,
        grid=(BATCH, N_HEADS // 2),
        in_specs=[pl.BlockSpec((SEQ, LANES), lambda b, p: (b, 4 * cb + p)),
                  pl.BlockSpec((SEQ, LANES), lambda b, p: (b, 5 * cb + p)),
                  pl.BlockSpec((SEQ, LANES), lambda b, p: (b, 6 * cb + p))],
        out_specs=pl.BlockSpec((SEQ, LANES), lambda b, p: (b, p)),
        out_shape=jax.ShapeDtypeStruct((NP_TOK, D), BF16),
        compiler_params=_cp(("arbitrary", "arbitrary"), 32),
        name="dense_attn",
    )(big, big, big)


NA_DR_PAD = NA_KROWS
NA_DR2 = 2 * MAX_WIN_H - 1 + 2 * NA_DR_PAD - 1


def _natten_body(q_ref, k_ref, v_ref, kc_ref, vc_ref, tab_ref, mask_ref, o_ref):
    rb = pl.program_id(2)
    ks_row = jnp.clip(NA_QROWS * rb - MAX_WIN_H // 2, 0, GRID_H - NA_KROWS)
    ks = pl.multiple_of(ks_row * GRID_W, GRID_W)
    q = q_ref[...]
    kl = k_ref[pl.ds(ks, NA_KROWS * GRID_W), :]
    vl = v_ref[pl.ds(ks, NA_KROWS * GRID_W), :]
    kc = kc_ref[0, 0].astype(BF16)
    vc = vc_ref[0, 0].astype(BF16)
    dr0 = NA_DR_PAD + ks_row - NA_QROWS * rb + MAX_WIN_H - 1
    window = mask_ref[0]
    masks = _pair_masks()
    outs = []
    for par, msk in enumerate(masks):
        qm = jnp.where(msk, q, jnp.zeros_like(q))
        bias = jnp.concatenate(
            [jnp.concatenate([tab_ref[par, dr0 - qr + 2 * kp] for kp in range(NA_KROWS // 2)], axis=1)
             for qr in range(NA_QROWS)], axis=0)
        sl = _dot_nt(qm, kl) * (N_HD ** -0.5) + (bias + window)
        sc = _dot_nt(qm, kc) * (N_HD ** -0.5)
        mx = jnp.maximum(jnp.max(sl, axis=-1, keepdims=True), jnp.max(sc, axis=-1, keepdims=True))
        el = jnp.exp(sl - mx)
        ec = jnp.exp(sc - mx)
        den = jnp.sum(el, axis=-1, keepdims=True) + jnp.sum(ec, axis=-1, keepdims=True)
        outs.append((_dot(el.astype(BF16), vl) + _dot(ec.astype(BF16), vc)) / den)
    o_ref[...] = jnp.where(masks[0], outs[0], outs[1]).astype(BF16)


def _na_pattern(rb):
    nrb = GRID_H // NA_QROWS
    return jnp.where(rb == 0, 0, jnp.where(rb == nrb - 1, 2, 1))


def _natten(big, cache_k, cache_v, bias_tab, window_mask, layer):
    cb = D // LANES
    nrb = GRID_H // NA_QROWS
    qrows = NA_QROWS * GRID_W
    q0 = NP_TOK // qrows
    b0 = NP_TOK // DEC_SEQ
    return pl.pallas_call(
        _natten_body,
        grid=(DEC_BATCH, N_HEADS // 2, nrb),
        in_specs=[pl.BlockSpec((qrows, LANES), lambda b, p, r: (q0 + b * nrb + r, 4 * cb + p)),
                  pl.BlockSpec((DEC_SEQ, LANES), lambda b, p, r: (b0 + b, 5 * cb + p)),
                  pl.BlockSpec((DEC_SEQ, LANES), lambda b, p, r: (b0 + b, 6 * cb + p)),
                  pl.BlockSpec((1, 1, PAST_LEN, LANES), lambda b, p, r: (b, layer, 0, p)),
                  pl.BlockSpec((1, 1, PAST_LEN, LANES), lambda b, p, r: (b, layer, 0, p)),
                  pl.BlockSpec((2, NA_DR2, GRID_W, 2 * GRID_W), lambda b, p, r: (p, 0, 0, 0)),
                  pl.BlockSpec((1, qrows, NA_KROWS * GRID_W), lambda b, p, r: (_na_pattern(r), 0, 0))],
        out_specs=pl.BlockSpec((qrows, LANES), lambda b, p, r: (b * nrb + r, p)),
        out_shape=jax.ShapeDtypeStruct((NS_TOK, D), BF16),
        compiler_params=_cp(("arbitrary", "arbitrary", "arbitrary"), 40),
        name="natten",
    )(big, big, big, cache_k, cache_v, bias_tab, window_mask)


def _na_bias_table(rpb):
    n = GRID_W
    n_dr = 2 * MAX_WIN_H - 1
    padl = n - WIN_W
    row = jnp.pad(rpb.astype(F32), ((0, 0), (0, 0), (padl, 2 * n - 1 - padl - (2 * WIN_W - 1))), constant_values=NEG)
    flat = jnp.broadcast_to(row[:, :, None, :], (N_HEADS, n_dr, n, 2 * n - 1)).reshape(N_HEADS, n_dr, n * (2 * n - 1))
    toep = flat[:, :, n - 1:n - 1 + n * (2 * n - 2)].reshape(N_HEADS, n_dr, n, 2 * n - 2)[..., :n]
    toep = jnp.pad(toep, ((0, 0), (NA_DR_PAD, NA_DR_PAD), (0, 0), (0, 0)), constant_values=NEG)
    return jnp.concatenate([toep[:, :-1], toep[:, 1:]], axis=-1)


def _na_window_mask():
    nrb = GRID_H // NA_QROWS
    tabs = []
    for rb in (0, 1, nrb - 1):
        ks = int(np.clip(NA_QROWS * rb - MAX_WIN_H // 2, 0, GRID_H - NA_KROWS))
        qi = np.arange(NA_QROWS * GRID_W)
        ki = np.arange(NA_KROWS * GRID_W)
        qrow = (NA_QROWS * rb + qi // GRID_W)[:, None]
        qcol = (qi % GRID_W)[:, None]
        krow = (ks + ki // GRID_W)[None, :]
        kcol = (ki % GRID_W)[None, :]
        rstart = np.clip(qrow - MAX_WIN_H // 2, 0, GRID_H - MAX_WIN_H)
        cstart = np.clip(qcol - WIN_W // 2, 0, GRID_W - WIN_W)
        valid = (krow >= rstart) & (krow < rstart + MAX_WIN_H) & (kcol >= cstart) & (kcol < cstart + WIN_W)
        tabs.append(np.where(valid, 0.0, NEG).astype(np.float32))
    return jnp.asarray(np.stack(tabs))


def _postmix_body(oap_ref, oas_ref, obp_ref, obs_ref, ga_ref, gb_ref, x_ref, g1_ref, nf_ref, sc2_ref, sh2_ref,
                  wa_ref, wb_ref, wo_ref, wr_ref, xo_ref, h2_ref, aff_ref):
    prompt = pl.program_id(0) < NP_TILES
    a = _dot(jnp.where(prompt, oap_ref[...], oas_ref[...]), wa_ref[...])
    b = _dot(jnp.where(prompt, obp_ref[...], obs_ref[...]), wb_ref[...])
    merged = _sigmoid(ga_ref[...].astype(F32)) * a + _sigmoid(gb_ref[...].astype(F32)) * b
    xn = x_ref[...] + g1_ref[0] * _dot(merged.astype(BF16), wo_ref[...])
    xo_ref[...] = xn
    h2 = (_rms(xn) * nf_ref[...]) * (1.0 + sc2_ref[0]) + sh2_ref[0]
    h2_ref[...] = h2
    h1, h2m, _ = _split3(h2)
    w1, w2, _ = _split3(wr_ref[...])
    lg = _dot_nt(w1, h1) + _dot_nt(w1, h2m) + _dot_nt(w2, h1)
    e = jnp.exp(lg - jnp.max(lg, axis=0, keepdims=True))
    aff_ref[...] = e / jnp.sum(e, axis=0, keepdims=True)


def _postmix(oa_p, oa_s, ob_p, ob_s, big, x, mod, norm_ffn, wa, wb, wo, wr_t):
    row = lambda i: (i, 0)
    const = lambda i: (0, 0)
    prow = lambda i: (jnp.minimum(i, NP_TILES - 1), 0)
    srow = lambda i: (jnp.maximum(i - NP_TILES, 0), 0)
    return pl.pallas_call(
        _postmix_body,
        grid=(N_TILES,),
        in_specs=[pl.BlockSpec((TM, D), prow), pl.BlockSpec((TM, D), srow),
                  pl.BlockSpec((TM, D), prow), pl.BlockSpec((TM, D), srow),
                  pl.BlockSpec((TM, D), lambda i: (i, 7)), pl.BlockSpec((TM, D), lambda i: (i, 8)),
                  pl.BlockSpec((TM, D), row),
                  _mod_spec(2), pl.BlockSpec((1, D), const), _mod_spec(4), _mod_spec(3),
                  pl.BlockSpec((D, D), const), pl.BlockSpec((D, D), const), pl.BlockSpec((D, D), const),
                  pl.BlockSpec((N_EXPERTS, D), const)],
        out_specs=[pl.BlockSpec((TM, D), row), pl.BlockSpec((TM, D), row),
                   pl.BlockSpec((N_EXPERTS, TM), lambda i: (0, i))],
        out_shape=[jax.ShapeDtypeStruct((N_TOK, D), F32), jax.ShapeDtypeStruct((N_TOK, D), F32),
                   jax.ShapeDtypeStruct((N_EXPERTS, N_TOK), F32)],
        compiler_params=_cp(("arbitrary",), 48),
        name="postmix",
    )(oa_p, oa_s, ob_p, ob_s, big, big, x, mod, norm_ffn, mod, mod, wa, wb, wo, wr_t)


def _route_body(a_ref, idx_ref, gv_ref, off_ref, pos_ref, *, R, cap):
    a = a_ref[...]
    capf = float(cap)

    def count_ge(thr):
        c = jnp.sum(jnp.where(a >= thr, 1.0, 0.0), axis=1, keepdims=True)
        return jnp.sum(c, axis=2, keepdims=True)

    def search(i, lo_bits):
        cand = lo_bits | jnp.left_shift(jnp.int32(1), 30 - i)
        return jnp.where(count_ge(pltpu.bitcast(cand, F32)) >= capf, cand, lo_bits)

    tau_bits = lax.fori_loop(0, 31, search, jnp.zeros((N_EXPERTS, 1, 1), I32))
    lo = pltpu.bitcast(tau_bits, F32)
    hi = pltpu.bitcast(jnp.maximum(tau_bits + 1, jnp.int32(MIN_NORMAL_BITS)), F32)

    def refine(i, lh):
        lo, hi = lh
        mid = lo + (hi - lo) * 0.5
        ok = count_ge(mid) >= capf
        return jnp.where(ok, mid, lo), jnp.where(ok, hi, mid)

    lo, hi = lax.fori_loop(0, 32, refine, (lo, hi))
    above = jnp.where(a >= hi, 1.0, 0.0)
    ties = jnp.where(a >= lo, 1.0, 0.0) - above
    need = capf - jnp.sum(jnp.sum(above, axis=1, keepdims=True), axis=2, keepdims=True)

    l0 = lax.broadcasted_iota(I32, (LANES, LANES), 0)
    l1 = lax.broadcasted_iota(I32, (LANES, LANES), 1)
    upper = jnp.where(l0 <= l1, 1.0, 0.0).astype(BF16)
    r0 = lax.broadcasted_iota(I32, (R, R), 0)
    r1 = lax.broadcasted_iota(I32, (R, R), 1)
    below = jnp.where(r1 < r0, 1.0, 0.0).astype(BF16)
    eye_r = r0 == r1
    r_row = lax.broadcasted_iota(I32, (1, R), 1).astype(F32)
    lane_row = lax.broadcasted_iota(I32, (1, LANES), 1).astype(F32)
    jcol = lax.broadcasted_iota(I32, (cap, 1), 0).astype(F32)

    def prefix(x):
        within = _dot(x.astype(BF16), upper)
        tot = jnp.broadcast_to(within[:, LANES - 1:LANES], (R, LANES))
        offs = _dot(below, tot.astype(BF16))
        return within + offs, offs

    for e in range(N_EXPERTS):
        eq = ties[e]
        cin_eq, _ = prefix(eq)
        sel = above[e] + eq * jnp.where(cin_eq - eq < need[e], 1.0, 0.0)
        cin, offs = prefix(sel)
        cend_row = jnp.sum(jnp.where(eye_r, cin[:, LANES - 1:LANES], 0.0), axis=0, keepdims=True)
        bcol = jnp.sum(jnp.where(cend_row <= jcol, 1.0, 0.0), axis=1, keepdims=True)
        onehot = jnp.where(bcol == r_row, 1.0, 0.0).astype(BF16)
        lcol = jnp.sum(jnp.where(_dot_x3(onehot, cin) <= jcol, 1.0, 0.0), axis=1, keepdims=True)
        idx_ref[e] = (bcol * LANES + lcol).astype(I32)
        gv_ref[e] = jnp.sum(jnp.where(lane_row == lcol, _dot_x3(onehot, a[e]), 0.0), axis=1, keepdims=True)
        off_ref[e] = offs[:, 0:1]
        pos_ref[e] = jnp.where(sel > 0.0, cin - 1.0, -1.0)


def _route(aff3, cap):
    R = aff3.shape[1]
    full = lambda s: pl.BlockSpec(s, lambda i: (0, 0, 0))
    return pl.pallas_call(
        functools.partial(_route_body, R=R, cap=cap),
        grid=(1,),
        in_specs=[full((N_EXPERTS, R, LANES))],
        out_specs=[full((N_EXPERTS, cap, 1)), full((N_EXPERTS, cap, 1)), full((N_EXPERTS, R, 1)),
                   full((N_EXPERTS, R, LANES))],
        out_shape=[jax.ShapeDtypeStruct((N_EXPERTS, cap, 1), I32),
                   jax.ShapeDtypeStruct((N_EXPERTS, cap, 1), F32),
                   jax.ShapeDtypeStruct((N_EXPERTS, R, 1), F32),
                   jax.ShapeDtypeStruct((N_EXPERTS, R, LANES), F32)],
        compiler_params=_cp(("arbitrary",), 48),
        name="route",
    )(aff3)


GATHER_UNROLL = 8


def _gather_copy(h_hbm, xs_ref, sem, src_row, dst_row, nrows):
    return pltpu.make_async_copy(h_hbm.at[pl.ds(src_row, nrows), :], xs_ref.at[pl.ds(dst_row, nrows), :], sem)


def _slab_copy(y_ref, ys_hbm, sem, e, s):
    return pltpu.make_async_copy(y_ref.at[:, s * LANES:(s + 1) * LANES], ys_hbm.at[e, :, s, :], sem)


def _ffn_body(idx_ref, h_hbm, wg_ref, wu_ref, wd_ref, gv_ref, ys_hbm, xs_ref, xb_ref, y_ref, gsem, osem):
    e = pl.program_id(0)
    f = pl.program_id(1)
    last_e = pl.num_programs(0) - 1

    @pl.when(f == 0)
    def _():
        def issue(j, carry):
            _gather_copy(h_hbm, xs_ref, gsem, idx_ref[e * CAP_TOT + j], j, 1).start()
            return carry
        lax.fori_loop(0, CAP_TOT, issue, 0, unroll=GATHER_UNROLL)

        @pl.when(e > 0)
        def _():
            for s in range(SLAB):
                _slab_copy(y_ref, ys_hbm, osem, e - 1, s).wait()

        _gather_copy(h_hbm, xs_ref, gsem, 0, 0, CAP_TOT).wait()
        xb_ref[...] = xs_ref[...].astype(BF16)

    xb = xb_ref[...]
    g = _dot(xb, wg_ref[0, 0].astype(BF16))
    u = _dot(xb, wu_ref[0, 0].astype(BF16))
    hid = (g * _sigmoid(g) * u).astype(BF16)
    y = _dot(hid, wd_ref[0, 0].astype(BF16))

    @pl.when(f == 0)
    def _():
        y_ref[...] = y

    @pl.when(f > 0)
    def _():
        y_ref[...] = y_ref[...] + y

    @pl.when(f == pl.num_programs(1) - 1)
    def _():
        y_ref[...] = y_ref[...] * gv_ref[0]
        for s in range(SLAB):
            _slab_copy(y_ref, ys_hbm, osem, e, s).start()

        @pl.when(e == last_e)
        def _():
            for s in range(SLAB):
                _slab_copy(y_ref, ys_hbm, osem, e, s).wait()


def _ffn(idx_flat, h2, wg, wu, wd, gv, layer):
    nf = EXPERT_FF // FF_CHUNK
    return pl.pallas_call(
        _ffn_body,
        grid_spec=pltpu.PrefetchScalarGridSpec(
            num_scalar_prefetch=1,
            grid=(N_EXPERTS, nf),
            in_specs=[pl.BlockSpec(memory_space=pl.ANY),
                      pl.BlockSpec((1, 1, D, FF_CHUNK), lambda e, f, idx: (layer, e, 0, f)),
                      pl.BlockSpec((1, 1, D, FF_CHUNK), lambda e, f, idx: (layer, e, 0, f)),
                      pl.BlockSpec((1, 1, FF_CHUNK, D), lambda e, f, idx: (layer, e, f, 0)),
                      pl.BlockSpec((1, CAP_TOT, 1), lambda e, f, idx: (e, 0, 0))],
            out_specs=pl.BlockSpec(memory_space=pl.ANY),
            scratch_shapes=[pltpu.VMEM((CAP_TOT, D), F32), pltpu.VMEM((CAP_TOT, D), BF16),
                            pltpu.VMEM((CAP_TOT, D), F32),
                            pltpu.SemaphoreType.DMA(()), pltpu.SemaphoreType.DMA(())]),
        out_shape=jax.ShapeDtypeStruct((N_EXPERTS, CAP_TOT, SLAB, LANES), F32),
        compiler_params=_cp(("arbitrary", "arbitrary"), 58),
        name="expert_ffn",
    )(idx_flat, h2, wg, wu, wd, gv)


MAX_PAIRS = N_EXPERTS * TM
SEG_BITS = TM.bit_length()


def _seg_copy(ys_hbm, w_ref, sem, e, src_row, dst_row, nrows):
    src = pl.multiple_of(src_row * SLAB, SLAB)
    dst = pl.multiple_of(dst_row * SLAB, SLAB)
    return pltpu.make_async_copy(ys_hbm.at[e, pl.ds(src, nrows * SLAB), :], w_ref.at[pl.ds(dst, nrows * SLAB), :], sem)


def _combine_body(lo_ref, hi_ref, ys_hbm, pos_ref, x_ref, g2_ref, nf_ref, o_ref, w_ref, acc_ref, sem, *, last):
    b = pl.program_id(0)
    lane = lax.broadcasted_iota(I32, (1, LANES), 1)
    off = jnp.int32(0)
    shift = jnp.zeros((1, LANES), F32)
    for e in range(N_EXPERTS):
        lo = lo_ref[e * N_TILES + b]
        n = hi_ref[e * N_TILES + b] - lo
        for bit in range(SEG_BITS - 1, -1, -1):
            done = (n >> (bit + 1)) << (bit + 1)

            @pl.when((n & (1 << bit)) != 0)
            def _(e=e, lo=lo, off=off, done=done, bit=bit):
                _seg_copy(ys_hbm, w_ref, sem, e, lo + done, off + done, 1 << bit).start()
        shift = jnp.where(lane == e, (off - lo).astype(F32), shift)
        off = off + n
    npairs = off
    for bit in range(MAX_PAIRS.bit_length() - 1, -1, -1):
        @pl.when((npairs & (1 << bit)) != 0)
        def _(bit=bit):
            _seg_copy(ys_hbm, w_ref, sem, 0, 0, 0, 1 << bit).wait()

    acc_ref[...] = jnp.zeros_like(acc_ref)
    pos = pos_ref[...]
    prow = jnp.where(pos >= 0.0, pos + shift[:, 0:N_EXPERTS], -1.0)
    sub = lax.broadcasted_iota(I32, (TM, 1), 0)
    pair = lax.broadcasted_iota(I32, (1, TM), 1).astype(F32)

    def body(c, carry):
        valid = (c * TM + sub) < npairs
        local = prow - (c * TM).astype(F32)
        onehot = jnp.zeros((TM, TM), F32)
        for e in range(N_EXPERTS):
            onehot = onehot + jnp.where(local[:, e:e + 1] == pair, 1.0, 0.0)
        onehot = onehot.astype(BF16)
        for s in range(SLAB):
            data = jnp.where(valid, w_ref[pl.ds(c * TM * SLAB + s, TM, stride=SLAB), :], 0.0)
            sl = slice(s * LANES, (s + 1) * LANES)
            acc_ref[:, sl] = acc_ref[:, sl] + _dot_x3(onehot, data)
        return carry

    lax.fori_loop(0, (npairs + TM - 1) // TM, body, 0)
    xn = x_ref[...] + g2_ref[0] * acc_ref[...]
    if last:
        xn = _rms(xn) * nf_ref[...]
    o_ref[...] = xn


def _combine(seg_lo, seg_hi, ys, pos_t, x, mod, norm_final, *, last):
    return pl.pallas_call(
        functools.partial(_combine_body, last=last),
        grid_spec=pltpu.PrefetchScalarGridSpec(
            num_scalar_prefetch=2,
            grid=(N_TILES,),
            in_specs=[pl.BlockSpec(memory_space=pl.ANY),
                      pl.BlockSpec((TM, N_EXPERTS), lambda i, lo, hi: (i, 0)),
                      pl.BlockSpec((TM, D), lambda i, lo, hi: (i, 0)),
                      pl.BlockSpec((1, 1, D), lambda i, lo, hi: (_mod_row(i), 0, 5)),
                      pl.BlockSpec((1, D), lambda i, lo, hi: (0, 0))],
            out_specs=pl.BlockSpec((TM, D), lambda i, lo, hi: (i, 0)),
            scratch_shapes=[pltpu.VMEM((MAX_PAIRS * SLAB, LANES), F32), pltpu.VMEM((TM, D), F32),
                            pltpu.SemaphoreType.DMA(())]),
        out_shape=jax.ShapeDtypeStruct((N_TOK, D), F32),
        compiler_params=_cp(("arbitrary",), 48),
        name="combine",
    )(seg_lo, seg_hi, ys, pos_t, x, mod, norm_final)


def _tile_segments(off_p, off_s):
    per = TM // LANES
    lo_p = off_p[:, ::per, 0].astype(I32)
    lo_s = off_s[:, ::per, 0].astype(I32) + CAP_P
    hi_p = jnp.concatenate([lo_p[:, 1:], jnp.full((N_EXPERTS, 1), CAP_P, I32)], axis=1)
    hi_s = jnp.concatenate([lo_s[:, 1:], jnp.full((N_EXPERTS, 1), CAP_TOT, I32)], axis=1)
    lo = jnp.concatenate([lo_p, lo_s], axis=1)
    hi = jnp.concatenate([hi_p, hi_s], axis=1)
    return lo.reshape(-1), hi.reshape(-1)


def kernel(x_prompt, x_sample, cache_na_k, cache_na_v, state_mlstm_C, state_mlstm_n, state_mlstm_m, c, c_ctx, w_ada, b_ada, norm_mix, norm_ffn, w_in, conv_qk, mlstm_gate_bias, mlstm_head_norm, na_rpb, w_branch_a, w_branch_b, w_out, w_router, w_expert_gate, w_expert_up, w_expert_down, norm_final):
    depth = w_in.shape[0]
    m_width = M_HEADS * M_HD
    n_gates = 4 * M_HEADS

    x = jnp.concatenate([x_prompt.reshape(NP_TOK, D), x_sample.reshape(NS_TOK, D)], axis=0)
    cond8 = jnp.concatenate([c_ctx[None], c, jnp.zeros((8 - 1 - DEC_BATCH, D), F32)], axis=0)
    mods = _adaln(cond8, w_ada, b_ada).reshape(depth, 8, 1, 6 * D)
    cos_t, sin_t = _rope_tables()
    cache_k = cache_na_k.reshape(DEC_BATCH, depth, PAST_LEN, D)
    cache_v = cache_na_v.reshape(DEC_BATCH, depth, PAST_LEN, D)
    norm_final2 = norm_final.reshape(1, D)
    window_mask = _na_window_mask()

    ks_, vs_, cs_, ns_, ms_ = [], [], [], [], []
    for l in range(depth):
        mod = mods[l]
        w_big, w_gates = _pack_w_in(w_in, l)
        gate_bias = jnp.pad(mlstm_gate_bias[l], (0, LANES - n_gates)).reshape(1, LANES)
        head_norm = mlstm_head_norm[l].reshape(1, m_width)

        big, gates, kv = _inproj(x, norm_mix[l].reshape(1, D), mod, w_big, w_gates)

        qk_p = _convprep(big, conv_qk[l], cos_t, sin_t, rope=False, nb=BATCH, T=SEQ, row0=0)
        qk_s = _convprep(big, conv_qk[l], cos_t, sin_t, rope=True, nb=DEC_BATCH, T=DEC_SEQ, row0=NP_TOK // DEC_SEQ)
        oa_p, c_new, n_new, m_new = _mlstm(qk_p, big, gates, gate_bias, head_norm, None,
                                           nb=BATCH, T=SEQ, row0=0, emit_state=True)
        state = (state_mlstm_C[:, l], state_mlstm_n[:, l].reshape(DEC_BATCH, 2, M_HEADS, 1, M_HD),
                 state_mlstm_m[:, l].reshape(DEC_BATCH, 2, M_HEADS, 1, 1))
        (oa_s,) = _mlstm(qk_s, big, gates, gate_bias, head_norm, state,
                         nb=DEC_BATCH, T=DEC_SEQ, row0=NP_TOK // DEC_SEQ, emit_state=False)

        ob_p = _dense_attention(big)
        ob_s = _natten(big, cache_k, cache_v, _na_bias_table(na_rpb[l]), window_mask, l)

        x, h2, aff_t = _postmix(oa_p, oa_s, ob_p, ob_s, big, x, mod, norm_ffn[l].reshape(1, D),
                                w_branch_a[l].astype(BF16), w_branch_b[l].astype(BF16), w_out[l].astype(BF16),
                                w_router[l].T)

        idx_p, gv_p, off_p, pos_p = _route(aff_t[:, :NP_TOK].reshape(N_EXPERTS, NP_TOK // LANES, LANES), CAP_P)
        idx_s, gv_s, off_s, pos_s = _route(aff_t[:, NP_TOK:].reshape(N_EXPERTS, NS_TOK // LANES, LANES), CAP_S)
        idx = jnp.concatenate([idx_p, idx_s + NP_TOK], axis=1)
        gv = jnp.concatenate([gv_p, gv_s], axis=1)
        pos_s = jnp.where(pos_s >= 0.0, pos_s + CAP_P, pos_s)
        pos_t = jnp.concatenate([pos_p.reshape(N_EXPERTS, NP_TOK), pos_s.reshape(N_EXPERTS, NS_TOK)], axis=1).T
        ys = _ffn(idx.reshape(-1), h2, w_expert_gate, w_expert_up, w_expert_down, gv, l)
        ys = ys.reshape(N_EXPERTS, CAP_TOT * SLAB, LANES)
        seg_lo, seg_hi = _tile_segments(off_p, off_s)
        x = _combine(seg_lo, seg_hi, ys, pos_t, x, mod, norm_final2, last=(l == depth - 1))

        ks_.append(kv[:NP_TOK, :D].reshape(BATCH, SEQ, N_HEADS, N_HD))
        vs_.append(kv[:NP_TOK, D:].reshape(BATCH, SEQ, N_HEADS, N_HD))
        cs_.append(c_new)
        ns_.append(n_new.reshape(BATCH, 2, M_HEADS, M_HD))
        ms_.append(m_new.reshape(BATCH, 2, M_HEADS))

    y_prompt = x[:NP_TOK].reshape(BATCH, SEQ, D)
    y_sample = x[NP_TOK:].reshape(DEC_BATCH, DEC_SEQ, D)
    return (y_prompt, y_sample, jnp.stack(ks_, axis=1), jnp.stack(vs_, axis=1),
            jnp.stack(cs_, axis=1), jnp.stack(ns_, axis=1), jnp.stack(ms_, axis=1))
```

```python
import functools

import numpy as np
import jax
import jax.numpy as jnp
from jax import lax
from jax.experimental import pallas as pl
from jax.experimental.pallas import tpu as pltpu

F32 = jnp.float32
BF16 = jnp.bfloat16
I32 = jnp.int32

D = 1024
BATCH, SEQ = 16, 256
DEC_BATCH, DEC_SEQ = 4, 2048
PAST_LEN = 512
NP_TOK = BATCH * SEQ
NS_TOK = DEC_BATCH * DEC_SEQ
N_TOK = NP_TOK + NS_TOK
GRID_W = 64
GRID_H = DEC_SEQ // GRID_W
M_HEADS, M_HD = 4, 256
CHUNK = 128
N_HEADS, N_HD = 16, 64
MAX_WIN_H, WIN_W = 8, 16
N_EXPERTS, EXPERT_FF = 16, 2048
CAP_P = 2 * NP_TOK // N_EXPERTS
CAP_S = 2 * NS_TOK // N_EXPERTS
CAP_TOT = CAP_P + CAP_S
ROPE_BASE = 10000.0
EPS = 1e-6
NEG = -1e30
MIN_NORMAL_BITS = 0x00800000

TM = 256
N_TILES = N_TOK // TM
NP_TILES = NP_TOK // TM
LANES = 128
NA_QROWS = 4
NA_KROWS = 12
FF_CHUNK = 512
SLAB = D // LANES


def _cp(sem, vmem_mb):
    return pltpu.CompilerParams(dimension_semantics=sem, vmem_limit_bytes=vmem_mb * 2 ** 20)


def _dot(a, b):
    return jnp.dot(a, b, preferred_element_type=F32)


def _dot_nt(a, b):
    return lax.dot_general(a, b, (((1,), (1,)), ((), ())), preferred_element_type=F32)


def _dot_tn(a, b):
    return lax.dot_general(a, b, (((0,), (0,)), ((), ())), preferred_element_type=F32)


def _split3(x):
    hi = x.astype(BF16)
    r = x - hi.astype(F32)
    mid = r.astype(BF16)
    lo = (r - mid.astype(F32)).astype(BF16)
    return hi, mid, lo


def _dot_x3(a_bf, x):
    hi, mid, lo = _split3(x)
    return _dot(a_bf, hi) + _dot(a_bf, mid) + _dot(a_bf, lo)


def _sigmoid(x):
    return 1.0 / (1.0 + jnp.exp(-x))


def _rms(x):
    return x * lax.rsqrt(jnp.mean(x * x, axis=-1, keepdims=True) + EPS)


def _mod_row(i, tm=TM):
    return jnp.where(i < NP_TOK // tm, 0, 1 + (i - NP_TOK // tm) // (DEC_SEQ // tm))


def _mod_spec(kind, tm=TM):
    return pl.BlockSpec((1, 1, D), lambda i: (_mod_row(i, tm), 0, kind))


def _adaln_body(c_ref, w_ref, b_ref, o_ref):
    c = c_ref[...]
    s = (c * _sigmoid(c)).astype(BF16)
    o_ref[0] = _dot(s, w_ref[0].astype(BF16)) + b_ref[0]


def _adaln(cond8, w_ada, b_ada):
    depth = w_ada.shape[0]
    tn = 1536
    return pl.pallas_call(
        _adaln_body,
        grid=(depth, 6 * D // tn),
        in_specs=[pl.BlockSpec((8, D), lambda l, j: (0, 0)),
                  pl.BlockSpec((1, D, tn), lambda l, j: (l, 0, j)),
                  pl.BlockSpec((1, 1, tn), lambda l, j: (l, 0, j))],
        out_specs=pl.BlockSpec((1, 8, tn), lambda l, j: (l, 0, j)),
        out_shape=jax.ShapeDtypeStruct((depth, 8, 6 * D), F32),
        compiler_params=_cp(("arbitrary", "arbitrary"), 40),
        name="adaln",
    )(cond8, w_ada, b_ada.reshape(depth, 1, 6 * D))


N_GATES = 4 * M_HEADS
GATE_COL = 4 * M_HEADS * M_HD


def _pack_body(w_ref, big_ref, gates_ref):
    w = w_ref[0]
    big_ref[:, 0:GATE_COL] = w[:, 0:GATE_COL].astype(BF16)
    big_ref[:, GATE_COL:9 * D] = w[:, GATE_COL + N_GATES:9 * D + N_GATES].astype(BF16)
    lane = lax.broadcasted_iota(I32, (1, LANES), 1)
    gates_ref[...] = jnp.where(lane < N_GATES, w[:, GATE_COL:GATE_COL + LANES], 0.0).astype(BF16)


def _pack_w_in(w, layer):
    tr = 128
    return pl.pallas_call(
        _pack_body,
        grid=(D // tr,),
        in_specs=[pl.BlockSpec((1, tr, w.shape[2]), lambda i: (layer, i, 0))],
        out_specs=[pl.BlockSpec((tr, 9 * D), lambda i: (i, 0)), pl.BlockSpec((tr, LANES), lambda i: (i, 0))],
        out_shape=[jax.ShapeDtypeStruct((D, 9 * D), BF16), jax.ShapeDtypeStruct((D, LANES), BF16)],
        compiler_params=_cp(("arbitrary",), 48),
        name="pack_w_in",
    )(w)


def _inproj_body(x_ref, g_ref, sc_ref, sh_ref, w_ref, wg_ref, big_ref, gates_ref, kv_ref):
    i = pl.program_id(0)
    h = (_rms(x_ref[...]) * g_ref[...]) * (1.0 + sc_ref[0]) + sh_ref[0]
    hb = h.astype(BF16)
    gates_ref[...] = _dot(hb, wg_ref[...])
    for c in range(9):
        r = _dot(hb, w_ref[:, c * D:(c + 1) * D])
        big_ref[:, c * D:(c + 1) * D] = r.astype(BF16)
        if c in (5, 6):
            @pl.when(i < NP_TOK // TM_IN)
            def _():
                kv_ref[:, (c - 5) * D:(c - 4) * D] = r

    @pl.when(i == NP_TOK // TM_IN)
    def _():
        kv_ref[...] = jnp.zeros_like(kv_ref)


TM_IN = 512


def _inproj(x, norm_g, mod, w_big, w_gates):
    np_tiles = NP_TOK // TM_IN
    return pl.pallas_call(
        _inproj_body,
        grid=(N_TOK // TM_IN,),
        in_specs=[pl.BlockSpec((TM_IN, D), lambda i: (i, 0)),
                  pl.BlockSpec((1, D), lambda i: (0, 0)),
                  _mod_spec(1, TM_IN), _mod_spec(0, TM_IN),
                  pl.BlockSpec((D, 9 * D), lambda i: (0, 0), pipeline_mode=pl.Buffered(1)),
                  pl.BlockSpec((D, LANES), lambda i: (0, 0))],
        out_specs=[pl.BlockSpec((TM_IN, 9 * D), lambda i: (i, 0)),
                   pl.BlockSpec((TM_IN, LANES), lambda i: (i, 0)),
                   pl.BlockSpec((TM_IN, 2 * D), lambda i: (jnp.minimum(i, np_tiles), 0))],
        out_shape=[jax.ShapeDtypeStruct((N_TOK, 9 * D), BF16),
                   jax.ShapeDtypeStruct((N_TOK, LANES), F32),
                   jax.ShapeDtypeStruct((NP_TOK + TM_IN, 2 * D), F32)],
        compiler_params=_cp(("arbitrary",), 60),
        name="inproj",
    )(x, norm_g, mod, mod, w_big, w_gates)


def _conv_body(x_ref, w_ref, cos_ref, sin_ref, o_ref, *, rope, T):
    j = pl.program_id(1)
    x = x_ref[...].astype(F32)
    w = w_ref[...]
    row = lax.broadcasted_iota(I32, (T, 1), 0)
    xp = jnp.where(row == 0, 0.0, pltpu.roll(x, 1, 0))
    xn = jnp.where(row == T - 1, 0.0, pltpu.roll(x, T - 1, 0))
    y = xp * w[0:1] + x * w[1:2] + xn * w[2:3]
    y = y * _sigmoid(y)
    scale = jnp.where(j >= M_HEADS, M_HD ** -0.5, 1.0)
    for hlf in range(2):
        sl = slice(hlf * LANES, (hlf + 1) * LANES)
        yh = y[:, sl]
        if rope:
            yh = yh * cos_ref[:, sl] + pltpu.roll(yh, LANES // 2, 1) * sin_ref[:, sl]
        o_ref[:, sl] = (yh * scale).astype(BF16)


def _convprep(big, conv_w, cos_t, sin_t, *, rope, nb, T, row0):
    return pl.pallas_call(
        functools.partial(_conv_body, rope=rope, T=T),
        grid=(nb, 2 * M_HEADS),
        in_specs=[pl.BlockSpec((T, M_HD), lambda b, j: (row0 + b, j)),
                  pl.BlockSpec((3, M_HD), lambda b, j: (0, j)),
                  pl.BlockSpec((T, M_HD), lambda b, j: (0, 0)),
                  pl.BlockSpec((T, M_HD), lambda b, j: (0, 0))],
        out_specs=pl.BlockSpec((T, M_HD), lambda b, j: (b, j)),
        out_shape=jax.ShapeDtypeStruct((nb * T, 2 * M_HEADS * M_HD), BF16),
        compiler_params=_cp(("arbitrary", "arbitrary"), 48),
        name="convprep",
    )(big, conv_w, cos_t, sin_t)


def _rope_tables():
    pos = np.arange(DEC_SEQ)
    rows = (pos // GRID_W).astype(np.float32)
    cols = (pos % GRID_W).astype(np.float32)
    nfreq = M_HD // 4
    inv = (ROPE_BASE ** (-np.arange(nfreq, dtype=np.float32) / nfreq)).astype(np.float32)
    d = np.arange(M_HD)
    p = np.where(d[None, :] < M_HD // 2, rows[:, None], cols[:, None]).astype(np.float32)
    ang = (p * inv[d % nfreq][None, :]).astype(np.float32)
    sign = np.where((d % (M_HD // 2)) < nfreq, -1.0, 1.0).astype(np.float32)
    return jnp.asarray(np.cos(ang), F32), jnp.asarray(np.sin(ang) * sign[None, :], F32)


def _mlstm_body(*refs, T, has_state, emit_state):
    q_ref, k_ref, v_ref, om_ref, g_ref, gb_ref, hn_ref = refs[:7]
    pos = 7
    if has_state:
        c0_ref, n0_ref, m0_ref = refs[pos:pos + 3]
        pos += 3
    oa_ref = refs[pos]
    pos += 1
    if emit_state:
        co_ref, no_ref, mo_ref = refs[pos:pos + 3]
        pos += 3
    hs_refs = refs[pos:pos + 2]
    c_refs = refs[pos + 2:pos + 4]
    n_refs = refs[pos + 4:pos + 6]

    hd = pl.program_id(1)
    nc = T // CHUNK
    lane = lax.broadcasted_iota(I32, (1, LANES), 1)
    r_i = lax.broadcasted_iota(I32, (CHUNK, CHUNK), 0)
    c_i = lax.broadcasted_iota(I32, (CHUNK, CHUNK), 1)
    eye = r_i == c_i
    causal = (r_i >= c_i, r_i <= c_i)
    cum = tuple(jnp.where(m, 1.0, 0.0).astype(BF16) for m in causal)

    def to_row(col):
        return jnp.sum(jnp.where(eye, col, 0.0), axis=0, keepdims=True)

    def pick(mat, colidx):
        return jnp.sum(jnp.where(lane == colidx, mat, 0.0), axis=1, keepdims=True)

    def chunk_step(d, c, m_prev):
        hs_ref, c_ref, n_ref = hs_refs[d], c_refs[d], n_refs[d]
        rows = pl.ds(pl.multiple_of(c * CHUNK, CHUNK), CHUNK)
        g = g_ref[rows, :] + gb_ref[...]
        lf = jnp.minimum(g, 0.0) - jnp.log(1.0 + jnp.exp(-jnp.abs(g)))
        bmat = _dot_x3(cum[d], lf)
        ig_col = pick(g, d * M_HEADS + hd)
        b_col = pick(bmat, 2 * M_HEADS + d * M_HEADS + hd)
        ig_row = to_row(ig_col)
        b_row = to_row(b_col)
        b_last = b_row[:, CHUNK - 1:CHUNK] if d == 0 else b_row[:, 0:1]
        logd = jnp.where(causal[d], b_col - b_row + ig_row, -jnp.inf)
        inter = b_col + m_prev
        m_row = jnp.maximum(inter, jnp.max(logd, axis=1, keepdims=True))
        dmat = jnp.exp(logd - m_row)
        s_inter = jnp.exp(inter - m_row)
        q = q_ref[rows, :]
        k = k_ref[rows, :]
        v = v_ref[rows, :]
        s = _dot_nt(q, k) * dmat
        num = _dot(s.astype(BF16), v) + s_inter * _dot(q, c_ref[...].astype(BF16))
        den = (jnp.sum(s, axis=1, keepdims=True)
               + s_inter * jnp.sum(q.astype(F32) * n_ref[...], axis=1, keepdims=True))
        hs_ref[rows, :] = num / jnp.maximum(jnp.abs(den), jnp.exp(-m_row))
        log_w = b_last - b_col + ig_col
        m_new = jnp.maximum(b_last + m_prev, jnp.max(log_w, axis=0, keepdims=True))
        w = jnp.exp(log_w - m_new)
        decay = jnp.exp(b_last + m_prev - m_new)
        kw = k.astype(F32) * w
        c_ref[...] = decay * c_ref[...] + _dot_tn(kw.astype(BF16), v)
        n_ref[...] = decay * n_ref[...] + jnp.sum(kw, axis=0, keepdims=True)
        return m_new

    m_init = []
    for d in range(2):
        if has_state:
            c_refs[d][...] = c0_ref[0, d, 0]
            n_refs[d][...] = n0_ref[0, d, 0]
            m_init.append(m0_ref[0, d, 0])
        else:
            c_refs[d][...] = jnp.zeros_like(c_refs[d])
            n_refs[d][...] = jnp.zeros_like(n_refs[d])
            m_init.append(jnp.zeros((1, 1), F32))

    def body(ci, ms):
        return chunk_step(0, ci, ms[0]), chunk_step(1, nc - 1 - ci, ms[1])

    m_fin = lax.fori_loop(0, nc, body, tuple(m_init))
    if emit_state:
        for d in range(2):
            co_ref[0, d, 0] = c_refs[d][...]
            no_ref[0, d, 0] = n_refs[d][...]
            mo_ref[0, d, 0] = m_fin[d]

    hm = _rms(hs_refs[0][...] + hs_refs[1][...]) * hn_ref[...]
    oa_ref[...] = (_sigmoid(om_ref[...].astype(F32)) * hm).astype(BF16)


def _mlstm(qk, big, gates, gate_bias, head_norm, state, *, nb, T, row0, emit_state):
    has_state = state is not None
    in_specs = [pl.BlockSpec((T, M_HD), lambda b, h: (b, h)),
                pl.BlockSpec((T, M_HD), lambda b, h: (b, M_HEADS + h)),
                pl.BlockSpec((T, M_HD), lambda b, h: (row0 + b, 2 * M_HEADS + h)),
                pl.BlockSpec((T, M_HD), lambda b, h: (row0 + b, 3 * M_HEADS + h)),
                pl.BlockSpec((T, LANES), lambda b, h: (row0 + b, 0)),
                pl.BlockSpec((1, LANES), lambda b, h: (0, 0)),
                pl.BlockSpec((1, M_HD), lambda b, h: (0, h))]
    args = [qk, qk, big, big, gates, gate_bias, head_norm]
    if has_state:
        c0, n0, m0 = state
        in_specs += [pl.BlockSpec((1, 2, 1, M_HD, M_HD), lambda b, h: (b, 0, h, 0, 0)),
                     pl.BlockSpec((1, 2, 1, 1, M_HD), lambda b, h: (b, 0, h, 0, 0)),
                     pl.BlockSpec((1, 2, 1, 1, 1), lambda b, h: (b, 0, h, 0, 0))]
        args += [c0, n0, m0]
    out_specs = [pl.BlockSpec((T, M_HD), lambda b, h: (b, h))]
    out_shape = [jax.ShapeDtypeStruct((nb * T, M_HEADS * M_HD), BF16)]
    if emit_state:
        out_specs += [pl.BlockSpec((1, 2, 1, M_HD, M_HD), lambda b, h: (b, 0, h, 0, 0)),
                      pl.BlockSpec((1, 2, 1, 1, M_HD), lambda b, h: (b, 0, h, 0, 0)),
                      pl.BlockSpec((1, 2, 1, 1, 1), lambda b, h: (b, 0, h, 0, 0))]
        out_shape += [jax.ShapeDtypeStruct((nb, 2, M_HEADS, M_HD, M_HD), F32),
                      jax.ShapeDtypeStruct((nb, 2, M_HEADS, 1, M_HD), F32),
                      jax.ShapeDtypeStruct((nb, 2, M_HEADS, 1, 1), F32)]
    return pl.pallas_call(
        functools.partial(_mlstm_body, T=T, has_state=has_state, emit_state=emit_state),
        grid=(nb, M_HEADS),
        in_specs=in_specs,
        out_specs=out_specs,
        out_shape=out_shape,
        scratch_shapes=[pltpu.VMEM((T, M_HD), F32)] * 2 + [pltpu.VMEM((M_HD, M_HD), F32)] * 2
        + [pltpu.VMEM((1, M_HD), F32)] * 2,
        compiler_params=_cp(("arbitrary", "arbitrary"), 40),
        name="mlstm",
    )(*args)


def _pair_masks():
    lane = lax.broadcasted_iota(I32, (1, LANES), 1)
    first = lane < N_HD
    return first, jnp.logical_not(first)


DA_PAIRS = 4


def _attn_body(q_ref, k_ref, v_ref, o_ref):
    first, second = _pair_masks()
    for p in range(DA_PAIRS):
        sl = slice(p * LANES, (p + 1) * LANES)
        q = q_ref[:, sl]
        q2 = jnp.concatenate([jnp.where(first, q, jnp.zeros_like(q)), jnp.where(second, q, jnp.zeros_like(q))], axis=0)
        s = _dot_nt(q2, k_ref[:, sl]) * (N_HD ** -0.5)
        e = jnp.exp(s - jnp.max(s, axis=-1, keepdims=True))
        o2 = _dot(e.astype(BF16), v_ref[:, sl]) / jnp.sum(e, axis=-1, keepdims=True)
        o_ref[:, sl] = jnp.where(first, o2[0:SEQ], o2[SEQ:2 * SEQ]).astype(BF16)


def _dense_attention(big):
    w = DA_PAIRS * LANES
    cb = D // w
    return pl.pallas_call(
        _attn_body,
        grid=(BATCH, cb),
        in_specs=[pl.BlockSpec((SEQ, w), lambda b, p: (b, 4 * cb + p)),
                  pl.BlockSpec((SEQ, w), lambda b, p: (b, 5 * cb + p)),
                  pl.BlockSpec((SEQ, w), lambda b, p: (b, 6 * cb + p))],
        out_specs=pl.BlockSpec((SEQ, w), lambda b, p: (b, p)),
        out_shape=jax.ShapeDtypeStruct((NP_TOK, D), BF16),
        compiler_params=_cp(("arbitrary", "arbitrary"), 32),
        name="dense_attn",
    )(big, big, big)


NA_DR2 = 2 * MAX_WIN_H - 2
NA_LOCAL = MAX_WIN_H * GRID_W


def _natten_body(q_ref, k_ref, v_ref, kc_ref, vc_ref, tab_ref, mask_ref, o_ref):
    rb = pl.program_id(2)
    kc = kc_ref[0, 0].astype(BF16)
    vc = vc_ref[0, 0].astype(BF16)
    colmask = mask_ref[...]
    first, second = _pair_masks()
    for qr in range(NA_QROWS):
        r = NA_QROWS * rb + qr
        rstart = jnp.clip(r - MAX_WIN_H // 2, 0, GRID_H - MAX_WIN_H)
        ks = pl.multiple_of(rstart * GRID_W, GRID_W)
        kl = k_ref[pl.ds(ks, NA_LOCAL), :]
        vl = v_ref[pl.ds(ks, NA_LOCAL), :]
        dr0 = rstart - r + MAX_WIN_H - 1
        qb = q_ref[qr * GRID_W:(qr + 1) * GRID_W, :]
        q2 = jnp.concatenate([jnp.where(first, qb, jnp.zeros_like(qb)),
                              jnp.where(second, qb, jnp.zeros_like(qb))], axis=0)
        bias = jnp.concatenate(
            [jnp.concatenate([tab_ref[par, dr0 + 2 * kp] for kp in range(MAX_WIN_H // 2)], axis=1) + colmask
             for par in range(2)], axis=0)
        sl = _dot_nt(q2, kl) * (N_HD ** -0.5) + bias
        sc = _dot_nt(q2, kc) * (N_HD ** -0.5)
        mx = jnp.maximum(jnp.max(sl, axis=-1, keepdims=True), jnp.max(sc, axis=-1, keepdims=True))
        el = jnp.exp(sl - mx)
        ec = jnp.exp(sc - mx)
        den = jnp.sum(el, axis=-1, keepdims=True) + jnp.sum(ec, axis=-1, keepdims=True)
        o2 = (_dot(el.astype(BF16), vl) + _dot(ec.astype(BF16), vc)) / den
        o_ref[qr * GRID_W:(qr + 1) * GRID_W, :] = jnp.where(first, o2[0:GRID_W], o2[GRID_W:2 * GRID_W]).astype(BF16)


def _natten(big, cache_k, cache_v, bias_tab, window_mask, layer):
    cb = D // LANES
    nrb = GRID_H // NA_QROWS
    qrows = NA_QROWS * GRID_W
    q0 = NP_TOK // qrows
    b0 = NP_TOK // DEC_SEQ
    return pl.pallas_call(
        _natten_body,
        grid=(DEC_BATCH, N_HEADS // 2, nrb),
        in_specs=[pl.BlockSpec((qrows, LANES), lambda b, p, r: (q0 + b * nrb + r, 4 * cb + p)),
                  pl.BlockSpec((DEC_SEQ, LANES), lambda b, p, r: (b0 + b, 5 * cb + p)),
                  pl.BlockSpec((DEC_SEQ, LANES), lambda b, p, r: (b0 + b, 6 * cb + p)),
                  pl.BlockSpec((1, 1, PAST_LEN, LANES), lambda b, p, r: (b, layer, 0, p)),
                  pl.BlockSpec((1, 1, PAST_LEN, LANES), lambda b, p, r: (b, layer, 0, p)),
                  pl.BlockSpec((2, NA_DR2, GRID_W, 2 * GRID_W), lambda b, p, r: (p, 0, 0, 0)),
                  pl.BlockSpec((GRID_W, NA_LOCAL), lambda b, p, r: (0, 0))],
        out_specs=pl.BlockSpec((qrows, LANES), lambda b, p, r: (b * nrb + r, p)),
        out_shape=jax.ShapeDtypeStruct((NS_TOK, D), BF16),
        compiler_params=_cp(("arbitrary", "arbitrary", "arbitrary"), 40),
        name="natten",
    )(big, big, big, cache_k, cache_v, bias_tab, window_mask)


def _na_bias_table(rpb):
    n = GRID_W
    n_dr = 2 * MAX_WIN_H - 1
    padl = n - WIN_W
    row = jnp.pad(rpb.astype(F32), ((0, 0), (0, 0), (padl, 2 * n - 1 - padl - (2 * WIN_W - 1))), constant_values=NEG)
    flat = jnp.broadcast_to(row[:, :, None, :], (N_HEADS, n_dr, n, 2 * n - 1)).reshape(N_HEADS, n_dr, n * (2 * n - 1))
    toep = flat[:, :, n - 1:n - 1 + n * (2 * n - 2)].reshape(N_HEADS, n_dr, n, 2 * n - 2)[..., :n]
    return jnp.concatenate([toep[:, :-1], toep[:, 1:]], axis=-1)


def _na_window_mask():
    qcol = np.arange(GRID_W)[:, None]
    kcol = (np.arange(NA_LOCAL) % GRID_W)[None, :]
    cstart = np.clip(qcol - WIN_W // 2, 0, GRID_W - WIN_W)
    valid = (kcol >= cstart) & (kcol < cstart + WIN_W)
    return jnp.asarray(np.where(valid, 0.0, NEG).astype(np.float32))


def _postmix_body(oap_ref, oas_ref, obp_ref, obs_ref, ga_ref, gb_ref, x_ref, g1_ref, nf_ref, sc2_ref, sh2_ref,
                  wa_ref, wb_ref, wo_ref, wr_ref, xo_ref, h2_ref, aff_ref):
    prompt = pl.program_id(0) < NP_TILES
    a = _dot(jnp.where(prompt, oap_ref[...], oas_ref[...]), wa_ref[...])
    b = _dot(jnp.where(prompt, obp_ref[...], obs_ref[...]), wb_ref[...])
    merged = _sigmoid(ga_ref[...].astype(F32)) * a + _sigmoid(gb_ref[...].astype(F32)) * b
    xn = x_ref[...] + g1_ref[0] * _dot(merged.astype(BF16), wo_ref[...])
    xo_ref[...] = xn
    h2 = (_rms(xn) * nf_ref[...]) * (1.0 + sc2_ref[0]) + sh2_ref[0]
    h2_ref[...] = h2
    h1, h2m, _ = _split3(h2)
    w1, w2, _ = _split3(wr_ref[...])
    lg = _dot_nt(w1, h1) + _dot_nt(w1, h2m) + _dot_nt(w2, h1)
    e = jnp.exp(lg - jnp.max(lg, axis=0, keepdims=True))
    aff_ref[...] = e / jnp.sum(e, axis=0, keepdims=True)


def _postmix(oa_p, oa_s, ob_p, ob_s, big, x, mod, norm_ffn, wa, wb, wo, wr_t):
    row = lambda i: (i, 0)
    const = lambda i: (0, 0)
    prow = lambda i: (jnp.minimum(i, NP_TILES - 1), 0)
    srow = lambda i: (jnp.maximum(i - NP_TILES, 0), 0)
    return pl.pallas_call(
        _postmix_body,
        grid=(N_TILES,),
        in_specs=[pl.BlockSpec((TM, D), prow), pl.BlockSpec((TM, D), srow),
                  pl.BlockSpec((TM, D), prow), pl.BlockSpec((TM, D), srow),
                  pl.BlockSpec((TM, D), lambda i: (i, 7)), pl.BlockSpec((TM, D), lambda i: (i, 8)),
                  pl.BlockSpec((TM, D), row),
                  _mod_spec(2), pl.BlockSpec((1, D), const), _mod_spec(4), _mod_spec(3),
                  pl.BlockSpec((D, D), const), pl.BlockSpec((D, D), const), pl.BlockSpec((D, D), const),
                  pl.BlockSpec((N_EXPERTS, D), const)],
        out_specs=[pl.BlockSpec((TM, D), row), pl.BlockSpec((TM, D), row),
                   pl.BlockSpec((N_EXPERTS, TM), lambda i: (0, i))],
        out_shape=[jax.ShapeDtypeStruct((N_TOK, D), F32), jax.ShapeDtypeStruct((N_TOK, D), F32),
                   jax.ShapeDtypeStruct((N_EXPERTS, N_TOK), F32)],
        compiler_params=_cp(("arbitrary",), 48),
        name="postmix",
    )(oa_p, oa_s, ob_p, ob_s, big, big, x, mod, norm_ffn, mod, mod, wa, wb, wo, wr_t)


def _route_body(a_ref, idx_ref, gv_ref, off_ref, pos_ref, *, R, cap):
    a = a_ref[...]
    capf = float(cap)

    def count_ge(thr):
        c = jnp.sum(jnp.where(a >= thr, 1.0, 0.0), axis=1, keepdims=True)
        return jnp.sum(c, axis=2, keepdims=True)

    def search(i, lo_bits):
        cand = lo_bits | jnp.left_shift(jnp.int32(1), 30 - i)
        return jnp.where(count_ge(pltpu.bitcast(cand, F32)) >= capf, cand, lo_bits)

    tau_bits = lax.fori_loop(0, 31, search, jnp.zeros((N_EXPERTS, 1, 1), I32))
    lo = pltpu.bitcast(tau_bits, F32)
    hi = pltpu.bitcast(jnp.maximum(tau_bits + 1, jnp.int32(MIN_NORMAL_BITS)), F32)

    def refine(i, lh):
        lo, hi = lh
        mid = lo + (hi - lo) * 0.5
        ok = count_ge(mid) >= capf
        return jnp.where(ok, mid, lo), jnp.where(ok, hi, mid)

    lo, hi = lax.fori_loop(0, 32, refine, (lo, hi))
    above = jnp.where(a >= hi, 1.0, 0.0)
    ties = jnp.where(a >= lo, 1.0, 0.0) - above
    need = capf - jnp.sum(jnp.sum(above, axis=1, keepdims=True), axis=2, keepdims=True)

    l0 = lax.broadcasted_iota(I32, (LANES, LANES), 0)
    l1 = lax.broadcasted_iota(I32, (LANES, LANES), 1)
    upper = jnp.where(l0 <= l1, 1.0, 0.0).astype(BF16)
    r0 = lax.broadcasted_iota(I32, (R, R), 0)
    r1 = lax.broadcasted_iota(I32, (R, R), 1)
    below = jnp.where(r1 < r0, 1.0, 0.0).astype(BF16)
    eye_r = r0 == r1
    r_row = lax.broadcasted_iota(I32, (1, R), 1).astype(F32)
    lane_row = lax.broadcasted_iota(I32, (1, LANES), 1).astype(F32)
    jcol = lax.broadcasted_iota(I32, (cap, 1), 0).astype(F32)

    def prefix(x):
        within = _dot(x.astype(BF16), upper)
        tot = jnp.broadcast_to(within[:, LANES - 1:LANES], (R, LANES))
        offs = _dot(below, tot.astype(BF16))
        return within + offs, offs

    for e in range(N_EXPERTS):
        eq = ties[e]
        cin_eq, _ = prefix(eq)
        sel = above[e] + eq * jnp.where(cin_eq - eq < need[e], 1.0, 0.0)
        cin, offs = prefix(sel)
        cend_row = jnp.sum(jnp.where(eye_r, cin[:, LANES - 1:LANES], 0.0), axis=0, keepdims=True)
        bcol = jnp.sum(jnp.where(cend_row <= jcol, 1.0, 0.0), axis=1, keepdims=True)
        onehot = jnp.where(bcol == r_row, 1.0, 0.0).astype(BF16)
        lcol = jnp.sum(jnp.where(_dot_x3(onehot, cin) <= jcol, 1.0, 0.0), axis=1, keepdims=True)
        idx_ref[e] = (bcol * LANES + lcol).astype(I32)
        gv_ref[e] = jnp.sum(jnp.where(lane_row == lcol, _dot_x3(onehot, a[e]), 0.0), axis=1, keepdims=True)
        off_ref[e] = offs[:, 0:1]
        pos_ref[e] = jnp.where(sel > 0.0, cin - 1.0, -1.0)


def _route(aff3, cap):
    R = aff3.shape[1]
    full = lambda s: pl.BlockSpec(s, lambda i: (0, 0, 0))
    return pl.pallas_call(
        functools.partial(_route_body, R=R, cap=cap),
        grid=(1,),
        in_specs=[full((N_EXPERTS, R, LANES))],
        out_specs=[full((N_EXPERTS, cap, 1)), full((N_EXPERTS, cap, 1)), full((N_EXPERTS, R, 1)),
                   full((N_EXPERTS, R, LANES))],
        out_shape=[jax.ShapeDtypeStruct((N_EXPERTS, cap, 1), I32),
                   jax.ShapeDtypeStruct((N_EXPERTS, cap, 1), F32),
                   jax.ShapeDtypeStruct((N_EXPERTS, R, 1), F32),
                   jax.ShapeDtypeStruct((N_EXPERTS, R, LANES), F32)],
        compiler_params=_cp(("arbitrary",), 48),
        name="route",
    )(aff3)


GATHER_UNROLL = 8


def _gather_copy(h_hbm, xs_ref, sem, src_row, dst_row, nrows):
    return pltpu.make_async_copy(h_hbm.at[pl.ds(src_row, nrows), :], xs_ref.at[pl.ds(dst_row, nrows), :], sem)


def _slab_copy(y_ref, ys_hbm, sem, e, s):
    return pltpu.make_async_copy(y_ref.at[:, s * LANES:(s + 1) * LANES], ys_hbm.at[e, :, s, :], sem)


def _ffn_body(idx_ref, h_hbm, wg_ref, wu_ref, wd_ref, gv_ref, ys_hbm, xs_ref, xb_ref, y_ref, gsem, osem):
    e = pl.program_id(0)
    f = pl.program_id(1)
    last_e = pl.num_programs(0) - 1

    @pl.when(f == 0)
    def _():
        def issue(j, carry):
            _gather_copy(h_hbm, xs_ref, gsem, idx_ref[e * CAP_TOT + j], j, 1).start()
            return carry
        lax.fori_loop(0, CAP_TOT, issue, 0, unroll=GATHER_UNROLL)

        @pl.when(e > 0)
        def _():
            for s in range(SLAB):
                _slab_copy(y_ref, ys_hbm, osem, e - 1, s).wait()

        _gather_copy(h_hbm, xs_ref, gsem, 0, 0, CAP_TOT).wait()
        xb_ref[...] = xs_ref[...].astype(BF16)

    xb = xb_ref[...]
    g = _dot(xb, wg_ref[0, 0].astype(BF16))
    u = _dot(xb, wu_ref[0, 0].astype(BF16))
    hid = (g * _sigmoid(g) * u).astype(BF16)
    y = _dot(hid, wd_ref[0, 0].astype(BF16))

    @pl.when(f == 0)
    def _():
        y_ref[...] = y

    @pl.when(f > 0)
    def _():
        y_ref[...] = y_ref[...] + y

    @pl.when(f == pl.num_programs(1) - 1)
    def _():
        y_ref[...] = y_ref[...] * gv_ref[0]
        for s in range(SLAB):
            _slab_copy(y_ref, ys_hbm, osem, e, s).start()

        @pl.when(e == last_e)
        def _():
            for s in range(SLAB):
                _slab_copy(y_ref, ys_hbm, osem, e, s).wait()


def _ffn(idx_flat, h2, wg, wu, wd, gv, layer):
    nf = EXPERT_FF // FF_CHUNK
    return pl.pallas_call(
        _ffn_body,
        grid_spec=pltpu.PrefetchScalarGridSpec(
            num_scalar_prefetch=1,
            grid=(N_EXPERTS, nf),
            in_specs=[pl.BlockSpec(memory_space=pl.ANY),
                      pl.BlockSpec((1, 1, D, FF_CHUNK), lambda e, f, idx: (layer, e, 0, f)),
                      pl.BlockSpec((1, 1, D, FF_CHUNK), lambda e, f, idx: (layer, e, 0, f)),
                      pl.BlockSpec((1, 1, FF_CHUNK, D), lambda e, f, idx: (layer, e, f, 0)),
                      pl.BlockSpec((1, CAP_TOT, 1), lambda e, f, idx: (e, 0, 0))],
            out_specs=pl.BlockSpec(memory_space=pl.ANY),
            scratch_shapes=[pltpu.VMEM((CAP_TOT, D), F32), pltpu.VMEM((CAP_TOT, D), BF16),
                            pltpu.VMEM((CAP_TOT, D), F32),
                            pltpu.SemaphoreType.DMA(()), pltpu.SemaphoreType.DMA(())]),
        out_shape=jax.ShapeDtypeStruct((N_EXPERTS, CAP_TOT, SLAB, LANES), F32),
        compiler_params=_cp(("arbitrary", "arbitrary"), 58),
        name="expert_ffn",
    )(idx_flat, h2, wg, wu, wd, gv)


MAX_PAIRS = N_EXPERTS * TM
SEG_BITS = TM.bit_length()


def _seg_copy(ys_hbm, w_ref, sem, e, src_row, dst_row, nrows):
    src = pl.multiple_of(src_row * SLAB, SLAB)
    dst = pl.multiple_of(dst_row * SLAB, SLAB)
    return pltpu.make_async_copy(ys_hbm.at[e, pl.ds(src, nrows * SLAB), :], w_ref.at[pl.ds(dst, nrows * SLAB), :], sem)


def _combine_body(lo_ref, hi_ref, ys_hbm, pos_ref, x_ref, g2_ref, nf_ref, o_ref, w_ref, acc_ref, oh_ref, sem, *, last):
    b = pl.program_id(0)
    lane = lax.broadcasted_iota(I32, (1, LANES), 1)
    off = jnp.int32(0)
    shift = jnp.zeros((1, LANES), F32)
    seg = []
    for e in range(N_EXPERTS):
        lo = lo_ref[e * N_TILES + b]
        n = hi_ref[e * N_TILES + b] - lo
        seg.append((off, off + n))
        for bit in range(SEG_BITS - 1, -1, -1):
            done = (n >> (bit + 1)) << (bit + 1)

            @pl.when((n & (1 << bit)) != 0)
            def _(e=e, lo=lo, off=off, done=done, bit=bit):
                _seg_copy(ys_hbm, w_ref, sem, e, lo + done, off + done, 1 << bit).start()
        shift = jnp.where(lane == e, (off - lo).astype(F32), shift)
        off = off + n
    npairs = off
    for bit in range(MAX_PAIRS.bit_length() - 1, -1, -1):
        @pl.when((npairs & (1 << bit)) != 0)
        def _(bit=bit):
            _seg_copy(ys_hbm, w_ref, sem, 0, 0, 0, 1 << bit).wait()

    acc_ref[...] = jnp.zeros_like(acc_ref)
    pos = pos_ref[...]
    prow = jnp.where(pos >= 0.0, pos + shift[:, 0:N_EXPERTS], -1.0)
    sub = lax.broadcasted_iota(I32, (TM, 1), 0)
    pair = lax.broadcasted_iota(I32, (1, TM), 1).astype(F32)

    def body(c, carry):
        valid = (c * TM + sub) < npairs
        local = prow - (c * TM).astype(F32)
        oh_ref[...] = jnp.zeros_like(oh_ref)
        for e in range(N_EXPERTS):
            @pl.when((seg[e][0] < (c + 1) * TM) & (seg[e][1] > c * TM))
            def _(e=e):
                oh_ref[...] += jnp.where(local[:, e:e + 1] == pair, 1.0, 0.0)
        onehot = oh_ref[...].astype(BF16)
        for s in range(SLAB):
            data = jnp.where(valid, w_ref[pl.ds(c * TM * SLAB + s, TM, stride=SLAB), :], 0.0)
            hi, mid, _ = _split3(data)
            sl = slice(s * LANES, (s + 1) * LANES)
            acc_ref[:, sl] = acc_ref[:, sl] + (_dot(onehot, hi) + _dot(onehot, mid))
        return carry

    lax.fori_loop(0, (npairs + TM - 1) // TM, body, 0)
    xn = x_ref[...] + g2_ref[0] * acc_ref[...]
    if last:
        xn = _rms(xn) * nf_ref[...]
    o_ref[...] = xn


def _combine(seg_lo, seg_hi, ys, pos_t, x, mod, norm_final, *, last):
    return pl.pallas_call(
        functools.partial(_combine_body, last=last),
        grid_spec=pltpu.PrefetchScalarGridSpec(
            num_scalar_prefetch=2,
            grid=(N_TILES,),
            in_specs=[pl.BlockSpec(memory_space=pl.ANY),
                      pl.BlockSpec((TM, N_EXPERTS), lambda i, lo, hi: (i, 0)),
                      pl.BlockSpec((TM, D), lambda i, lo, hi: (i, 0)),
                      pl.BlockSpec((1, 1, D), lambda i, lo, hi: (_mod_row(i), 0, 5)),
                      pl.BlockSpec((1, D), lambda i, lo, hi: (0, 0))],
            out_specs=pl.BlockSpec((TM, D), lambda i, lo, hi: (i, 0)),
            scratch_shapes=[pltpu.VMEM((MAX_PAIRS * SLAB, LANES), F32), pltpu.VMEM((TM, D), F32),
                            pltpu.VMEM((TM, TM), F32), pltpu.SemaphoreType.DMA(())]),
        out_shape=jax.ShapeDtypeStruct((N_TOK, D), F32),
        compiler_params=_cp(("arbitrary",), 48),
        name="combine",
    )(seg_lo, seg_hi, ys, pos_t, x, mod, norm_final)


def _tile_segments(off_p, off_s):
    per = TM // LANES
    lo_p = off_p[:, ::per, 0].astype(I32)
    lo_s = off_s[:, ::per, 0].astype(I32) + CAP_P
    hi_p = jnp.concatenate([lo_p[:, 1:], jnp.full((N_EXPERTS, 1), CAP_P, I32)], axis=1)
    hi_s = jnp.concatenate([lo_s[:, 1:], jnp.full((N_EXPERTS, 1), CAP_TOT, I32)], axis=1)
    lo = jnp.concatenate([lo_p, lo_s], axis=1)
    hi = jnp.concatenate([hi_p, hi_s], axis=1)
    return lo.reshape(-1), hi.reshape(-1)


def kernel(x_prompt, x_sample, cache_na_k, cache_na_v, state_mlstm_C, state_mlstm_n, state_mlstm_m, c, c_ctx, w_ada, b_ada, norm_mix, norm_ffn, w_in, conv_qk, mlstm_gate_bias, mlstm_head_norm, na_rpb, w_branch_a, w_branch_b, w_out, w_router, w_expert_gate, w_expert_up, w_expert_down, norm_final):
    depth = w_in.shape[0]
    m_width = M_HEADS * M_HD
    n_gates = 4 * M_HEADS

    x = jnp.concatenate([x_prompt.reshape(NP_TOK, D), x_sample.reshape(NS_TOK, D)], axis=0)
    cond8 = jnp.concatenate([c_ctx[None], c, jnp.zeros((8 - 1 - DEC_BATCH, D), F32)], axis=0)
    mods = _adaln(cond8, w_ada, b_ada).reshape(depth, 8, 1, 6 * D)
    cos_t, sin_t = _rope_tables()
    cache_k = cache_na_k.reshape(DEC_BATCH, depth, PAST_LEN, D)
    cache_v = cache_na_v.reshape(DEC_BATCH, depth, PAST_LEN, D)
    norm_final2 = norm_final.reshape(1, D)
    window_mask = _na_window_mask()

    ks_, vs_, cs_, ns_, ms_ = [], [], [], [], []
    for l in range(depth):
        mod = mods[l]
        w_big, w_gates = _pack_w_in(w_in, l)
        gate_bias = jnp.pad(mlstm_gate_bias[l], (0, LANES - n_gates)).reshape(1, LANES)
        head_norm = mlstm_head_norm[l].reshape(1, m_width)

        big, gates, kv = _inproj(x, norm_mix[l].reshape(1, D), mod, w_big, w_gates)

        qk_p = _convprep(big, conv_qk[l], cos_t, sin_t, rope=False, nb=BATCH, T=SEQ, row0=0)
        qk_s = _convprep(big, conv_qk[l], cos_t, sin_t, rope=True, nb=DEC_BATCH, T=DEC_SEQ, row0=NP_TOK // DEC_SEQ)
        oa_p, c_new, n_new, m_new = _mlstm(qk_p, big, gates, gate_bias, head_norm, None,
                                           nb=BATCH, T=SEQ, row0=0, emit_state=True)
        state = (state_mlstm_C[:, l], state_mlstm_n[:, l].reshape(DEC_BATCH, 2, M_HEADS, 1, M_HD),
                 state_mlstm_m[:, l].reshape(DEC_BATCH, 2, M_HEADS, 1, 1))
        (oa_s,) = _mlstm(qk_s, big, gates, gate_bias, head_norm, state,
                         nb=DEC_BATCH, T=DEC_SEQ, row0=NP_TOK // DEC_SEQ, emit_state=False)

        ob_p = _dense_attention(big)
        ob_s = _natten(big, cache_k, cache_v, _na_bias_table(na_rpb[l]), window_mask, l)

        x, h2, aff_t = _postmix(oa_p, oa_s, ob_p, ob_s, big, x, mod, norm_ffn[l].reshape(1, D),
                                w_branch_a[l].astype(BF16), w_branch_b[l].astype(BF16), w_out[l].astype(BF16),
                                w_router[l].T)

        idx_p, gv_p, off_p, pos_p = _route(aff_t[:, :NP_TOK].reshape(N_EXPERTS, NP_TOK // LANES, LANES), CAP_P)
        idx_s, gv_s, off_s, pos_s = _route(aff_t[:, NP_TOK:].reshape(N_EXPERTS, NS_TOK // LANES, LANES), CAP_S)
        idx = jnp.concatenate([idx_p, idx_s + NP_TOK], axis=1)
        gv = jnp.concatenate([gv_p, gv_s], axis=1)
        pos_s = jnp.where(pos_s >= 0.0, pos_s + CAP_P, pos_s)
        pos_t = jnp.concatenate([pos_p.reshape(N_EXPERTS, NP_TOK), pos_s.reshape(N_EXPERTS, NS_TOK)], axis=1).T
        ys = _ffn(idx.reshape(-1), h2, w_expert_gate, w_expert_up, w_expert_down, gv, l)
        ys = ys.reshape(N_EXPERTS, CAP_TOT * SLAB, LANES)
        seg_lo, seg_hi = _tile_segments(off_p, off_s)
        x = _combine(seg_lo, seg_hi, ys, pos_t, x, mod, norm_final2, last=(l == depth - 1))

        ks_.append(kv[:NP_TOK, :D].reshape(BATCH, SEQ, N_HEADS, N_HD))
        vs_.append(kv[:NP_TOK, D:].reshape(BATCH, SEQ, N_HEADS, N_HD))
        cs_.append(c_new)
        ns_.append(n_new.reshape(BATCH, 2, M_HEADS, M_HD))
        ms_.append(m_new.reshape(BATCH, 2, M_HEADS))

    y_prompt = x[:NP_TOK].reshape(BATCH, SEQ, D)
    y_sample = x[NP_TOK:].reshape(DEC_BATCH, DEC_SEQ, D)
    return (y_prompt, y_sample, jnp.stack(ks_, axis=1), jnp.stack(vs_, axis=1),
            jnp.stack(cs_, axis=1), jnp.stack(ns_, axis=1), jnp.stack(ms_, axis=1))
```

```python
import functools

import numpy as np
import jax
import jax.numpy as jnp
from jax import lax
from jax.experimental import pallas as pl
from jax.experimental.pallas import tpu as pltpu

F32 = jnp.float32
BF16 = jnp.bfloat16
I32 = jnp.int32

D = 1024
BATCH, SEQ = 16, 256
DEC_BATCH, DEC_SEQ = 4, 2048
PAST_LEN = 512
NP_TOK = BATCH * SEQ
NS_TOK = DEC_BATCH * DEC_SEQ
N_TOK = NP_TOK + NS_TOK
GRID_W = 64
GRID_H = DEC_SEQ // GRID_W
M_HEADS, M_HD = 4, 256
CHUNK = 128
N_HEADS, N_HD = 16, 64
MAX_WIN_H, WIN_W = 8, 16
N_EXPERTS, EXPERT_FF = 16, 2048
CAP_P = 2 * NP_TOK // N_EXPERTS
CAP_S = 2 * NS_TOK // N_EXPERTS
CAP_TOT = CAP_P + CAP_S
ROPE_BASE = 10000.0
EPS = 1e-6
NEG = -1e30
MIN_NORMAL_BITS = 0x00800000

TM = 256
N_TILES = N_TOK // TM
NP_TILES = NP_TOK // TM
LANES = 128
NA_QROWS = 4
NA_KROWS = 12
FF_CHUNK = 512
SLAB = D // LANES


def _cp(sem, vmem_mb):
    return pltpu.CompilerParams(dimension_semantics=sem, vmem_limit_bytes=vmem_mb * 2 ** 20)


def _dot(a, b):
    return jnp.dot(a, b, preferred_element_type=F32)


def _dot_nt(a, b):
    return lax.dot_general(a, b, (((1,), (1,)), ((), ())), preferred_element_type=F32)


def _dot_tn(a, b):
    return lax.dot_general(a, b, (((0,), (0,)), ((), ())), preferred_element_type=F32)


def _split3(x):
    hi = x.astype(BF16)
    r = x - hi.astype(F32)
    mid = r.astype(BF16)
    lo = (r - mid.astype(F32)).astype(BF16)
    return hi, mid, lo


def _dot_x3(a_bf, x):
    hi, mid, lo = _split3(x)
    return _dot(a_bf, hi) + _dot(a_bf, mid) + _dot(a_bf, lo)


def _sigmoid(x):
    return 1.0 / (1.0 + jnp.exp(-x))


def _rms(x):
    return x * lax.rsqrt(jnp.mean(x * x, axis=-1, keepdims=True) + EPS)


def _mod_row(i, tm=TM):
    return jnp.where(i < NP_TOK // tm, 0, 1 + (i - NP_TOK // tm) // (DEC_SEQ // tm))


def _mod_spec(kind, tm=TM):
    return pl.BlockSpec((1, 1, D), lambda i: (_mod_row(i, tm), 0, kind))


def _adaln_body(c_ref, w_ref, b_ref, o_ref):
    c = c_ref[...]
    s = (c * _sigmoid(c)).astype(BF16)
    o_ref[0] = _dot(s, w_ref[0].astype(BF16)) + b_ref[0]


def _adaln(cond8, w_ada, b_ada):
    depth = w_ada.shape[0]
    tn = 1536
    return pl.pallas_call(
        _adaln_body,
        grid=(depth, 6 * D // tn),
        in_specs=[pl.BlockSpec((8, D), lambda l, j: (0, 0)),
                  pl.BlockSpec((1, D, tn), lambda l, j: (l, 0, j)),
                  pl.BlockSpec((1, 1, tn), lambda l, j: (l, 0, j))],
        out_specs=pl.BlockSpec((1, 8, tn), lambda l, j: (l, 0, j)),
        out_shape=jax.ShapeDtypeStruct((depth, 8, 6 * D), F32),
        compiler_params=_cp(("arbitrary", "arbitrary"), 40),
        name="adaln",
    )(cond8, w_ada, b_ada.reshape(depth, 1, 6 * D))


N_GATES = 4 * M_HEADS
GATE_COL = 4 * M_HEADS * M_HD


def _pack_body(w_ref, big_ref, gates_ref):
    w = w_ref[0]
    big_ref[:, 0:GATE_COL] = w[:, 0:GATE_COL].astype(BF16)
    big_ref[:, GATE_COL:9 * D] = w[:, GATE_COL + N_GATES:9 * D + N_GATES].astype(BF16)
    lane = lax.broadcasted_iota(I32, (1, LANES), 1)
    gates_ref[...] = jnp.where(lane < N_GATES, w[:, GATE_COL:GATE_COL + LANES], 0.0).astype(BF16)


def _pack_w_in(w, layer):
    tr = 128
    return pl.pallas_call(
        _pack_body,
        grid=(D // tr,),
        in_specs=[pl.BlockSpec((1, tr, w.shape[2]), lambda i: (layer, i, 0))],
        out_specs=[pl.BlockSpec((tr, 9 * D), lambda i: (i, 0)), pl.BlockSpec((tr, LANES), lambda i: (i, 0))],
        out_shape=[jax.ShapeDtypeStruct((D, 9 * D), BF16), jax.ShapeDtypeStruct((D, LANES), BF16)],
        compiler_params=_cp(("arbitrary",), 48),
        name="pack_w_in",
    )(w)


def _inproj_body(x_ref, g_ref, sc_ref, sh_ref, w_ref, wg_ref, big_ref, gates_ref, kv_ref):
    i = pl.program_id(0)
    h = (_rms(x_ref[...]) * g_ref[...]) * (1.0 + sc_ref[0]) + sh_ref[0]
    hb = h.astype(BF16)
    gates_ref[...] = _dot(hb, wg_ref[...])
    for c in range(9):
        r = _dot(hb, w_ref[:, c * D:(c + 1) * D])
        big_ref[:, c * D:(c + 1) * D] = r.astype(BF16)
        if c in (5, 6):
            @pl.when(i < NP_TOK // TM_IN)
            def _():
                kv_ref[:, (c - 5) * D:(c - 4) * D] = r

    @pl.when(i == NP_TOK // TM_IN)
    def _():
        kv_ref[...] = jnp.zeros_like(kv_ref)


TM_IN = 512


def _inproj(x, norm_g, mod, w_big, w_gates):
    np_tiles = NP_TOK // TM_IN
    return pl.pallas_call(
        _inproj_body,
        grid=(N_TOK // TM_IN,),
        in_specs=[pl.BlockSpec((TM_IN, D), lambda i: (i, 0)),
                  pl.BlockSpec((1, D), lambda i: (0, 0)),
                  _mod_spec(1, TM_IN), _mod_spec(0, TM_IN),
                  pl.BlockSpec((D, 9 * D), lambda i: (0, 0), pipeline_mode=pl.Buffered(1)),
                  pl.BlockSpec((D, LANES), lambda i: (0, 0))],
        out_specs=[pl.BlockSpec((TM_IN, 9 * D), lambda i: (i, 0)),
                   pl.BlockSpec((TM_IN, LANES), lambda i: (i, 0)),
                   pl.BlockSpec((TM_IN, 2 * D), lambda i: (jnp.minimum(i, np_tiles), 0))],
        out_shape=[jax.ShapeDtypeStruct((N_TOK, 9 * D), BF16),
                   jax.ShapeDtypeStruct((N_TOK, LANES), F32),
                   jax.ShapeDtypeStruct((NP_TOK + TM_IN, 2 * D), F32)],
        compiler_params=_cp(("arbitrary",), 60),
        name="inproj",
    )(x, norm_g, mod, mod, w_big, w_gates)


def _conv_act(x, w, cos, sin, scale):
    T = x.shape[0]
    row = lax.broadcasted_iota(I32, (T, 1), 0)
    xp = jnp.where(row == 0, 0.0, pltpu.roll(x, 1, 0))
    xn = jnp.where(row == T - 1, 0.0, pltpu.roll(x, T - 1, 0))
    y = xp * w[0:1] + x * w[1:2] + xn * w[2:3]
    y = y * _sigmoid(y)
    halves = []
    for hlf in range(2):
        sl = slice(hlf * LANES, (hlf + 1) * LANES)
        yh = y[:, sl]
        if cos is not None:
            yh = yh * cos[:, sl] + pltpu.roll(yh, LANES // 2, 1) * sin[:, sl]
        halves.append((yh * scale).astype(BF16))
    return jnp.concatenate(halves, axis=1)


def _rope_tables():
    pos = np.arange(DEC_SEQ)
    rows = (pos // GRID_W).astype(np.float32)
    cols = (pos % GRID_W).astype(np.float32)
    nfreq = M_HD // 4
    inv = (ROPE_BASE ** (-np.arange(nfreq, dtype=np.float32) / nfreq)).astype(np.float32)
    d = np.arange(M_HD)
    p = np.where(d[None, :] < M_HD // 2, rows[:, None], cols[:, None]).astype(np.float32)
    ang = (p * inv[d % nfreq][None, :]).astype(np.float32)
    sign = np.where((d % (M_HD // 2)) < nfreq, -1.0, 1.0).astype(np.float32)
    return jnp.asarray(np.cos(ang), F32), jnp.asarray(np.sin(ang) * sign[None, :], F32)


MLSTM_HB = 2


def _mlstm_body(*refs, T, rope, has_state, emit_state):
    qraw_ref, kraw_ref, v_ref, om_ref, g_ref, gb_ref, hn_ref, cwq_ref, cwk_ref = refs[:9]
    pos = 9
    if rope:
        cos_ref, sin_ref = refs[pos:pos + 2]
        pos += 2
    if has_state:
        c0_ref, n0_ref, m0_ref = refs[pos:pos + 3]
        pos += 3
    oa_ref = refs[pos]
    pos += 1
    if emit_state:
        co_ref, no_ref, mo_ref = refs[pos:pos + 3]
        pos += 3
    nch = 2 * MLSTM_HB
    hs_refs = refs[pos:pos + nch]
    c_refs = refs[pos + nch:pos + 2 * nch]
    n_refs = refs[pos + 2 * nch:pos + 3 * nch]
    q_refs = refs[pos + 3 * nch:pos + 3 * nch + MLSTM_HB]
    k_refs = refs[pos + 3 * nch + MLSTM_HB:pos + 3 * nch + 2 * MLSTM_HB]

    cos = cos_ref[...] if rope else None
    sin = sin_ref[...] if rope else None
    for hh in range(MLSTM_HB):
        cols = slice(hh * M_HD, (hh + 1) * M_HD)
        q_refs[hh][...] = _conv_act(qraw_ref[:, cols].astype(F32), cwq_ref[:, cols], cos, sin, 1.0)
        k_refs[hh][...] = _conv_act(kraw_ref[:, cols].astype(F32), cwk_ref[:, cols], cos, sin, M_HD ** -0.5)

    head0 = pl.program_id(1) * MLSTM_HB
    nc = T // CHUNK
    lane = lax.broadcasted_iota(I32, (1, LANES), 1)
    r_i = lax.broadcasted_iota(I32, (CHUNK, CHUNK), 0)
    c_i = lax.broadcasted_iota(I32, (CHUNK, CHUNK), 1)
    eye = r_i == c_i
    causal = (r_i >= c_i, r_i <= c_i)
    cum = tuple(jnp.where(m, 1.0, 0.0).astype(BF16) for m in causal)

    def to_row(col):
        return jnp.sum(jnp.where(eye, col, 0.0), axis=0, keepdims=True)

    def pick(mat, colidx):
        return jnp.sum(jnp.where(lane == colidx, mat, 0.0), axis=1, keepdims=True)

    def dir_step(d, c, m_prevs):
        rows = pl.ds(pl.multiple_of(c * CHUNK, CHUNK), CHUNK)
        g = g_ref[rows, :] + gb_ref[...]
        lf = jnp.minimum(g, 0.0) - jnp.log(1.0 + jnp.exp(-jnp.abs(g)))
        bmat = _dot_x3(cum[d], lf)
        m_news = []
        for hh in range(MLSTM_HB):
            ch = hh * 2 + d
            hs_ref, c_ref, n_ref = hs_refs[ch], c_refs[ch], n_refs[ch]
            cols = slice(hh * M_HD, (hh + 1) * M_HD)
            m_prev = m_prevs[hh]
            ig_col = pick(g, d * M_HEADS + head0 + hh)
            b_col = pick(bmat, 2 * M_HEADS + d * M_HEADS + head0 + hh)
            ig_row = to_row(ig_col)
            b_row = to_row(b_col)
            b_last = b_row[:, CHUNK - 1:CHUNK] if d == 0 else b_row[:, 0:1]
            logd = jnp.where(causal[d], b_col - b_row + ig_row, -jnp.inf)
            inter = b_col + m_prev
            m_row = jnp.maximum(inter, jnp.max(logd, axis=1, keepdims=True))
            dmat = jnp.exp(logd - m_row)
            s_inter = jnp.exp(inter - m_row)
            q = q_refs[hh][rows, :]
            k = k_refs[hh][rows, :]
            v = v_ref[rows, cols]
            s = _dot_nt(q, k) * dmat
            num = _dot(s.astype(BF16), v) + s_inter * _dot(q, c_ref[...].astype(BF16))
            den = (jnp.sum(s, axis=1, keepdims=True)
                   + s_inter * jnp.sum(q.astype(F32) * n_ref[...], axis=1, keepdims=True))
            hs_ref[rows, :] = num / jnp.maximum(jnp.abs(den), jnp.exp(-m_row))
            log_w = b_last - b_col + ig_col
            m_new = jnp.maximum(b_last + m_prev, jnp.max(log_w, axis=0, keepdims=True))
            w = jnp.exp(log_w - m_new)
            decay = jnp.exp(b_last + m_prev - m_new)
            kw = k.astype(F32) * w
            c_ref[...] = decay * c_ref[...] + _dot_tn(kw.astype(BF16), v)
            n_ref[...] = decay * n_ref[...] + jnp.sum(kw, axis=0, keepdims=True)
            m_news.append(m_new)
        return tuple(m_news)

    m_init = ([], [])
    for hh in range(MLSTM_HB):
        for d in range(2):
            ch = hh * 2 + d
            if has_state:
                c_refs[ch][...] = c0_ref[0, d, hh]
                n_refs[ch][...] = n0_ref[0, d, hh]
                m_init[d].append(m0_ref[0, d, hh])
            else:
                c_refs[ch][...] = jnp.zeros_like(c_refs[ch])
                n_refs[ch][...] = jnp.zeros_like(n_refs[ch])
                m_init[d].append(jnp.zeros((1, 1), F32))

    def body(ci, ms):
        return dir_step(0, ci, ms[0]), dir_step(1, nc - 1 - ci, ms[1])

    m_fin = lax.fori_loop(0, nc, body, (tuple(m_init[0]), tuple(m_init[1])))
    for hh in range(MLSTM_HB):
        cols = slice(hh * M_HD, (hh + 1) * M_HD)
        if emit_state:
            for d in range(2):
                co_ref[0, d, hh] = c_refs[hh * 2 + d][...]
                no_ref[0, d, hh] = n_refs[hh * 2 + d][...]
                mo_ref[0, d, hh] = m_fin[d][hh]
        hm = _rms(hs_refs[hh * 2][...] + hs_refs[hh * 2 + 1][...]) * hn_ref[:, cols]
        oa_ref[:, cols] = (_sigmoid(om_ref[:, cols].astype(F32)) * hm).astype(BF16)


def _mlstm(big, gates, gate_bias, head_norm, conv_w, rope_tabs, state, *, nb, T, row0, emit_state):
    has_state = state is not None
    rope = rope_tabs is not None
    hb = MLSTM_HB
    w = hb * M_HD
    npair = M_HEADS // hb
    in_specs = [pl.BlockSpec((T, w), lambda b, h: (row0 + b, h)),
                pl.BlockSpec((T, w), lambda b, h: (row0 + b, npair + h)),
                pl.BlockSpec((T, w), lambda b, h: (row0 + b, 2 * npair + h)),
                pl.BlockSpec((T, w), lambda b, h: (row0 + b, 3 * npair + h)),
                pl.BlockSpec((T, LANES), lambda b, h: (row0 + b, 0)),
                pl.BlockSpec((1, LANES), lambda b, h: (0, 0)),
                pl.BlockSpec((1, w), lambda b, h: (0, h)),
                pl.BlockSpec((3, w), lambda b, h: (0, h)),
                pl.BlockSpec((3, w), lambda b, h: (0, npair + h))]
    args = [big, big, big, big, gates, gate_bias, head_norm, conv_w, conv_w]
    if rope:
        in_specs += [pl.BlockSpec((T, M_HD), lambda b, h: (0, 0))] * 2
        args += list(rope_tabs)
    state_specs = [pl.BlockSpec((1, 2, hb, M_HD, M_HD), lambda b, h: (b, 0, h, 0, 0)),
                   pl.BlockSpec((1, 2, hb, 1, M_HD), lambda b, h: (b, 0, h, 0, 0)),
                   pl.BlockSpec((1, 2, hb, 1, 1), lambda b, h: (b, 0, h, 0, 0))]
    if has_state:
        in_specs += state_specs
        args += list(state)
    out_specs = [pl.BlockSpec((T, w), lambda b, h: (b, h))]
    out_shape = [jax.ShapeDtypeStruct((nb * T, M_HEADS * M_HD), BF16)]
    if emit_state:
        out_specs += state_specs
        out_shape += [jax.ShapeDtypeStruct((nb, 2, M_HEADS, M_HD, M_HD), F32),
                      jax.ShapeDtypeStruct((nb, 2, M_HEADS, 1, M_HD), F32),
                      jax.ShapeDtypeStruct((nb, 2, M_HEADS, 1, 1), F32)]
    nch = 2 * hb
    return pl.pallas_call(
        functools.partial(_mlstm_body, T=T, rope=rope, has_state=has_state, emit_state=emit_state),
        grid=(nb, npair),
        in_specs=in_specs,
        out_specs=out_specs,
        out_shape=out_shape,
        scratch_shapes=[pltpu.VMEM((T, M_HD), F32)] * nch + [pltpu.VMEM((M_HD, M_HD), F32)] * nch
        + [pltpu.VMEM((1, M_HD), F32)] * nch + [pltpu.VMEM((T, M_HD), BF16)] * (2 * hb),
        compiler_params=_cp(("arbitrary", "arbitrary"), 56),
        name="mlstm",
    )(*args)


def _pair_masks():
    lane = lax.broadcasted_iota(I32, (1, LANES), 1)
    first = lane < N_HD
    return first, jnp.logical_not(first)


DA_PAIRS = 4


def _attn_body(q_ref, k_ref, v_ref, o_ref):
    first, second = _pair_masks()
    for p in range(DA_PAIRS):
        sl = slice(p * LANES, (p + 1) * LANES)
        q = q_ref[:, sl]
        q2 = jnp.concatenate([jnp.where(first, q, jnp.zeros_like(q)), jnp.where(second, q, jnp.zeros_like(q))], axis=0)
        s = _dot_nt(q2, k_ref[:, sl]) * (N_HD ** -0.5)
        e = jnp.exp(s - jnp.max(s, axis=-1, keepdims=True))
        o2 = _dot(e.astype(BF16), v_ref[:, sl]) / jnp.sum(e, axis=-1, keepdims=True)
        o_ref[:, sl] = jnp.where(first, o2[0:SEQ], o2[SEQ:2 * SEQ]).astype(BF16)


def _dense_attention(big):
    w = DA_PAIRS * LANES
    cb = D // w
    return pl.pallas_call(
        _attn_body,
        grid=(BATCH, cb),
        in_specs=[pl.BlockSpec((SEQ, w), lambda b, p: (b, 4 * cb + p)),
                  pl.BlockSpec((SEQ, w), lambda b, p: (b, 5 * cb + p)),
                  pl.BlockSpec((SEQ, w), lambda b, p: (b, 6 * cb + p))],
        out_specs=pl.BlockSpec((SEQ, w), lambda b, p: (b, p)),
        out_shape=jax.ShapeDtypeStruct((NP_TOK, D), BF16),
        compiler_params=_cp(("arbitrary", "arbitrary"), 32),
        name="dense_attn",
    )(big, big, big)


NA_DR2 = 2 * MAX_WIN_H - 2
NA_LOCAL = MAX_WIN_H * GRID_W


def _natten_body(q_ref, k_ref, v_ref, kc_ref, vc_ref, tab_ref, mask_ref, o_ref):
    rb = pl.program_id(2)
    kc = kc_ref[0, 0].astype(BF16)
    vc = vc_ref[0, 0].astype(BF16)
    colmask = mask_ref[...]
    first, second = _pair_masks()
    for qr in range(NA_QROWS):
        r = NA_QROWS * rb + qr
        rstart = jnp.clip(r - MAX_WIN_H // 2, 0, GRID_H - MAX_WIN_H)
        ks = pl.multiple_of(rstart * GRID_W, GRID_W)
        kl = k_ref[pl.ds(ks, NA_LOCAL), :]
        vl = v_ref[pl.ds(ks, NA_LOCAL), :]
        dr0 = rstart - r + MAX_WIN_H - 1
        qb = q_ref[qr * GRID_W:(qr + 1) * GRID_W, :]
        q2 = jnp.concatenate([jnp.where(first, qb, jnp.zeros_like(qb)),
                              jnp.where(second, qb, jnp.zeros_like(qb))], axis=0)
        bias = jnp.concatenate(
            [jnp.concatenate([tab_ref[par, dr0 + 2 * kp] for kp in range(MAX_WIN_H // 2)], axis=1) + colmask
             for par in range(2)], axis=0)
        sl = _dot_nt(q2, kl) * (N_HD ** -0.5) + bias
        sc = _dot_nt(q2, kc) * (N_HD ** -0.5)
        mx = jnp.maximum(jnp.max(sl, axis=-1, keepdims=True), jnp.max(sc, axis=-1, keepdims=True))
        el = jnp.exp(sl - mx)
        ec = jnp.exp(sc - mx)
        den = jnp.sum(el, axis=-1, keepdims=True) + jnp.sum(ec, axis=-1, keepdims=True)
        o2 = (_dot(el.astype(BF16), vl) + _dot(ec.astype(BF16), vc)) / den
        o_ref[qr * GRID_W:(qr + 1) * GRID_W, :] = jnp.where(first, o2[0:GRID_W], o2[GRID_W:2 * GRID_W]).astype(BF16)


def _natten(big, cache_k, cache_v, bias_tab, window_mask, layer):
    cb = D // LANES
    nrb = GRID_H // NA_QROWS
    qrows = NA_QROWS * GRID_W
    q0 = NP_TOK // qrows
    b0 = NP_TOK // DEC_SEQ
    return pl.pallas_call(
        _natten_body,
        grid=(DEC_BATCH, N_HEADS // 2, nrb),
        in_specs=[pl.BlockSpec((qrows, LANES), lambda b, p, r: (q0 + b * nrb + r, 4 * cb + p)),
                  pl.BlockSpec((DEC_SEQ, LANES), lambda b, p, r: (b0 + b, 5 * cb + p)),
                  pl.BlockSpec((DEC_SEQ, LANES), lambda b, p, r: (b0 + b, 6 * cb + p)),
                  pl.BlockSpec((1, 1, PAST_LEN, LANES), lambda b, p, r: (b, layer, 0, p)),
                  pl.BlockSpec((1, 1, PAST_LEN, LANES), lambda b, p, r: (b, layer, 0, p)),
                  pl.BlockSpec((2, NA_DR2, GRID_W, 2 * GRID_W), lambda b, p, r: (p, 0, 0, 0)),
                  pl.BlockSpec((GRID_W, NA_LOCAL), lambda b, p, r: (0, 0))],
        out_specs=pl.BlockSpec((qrows, LANES), lambda b, p, r: (b * nrb + r, p)),
        out_shape=jax.ShapeDtypeStruct((NS_TOK, D), BF16),
        compiler_params=_cp(("arbitrary", "arbitrary", "arbitrary"), 40),
        name="natten",
    )(big, big, big, cache_k, cache_v, bias_tab, window_mask)


def _na_bias_table(rpb):
    n = GRID_W
    n_dr = 2 * MAX_WIN_H - 1
    padl = n - WIN_W
    row = jnp.pad(rpb.astype(F32), ((0, 0), (0, 0), (padl, 2 * n - 1 - padl - (2 * WIN_W - 1))), constant_values=NEG)
    flat = jnp.broadcast_to(row[:, :, None, :], (N_HEADS, n_dr, n, 2 * n - 1)).reshape(N_HEADS, n_dr, n * (2 * n - 1))
    toep = flat[:, :, n - 1:n - 1 + n * (2 * n - 2)].reshape(N_HEADS, n_dr, n, 2 * n - 2)[..., :n]
    return jnp.concatenate([toep[:, :-1], toep[:, 1:]], axis=-1)


def _na_window_mask():
    qcol = np.arange(GRID_W)[:, None]
    kcol = (np.arange(NA_LOCAL) % GRID_W)[None, :]
    cstart = np.clip(qcol - WIN_W // 2, 0, GRID_W - WIN_W)
    valid = (kcol >= cstart) & (kcol < cstart + WIN_W)
    return jnp.asarray(np.where(valid, 0.0, NEG).astype(np.float32))


def _postmix_body(oap_ref, oas_ref, obp_ref, obs_ref, ga_ref, gb_ref, x_ref, g1_ref, nf_ref, sc2_ref, sh2_ref,
                  wa_ref, wb_ref, wo_ref, wr_ref, xo_ref, h2_ref, aff_ref):
    prompt = pl.program_id(0) < NP_TILES
    a = _dot(jnp.where(prompt, oap_ref[...], oas_ref[...]), wa_ref[...])
    b = _dot(jnp.where(prompt, obp_ref[...], obs_ref[...]), wb_ref[...])
    merged = _sigmoid(ga_ref[...].astype(F32)) * a + _sigmoid(gb_ref[...].astype(F32)) * b
    xn = x_ref[...] + g1_ref[0] * _dot(merged.astype(BF16), wo_ref[...])
    xo_ref[...] = xn
    h2 = (_rms(xn) * nf_ref[...]) * (1.0 + sc2_ref[0]) + sh2_ref[0]
    h2_ref[...] = h2
    h1, h2m, _ = _split3(h2)
    w1, w2, _ = _split3(wr_ref[...])
    lg = _dot_nt(w1, h1) + _dot_nt(w1, h2m) + _dot_nt(w2, h1)
    e = jnp.exp(lg - jnp.max(lg, axis=0, keepdims=True))
    aff_ref[...] = e / jnp.sum(e, axis=0, keepdims=True)


def _postmix(oa_p, oa_s, ob_p, ob_s, big, x, mod, norm_ffn, wa, wb, wo, wr_t):
    row = lambda i: (i, 0)
    const = lambda i: (0, 0)
    prow = lambda i: (jnp.minimum(i, NP_TILES - 1), 0)
    srow = lambda i: (jnp.maximum(i - NP_TILES, 0), 0)
    return pl.pallas_call(
        _postmix_body,
        grid=(N_TILES,),
        in_specs=[pl.BlockSpec((TM, D), prow), pl.BlockSpec((TM, D), srow),
                  pl.BlockSpec((TM, D), prow), pl.BlockSpec((TM, D), srow),
                  pl.BlockSpec((TM, D), lambda i: (i, 7)), pl.BlockSpec((TM, D), lambda i: (i, 8)),
                  pl.BlockSpec((TM, D), row),
                  _mod_spec(2), pl.BlockSpec((1, D), const), _mod_spec(4), _mod_spec(3),
                  pl.BlockSpec((D, D), const), pl.BlockSpec((D, D), const), pl.BlockSpec((D, D), const),
                  pl.BlockSpec((N_EXPERTS, D), const)],
        out_specs=[pl.BlockSpec((TM, D), row), pl.BlockSpec((TM, D), row),
                   pl.BlockSpec((N_EXPERTS, TM), lambda i: (0, i))],
        out_shape=[jax.ShapeDtypeStruct((N_TOK, D), F32), jax.ShapeDtypeStruct((N_TOK, D), F32),
                   jax.ShapeDtypeStruct((N_EXPERTS, N_TOK), F32)],
        compiler_params=_cp(("arbitrary",), 48),
        name="postmix",
    )(oa_p, oa_s, ob_p, ob_s, big, big, x, mod, norm_ffn, mod, mod, wa, wb, wo, wr_t)


def _route_body(a_ref, idx_ref, gv_ref, off_ref, pos_ref, *, R, cap):
    a = a_ref[...]
    capf = float(cap)

    def count_ge(thr):
        c = jnp.sum(jnp.where(a >= thr, 1.0, 0.0), axis=1, keepdims=True)
        return jnp.sum(c, axis=2, keepdims=True)

    def search(i, lo_bits):
        cand = lo_bits | jnp.left_shift(jnp.int32(1), 30 - i)
        return jnp.where(count_ge(pltpu.bitcast(cand, F32)) >= capf, cand, lo_bits)

    tau_bits = lax.fori_loop(0, 31, search, jnp.zeros((N_EXPERTS, 1, 1), I32))
    lo = pltpu.bitcast(tau_bits, F32)
    hi = pltpu.bitcast(jnp.maximum(tau_bits + 1, jnp.int32(MIN_NORMAL_BITS)), F32)

    def refine(i, lh):
        lo, hi = lh
        mid = lo + (hi - lo) * 0.5
        ok = count_ge(mid) >= capf
        return jnp.where(ok, mid, lo), jnp.where(ok, hi, mid)

    lo, hi = lax.fori_loop(0, 32, refine, (lo, hi))
    above = jnp.where(a >= hi, 1.0, 0.0)
    ties = jnp.where(a >= lo, 1.0, 0.0) - above
    need = capf - jnp.sum(jnp.sum(above, axis=1, keepdims=True), axis=2, keepdims=True)

    l0 = lax.broadcasted_iota(I32, (LANES, LANES), 0)
    l1 = lax.broadcasted_iota(I32, (LANES, LANES), 1)
    upper = jnp.where(l0 <= l1, 1.0, 0.0).astype(BF16)
    r0 = lax.broadcasted_iota(I32, (R, R), 0)
    r1 = lax.broadcasted_iota(I32, (R, R), 1)
    below = jnp.where(r1 < r0, 1.0, 0.0).astype(BF16)
    eye_r = r0 == r1
    r_row = lax.broadcasted_iota(I32, (1, R), 1).astype(F32)
    lane_row = lax.broadcasted_iota(I32, (1, LANES), 1).astype(F32)
    jcol = lax.broadcasted_iota(I32, (cap, 1), 0).astype(F32)

    def prefix(x):
        within = _dot(x.astype(BF16), upper)
        tot = jnp.broadcast_to(within[:, LANES - 1:LANES], (R, LANES))
        offs = _dot(below, tot.astype(BF16))
        return within + offs, offs

    for e in range(N_EXPERTS):
        eq = ties[e]
        cin_eq, _ = prefix(eq)
        sel = above[e] + eq * jnp.where(cin_eq - eq < need[e], 1.0, 0.0)
        cin, offs = prefix(sel)
        cend_row = jnp.sum(jnp.where(eye_r, cin[:, LANES - 1:LANES], 0.0), axis=0, keepdims=True)
        bcol = jnp.sum(jnp.where(cend_row <= jcol, 1.0, 0.0), axis=1, keepdims=True)
        onehot = jnp.where(bcol == r_row, 1.0, 0.0).astype(BF16)
        lcol = jnp.sum(jnp.where(_dot_x3(onehot, cin) <= jcol, 1.0, 0.0), axis=1, keepdims=True)
        idx_ref[e] = (bcol * LANES + lcol).astype(I32)
        gv_ref[e] = jnp.sum(jnp.where(lane_row == lcol, _dot_x3(onehot, a[e]), 0.0), axis=1, keepdims=True)
        off_ref[e] = offs[:, 0:1]
        pos_ref[e] = jnp.where(sel > 0.0, cin - 1.0, -1.0)


def _route(aff3, cap):
    R = aff3.shape[1]
    full = lambda s: pl.BlockSpec(s, lambda i: (0, 0, 0))
    return pl.pallas_call(
        functools.partial(_route_body, R=R, cap=cap),
        grid=(1,),
        in_specs=[full((N_EXPERTS, R, LANES))],
        out_specs=[full((N_EXPERTS, cap, 1)), full((N_EXPERTS, cap, 1)), full((N_EXPERTS, R, 1)),
                   full((N_EXPERTS, R, LANES))],
        out_shape=[jax.ShapeDtypeStruct((N_EXPERTS, cap, 1), I32),
                   jax.ShapeDtypeStruct((N_EXPERTS, cap, 1), F32),
                   jax.ShapeDtypeStruct((N_EXPERTS, R, 1), F32),
                   jax.ShapeDtypeStruct((N_EXPERTS, R, LANES), F32)],
        compiler_params=_cp(("arbitrary",), 48),
        name="route",
    )(aff3)


GATHER_UNROLL = 8


def _gather_copy(h_hbm, xs_ref, sem, src_row, dst_row, nrows):
    return pltpu.make_async_copy(h_hbm.at[pl.ds(src_row, nrows), :], xs_ref.at[pl.ds(dst_row, nrows), :], sem)


def _slab_copy(y_ref, ys_hbm, sem, e, s):
    return pltpu.make_async_copy(y_ref.at[:, s * LANES:(s + 1) * LANES], ys_hbm.at[e, :, s, :], sem)


def _ffn_body(idx_ref, h_hbm, wg_ref, wu_ref, wd_ref, gv_ref, ys_hbm, xs_ref, xb_ref, y_ref, gsem, osem):
    e = pl.program_id(0)
    f = pl.program_id(1)
    last_e = pl.num_programs(0) - 1

    @pl.when(f == 0)
    def _():
        def issue(j, carry):
            _gather_copy(h_hbm, xs_ref, gsem, idx_ref[e * CAP_TOT + j], j, 1).start()
            return carry
        lax.fori_loop(0, CAP_TOT, issue, 0, unroll=GATHER_UNROLL)

        @pl.when(e > 0)
        def _():
            for s in range(SLAB):
                _slab_copy(y_ref, ys_hbm, osem, e - 1, s).wait()

        _gather_copy(h_hbm, xs_ref, gsem, 0, 0, CAP_TOT).wait()
        xb_ref[...] = xs_ref[...].astype(BF16)

    xb = xb_ref[...]
    g = _dot(xb, wg_ref[0, 0].astype(BF16))
    u = _dot(xb, wu_ref[0, 0].astype(BF16))
    hid = (g * _sigmoid(g) * u).astype(BF16)
    y = _dot(hid, wd_ref[0, 0].astype(BF16))

    @pl.when(f == 0)
    def _():
        y_ref[...] = y

    @pl.when(f > 0)
    def _():
        y_ref[...] = y_ref[...] + y

    @pl.when(f == pl.num_programs(1) - 1)
    def _():
        y_ref[...] = y_ref[...] * gv_ref[0]
        for s in range(SLAB):
            _slab_copy(y_ref, ys_hbm, osem, e, s).start()

        @pl.when(e == last_e)
        def _():
            for s in range(SLAB):
                _slab_copy(y_ref, ys_hbm, osem, e, s).wait()


def _ffn(idx_flat, h2, wg, wu, wd, gv, layer):
    nf = EXPERT_FF // FF_CHUNK
    return pl.pallas_call(
        _ffn_body,
        grid_spec=pltpu.PrefetchScalarGridSpec(
            num_scalar_prefetch=1,
            grid=(N_EXPERTS, nf),
            in_specs=[pl.BlockSpec(memory_space=pl.ANY),
                      pl.BlockSpec((1, 1, D, FF_CHUNK), lambda e, f, idx: (layer, e, 0, f)),
                      pl.BlockSpec((1, 1, D, FF_CHUNK), lambda e, f, idx: (layer, e, 0, f)),
                      pl.BlockSpec((1, 1, FF_CHUNK, D), lambda e, f, idx: (layer, e, f, 0)),
                      pl.BlockSpec((1, CAP_TOT, 1), lambda e, f, idx: (e, 0, 0))],
            out_specs=pl.BlockSpec(memory_space=pl.ANY),
            scratch_shapes=[pltpu.VMEM((CAP_TOT, D), F32), pltpu.VMEM((CAP_TOT, D), BF16),
                            pltpu.VMEM((CAP_TOT, D), F32),
                            pltpu.SemaphoreType.DMA(()), pltpu.SemaphoreType.DMA(())]),
        out_shape=jax.ShapeDtypeStruct((N_EXPERTS, CAP_TOT, SLAB, LANES), F32),
        compiler_params=_cp(("arbitrary", "arbitrary"), 58),
        name="expert_ffn",
    )(idx_flat, h2, wg, wu, wd, gv)


MAX_PAIRS = N_EXPERTS * TM
SEG_BITS = TM.bit_length()


def _seg_copy(ys_hbm, w_ref, sem, e, src_row, dst_row, nrows):
    src = pl.multiple_of(src_row * SLAB, SLAB)
    dst = pl.multiple_of(dst_row * SLAB, SLAB)
    return pltpu.make_async_copy(ys_hbm.at[e, pl.ds(src, nrows * SLAB), :], w_ref.at[pl.ds(dst, nrows * SLAB), :], sem)


def _combine_body(lo_ref, hi_ref, ys_hbm, pos_ref, x_ref, g2_ref, nf_ref, o_ref, w_ref, acc_ref, oh_ref, sem, *, last):
    b = pl.program_id(0)
    lane = lax.broadcasted_iota(I32, (1, LANES), 1)
    off = jnp.int32(0)
    shift = jnp.zeros((1, LANES), F32)
    seg = []
    for e in range(N_EXPERTS):
        lo = lo_ref[e * N_TILES + b]
        n = hi_ref[e * N_TILES + b] - lo
        seg.append((off, off + n))
        for bit in range(SEG_BITS - 1, -1, -1):
            done = (n >> (bit + 1)) << (bit + 1)

            @pl.when((n & (1 << bit)) != 0)
            def _(e=e, lo=lo, off=off, done=done, bit=bit):
                _seg_copy(ys_hbm, w_ref, sem, e, lo + done, off + done, 1 << bit).start()
        shift = jnp.where(lane == e, (off - lo).astype(F32), shift)
        off = off + n
    npairs = off
    for bit in range(MAX_PAIRS.bit_length() - 1, -1, -1):
        @pl.when((npairs & (1 << bit)) != 0)
        def _(bit=bit):
            _seg_copy(ys_hbm, w_ref, sem, 0, 0, 0, 1 << bit).wait()

    acc_ref[...] = jnp.zeros_like(acc_ref)
    pos = pos_ref[...]
    prow = jnp.where(pos >= 0.0, pos + shift[:, 0:N_EXPERTS], -1.0)
    sub = lax.broadcasted_iota(I32, (TM, 1), 0)
    pair = lax.broadcasted_iota(I32, (1, TM), 1).astype(F32)

    def body(c, carry):
        valid = (c * TM + sub) < npairs
        local = prow - (c * TM).astype(F32)
        oh_ref[...] = jnp.zeros_like(oh_ref)
        for e in range(N_EXPERTS):
            @pl.when((seg[e][0] < (c + 1) * TM) & (seg[e][1] > c * TM))
            def _(e=e):
                oh_ref[...] += jnp.where(local[:, e:e + 1] == pair, 1.0, 0.0)
        onehot = oh_ref[...].astype(BF16)
        for s in range(SLAB):
            data = jnp.where(valid, w_ref[pl.ds(c * TM * SLAB + s, TM, stride=SLAB), :], 0.0)
            hi, mid, _ = _split3(data)
            sl = slice(s * LANES, (s + 1) * LANES)
            acc_ref[:, sl] = acc_ref[:, sl] + (_dot(onehot, hi) + _dot(onehot, mid))
        return carry

    lax.fori_loop(0, (npairs + TM - 1) // TM, body, 0)
    xn = x_ref[...] + g2_ref[0] * acc_ref[...]
    if last:
        xn = _rms(xn) * nf_ref[...]
    o_ref[...] = xn


def _combine(seg_lo, seg_hi, ys, pos_t, x, mod, norm_final, *, last):
    return pl.pallas_call(
        functools.partial(_combine_body, last=last),
        grid_spec=pltpu.PrefetchScalarGridSpec(
            num_scalar_prefetch=2,
            grid=(N_TILES,),
            in_specs=[pl.BlockSpec(memory_space=pl.ANY),
                      pl.BlockSpec((TM, N_EXPERTS), lambda i, lo, hi: (i, 0)),
                      pl.BlockSpec((TM, D), lambda i, lo, hi: (i, 0)),
                      pl.BlockSpec((1, 1, D), lambda i, lo, hi: (_mod_row(i), 0, 5)),
                      pl.BlockSpec((1, D), lambda i, lo, hi: (0, 0))],
            out_specs=pl.BlockSpec((TM, D), lambda i, lo, hi: (i, 0)),
            scratch_shapes=[pltpu.VMEM((MAX_PAIRS * SLAB, LANES), F32), pltpu.VMEM((TM, D), F32),
                            pltpu.VMEM((TM, TM), F32), pltpu.SemaphoreType.DMA(())]),
        out_shape=jax.ShapeDtypeStruct((N_TOK, D), F32),
        compiler_params=_cp(("arbitrary",), 48),
        name="combine",
    )(seg_lo, seg_hi, ys, pos_t, x, mod, norm_final)


def _tile_segments(off_p, off_s):
    per = TM // LANES
    lo_p = off_p[:, ::per, 0].astype(I32)
    lo_s = off_s[:, ::per, 0].astype(I32) + CAP_P
    hi_p = jnp.concatenate([lo_p[:, 1:], jnp.full((N_EXPERTS, 1), CAP_P, I32)], axis=1)
    hi_s = jnp.concatenate([lo_s[:, 1:], jnp.full((N_EXPERTS, 1), CAP_TOT, I32)], axis=1)
    lo = jnp.concatenate([lo_p, lo_s], axis=1)
    hi = jnp.concatenate([hi_p, hi_s], axis=1)
    return lo.reshape(-1), hi.reshape(-1)


def kernel(x_prompt, x_sample, cache_na_k, cache_na_v, state_mlstm_C, state_mlstm_n, state_mlstm_m, c, c_ctx, w_ada, b_ada, norm_mix, norm_ffn, w_in, conv_qk, mlstm_gate_bias, mlstm_head_norm, na_rpb, w_branch_a, w_branch_b, w_out, w_router, w_expert_gate, w_expert_up, w_expert_down, norm_final):
    depth = w_in.shape[0]
    m_width = M_HEADS * M_HD
    n_gates = 4 * M_HEADS

    x = jnp.concatenate([x_prompt.reshape(NP_TOK, D), x_sample.reshape(NS_TOK, D)], axis=0)
    cond8 = jnp.concatenate([c_ctx[None], c, jnp.zeros((8 - 1 - DEC_BATCH, D), F32)], axis=0)
    mods = _adaln(cond8, w_ada, b_ada).reshape(depth, 8, 1, 6 * D)
    cos_t, sin_t = _rope_tables()
    cache_k = cache_na_k.reshape(DEC_BATCH, depth, PAST_LEN, D)
    cache_v = cache_na_v.reshape(DEC_BATCH, depth, PAST_LEN, D)
    norm_final2 = norm_final.reshape(1, D)
    window_mask = _na_window_mask()

    ks_, vs_, cs_, ns_, ms_ = [], [], [], [], []
    for l in range(depth):
        mod = mods[l]
        w_big, w_gates = _pack_w_in(w_in, l)
        gate_bias = jnp.pad(mlstm_gate_bias[l], (0, LANES - n_gates)).reshape(1, LANES)
        head_norm = mlstm_head_norm[l].reshape(1, m_width)

        big, gates, kv = _inproj(x, norm_mix[l].reshape(1, D), mod, w_big, w_gates)

        oa_p, c_new, n_new, m_new = _mlstm(big, gates, gate_bias, head_norm, conv_qk[l], None, None,
                                           nb=BATCH, T=SEQ, row0=0, emit_state=True)
        state = (state_mlstm_C[:, l], state_mlstm_n[:, l].reshape(DEC_BATCH, 2, M_HEADS, 1, M_HD),
                 state_mlstm_m[:, l].reshape(DEC_BATCH, 2, M_HEADS, 1, 1))
        (oa_s,) = _mlstm(big, gates, gate_bias, head_norm, conv_qk[l], (cos_t, sin_t), state,
                         nb=DEC_BATCH, T=DEC_SEQ, row0=NP_TOK // DEC_SEQ, emit_state=False)

        ob_p = _dense_attention(big)
        ob_s = _natten(big, cache_k, cache_v, _na_bias_table(na_rpb[l]), window_mask, l)

        x, h2, aff_t = _postmix(oa_p, oa_s, ob_p, ob_s, big, x, mod, norm_ffn[l].reshape(1, D),
                                w_branch_a[l].astype(BF16), w_branch_b[l].astype(BF16), w_out[l].astype(BF16),
                                w_router[l].T)

        idx_p, gv_p, off_p, pos_p = _route(aff_t[:, :NP_TOK].reshape(N_EXPERTS, NP_TOK // LANES, LANES), CAP_P)
        idx_s, gv_s, off_s, pos_s = _route(aff_t[:, NP_TOK:].reshape(N_EXPERTS, NS_TOK // LANES, LANES), CAP_S)
        idx = jnp.concatenate([idx_p, idx_s + NP_TOK], axis=1)
        gv = jnp.concatenate([gv_p, gv_s], axis=1)
        pos_s = jnp.where(pos_s >= 0.0, pos_s + CAP_P, pos_s)
        pos_t = jnp.concatenate([pos_p.reshape(N_EXPERTS, NP_TOK), pos_s.reshape(N_EXPERTS, NS_TOK)], axis=1).T
        ys = _ffn(idx.reshape(-1), h2, w_expert_gate, w_expert_up, w_expert_down, gv, l)
        ys = ys.reshape(N_EXPERTS, CAP_TOT * SLAB, LANES)
        seg_lo, seg_hi = _tile_segments(off_p, off_s)
        x = _combine(seg_lo, seg_hi, ys, pos_t, x, mod, norm_final2, last=(l == depth - 1))

        ks_.append(kv[:NP_TOK, :D].reshape(BATCH, SEQ, N_HEADS, N_HD))
        vs_.append(kv[:NP_TOK, D:].reshape(BATCH, SEQ, N_HEADS, N_HD))
        cs_.append(c_new)
        ns_.append(n_new.reshape(BATCH, 2, M_HEADS, M_HD))
        ms_.append(m_new.reshape(BATCH, 2, M_HEADS))

    y_prompt = x[:NP_TOK].reshape(BATCH, SEQ, D)
    y_sample = x[NP_TOK:].reshape(DEC_BATCH, DEC_SEQ, D)
    return (y_prompt, y_sample, jnp.stack(ks_, axis=1), jnp.stack(vs_, axis=1),
            jnp.stack(cs_, axis=1), jnp.stack(ns_, axis=1), jnp.stack(ms_, axis=1))
```

```python
import functools

import numpy as np
import jax
import jax.numpy as jnp
from jax import lax
from jax.experimental import pallas as pl
from jax.experimental.pallas import tpu as pltpu

F32 = jnp.float32
BF16 = jnp.bfloat16
I32 = jnp.int32

D = 1024
BATCH, SEQ = 16, 256
DEC_BATCH, DEC_SEQ = 4, 2048
PAST_LEN = 512
NP_TOK = BATCH * SEQ
NS_TOK = DEC_BATCH * DEC_SEQ
N_TOK = NP_TOK + NS_TOK
GRID_W = 64
GRID_H = DEC_SEQ // GRID_W
M_HEADS, M_HD = 4, 256
CHUNK = 128
N_HEADS, N_HD = 16, 64
MAX_WIN_H, WIN_W = 8, 16
N_EXPERTS, EXPERT_FF = 16, 2048
CAP_P = 2 * NP_TOK // N_EXPERTS
CAP_S = 2 * NS_TOK // N_EXPERTS
CAP_TOT = CAP_P + CAP_S
ROPE_BASE = 10000.0
EPS = 1e-6
NEG = -1e30
MIN_NORMAL_BITS = 0x00800000

TM = 256
N_TILES = N_TOK // TM
NP_TILES = NP_TOK // TM
LANES = 128
NA_QROWS = 8
FF_CHUNK = 512
SLAB = D // LANES


def _cp(sem, vmem_mb):
    return pltpu.CompilerParams(dimension_semantics=sem, vmem_limit_bytes=vmem_mb * 2 ** 20)


def _dot(a, b):
    return jnp.dot(a, b, preferred_element_type=F32)


def _dot_nt(a, b):
    return lax.dot_general(a, b, (((1,), (1,)), ((), ())), preferred_element_type=F32)


def _dot_tn(a, b):
    return lax.dot_general(a, b, (((0,), (0,)), ((), ())), preferred_element_type=F32)


def _split3(x):
    hi = x.astype(BF16)
    r = x - hi.astype(F32)
    mid = r.astype(BF16)
    lo = (r - mid.astype(F32)).astype(BF16)
    return hi, mid, lo


def _dot_x3(a_bf, x):
    hi, mid, lo = _split3(x)
    return _dot(a_bf, hi) + _dot(a_bf, mid) + _dot(a_bf, lo)


def _sigmoid(x):
    return 1.0 / (1.0 + jnp.exp(-x))


def _rms(x):
    return x * lax.rsqrt(jnp.mean(x * x, axis=-1, keepdims=True) + EPS)


def _mod_row(i, tm=TM):
    return jnp.where(i < NP_TOK // tm, 0, 1 + (i - NP_TOK // tm) // (DEC_SEQ // tm))


def _mod_spec(kind, tm=TM):
    return pl.BlockSpec((1, 1, D), lambda i: (_mod_row(i, tm), 0, kind))


def _adaln_body(c_ref, w_ref, b_ref, o_ref):
    c = c_ref[...]
    s = (c * _sigmoid(c)).astype(BF16)
    o_ref[0] = _dot(s, w_ref[0].astype(BF16)) + b_ref[0]


def _adaln(cond8, w_ada, b_ada):
    depth = w_ada.shape[0]
    tn = 1536
    return pl.pallas_call(
        _adaln_body,
        grid=(depth, 6 * D // tn),
        in_specs=[pl.BlockSpec((8, D), lambda l, j: (0, 0)),
                  pl.BlockSpec((1, D, tn), lambda l, j: (l, 0, j)),
                  pl.BlockSpec((1, 1, tn), lambda l, j: (l, 0, j))],
        out_specs=pl.BlockSpec((1, 8, tn), lambda l, j: (l, 0, j)),
        out_shape=jax.ShapeDtypeStruct((depth, 8, 6 * D), F32),
        compiler_params=_cp(("arbitrary", "arbitrary"), 40),
        name="adaln",
    )(cond8, w_ada, b_ada.reshape(depth, 1, 6 * D))


N_GATES = 4 * M_HEADS
GATE_COL = 4 * M_HEADS * M_HD


def _pack_body(w_ref, big_ref, gates_ref):
    w = w_ref[0]
    big_ref[:, 0:GATE_COL] = w[:, 0:GATE_COL].astype(BF16)
    big_ref[:, GATE_COL:9 * D] = w[:, GATE_COL + N_GATES:9 * D + N_GATES].astype(BF16)
    lane = lax.broadcasted_iota(I32, (1, LANES), 1)
    gates_ref[...] = jnp.where(lane < N_GATES, w[:, GATE_COL:GATE_COL + LANES], 0.0).astype(BF16)


def _pack_w_in(w, layer):
    tr = 128
    return pl.pallas_call(
        _pack_body,
        grid=(D // tr,),
        in_specs=[pl.BlockSpec((1, tr, w.shape[2]), lambda i: (layer, i, 0))],
        out_specs=[pl.BlockSpec((tr, 9 * D), lambda i: (i, 0)), pl.BlockSpec((tr, LANES), lambda i: (i, 0))],
        out_shape=[jax.ShapeDtypeStruct((D, 9 * D), BF16), jax.ShapeDtypeStruct((D, LANES), BF16)],
        compiler_params=_cp(("arbitrary",), 48),
        name="pack_w_in",
    )(w)


def _inproj_body(x_ref, g_ref, sc_ref, sh_ref, w_ref, wg_ref, big_ref, gates_ref, kv_ref):
    i = pl.program_id(0)
    h = (_rms(x_ref[...]) * g_ref[...]) * (1.0 + sc_ref[0]) + sh_ref[0]
    hb = h.astype(BF16)
    gates_ref[...] = _dot(hb, wg_ref[...])
    for c in range(9):
        r = _dot(hb, w_ref[:, c * D:(c + 1) * D])
        big_ref[:, c * D:(c + 1) * D] = r.astype(BF16)
        if c in (5, 6):
            @pl.when(i < NP_TOK // TM_IN)
            def _():
                kv_ref[:, (c - 5) * D:(c - 4) * D] = r

    @pl.when(i == NP_TOK // TM_IN)
    def _():
        kv_ref[...] = jnp.zeros_like(kv_ref)


TM_IN = 512


def _inproj(x, norm_g, mod, w_big, w_gates):
    np_tiles = NP_TOK // TM_IN
    return pl.pallas_call(
        _inproj_body,
        grid=(N_TOK // TM_IN,),
        in_specs=[pl.BlockSpec((TM_IN, D), lambda i: (i, 0)),
                  pl.BlockSpec((1, D), lambda i: (0, 0)),
                  _mod_spec(1, TM_IN), _mod_spec(0, TM_IN),
                  pl.BlockSpec((D, 9 * D), lambda i: (0, 0), pipeline_mode=pl.Buffered(1)),
                  pl.BlockSpec((D, LANES), lambda i: (0, 0))],
        out_specs=[pl.BlockSpec((TM_IN, 9 * D), lambda i: (i, 0)),
                   pl.BlockSpec((TM_IN, LANES), lambda i: (i, 0)),
                   pl.BlockSpec((TM_IN, 2 * D), lambda i: (jnp.minimum(i, np_tiles), 0))],
        out_shape=[jax.ShapeDtypeStruct((N_TOK, 9 * D), BF16),
                   jax.ShapeDtypeStruct((N_TOK, LANES), F32),
                   jax.ShapeDtypeStruct((NP_TOK + TM_IN, 2 * D), F32)],
        compiler_params=_cp(("arbitrary",), 60),
        name="inproj",
    )(x, norm_g, mod, mod, w_big, w_gates)


def _conv_act(x, w, cos, sin, scale):
    T = x.shape[0]
    row = lax.broadcasted_iota(I32, (T, 1), 0)
    xp = jnp.where(row == 0, 0.0, pltpu.roll(x, 1, 0))
    xn = jnp.where(row == T - 1, 0.0, pltpu.roll(x, T - 1, 0))
    y = xp * w[0:1] + x * w[1:2] + xn * w[2:3]
    y = y * _sigmoid(y)
    halves = []
    for hlf in range(2):
        sl = slice(hlf * LANES, (hlf + 1) * LANES)
        yh = y[:, sl]
        if cos is not None:
            yh = yh * cos[:, sl] + pltpu.roll(yh, LANES // 2, 1) * sin[:, sl]
        halves.append((yh * scale).astype(BF16))
    return jnp.concatenate(halves, axis=1)


def _rope_tables():
    pos = np.arange(DEC_SEQ)
    rows = (pos // GRID_W).astype(np.float32)
    cols = (pos % GRID_W).astype(np.float32)
    nfreq = M_HD // 4
    inv = (ROPE_BASE ** (-np.arange(nfreq, dtype=np.float32) / nfreq)).astype(np.float32)
    d = np.arange(M_HD)
    p = np.where(d[None, :] < M_HD // 2, rows[:, None], cols[:, None]).astype(np.float32)
    ang = (p * inv[d % nfreq][None, :]).astype(np.float32)
    sign = np.where((d % (M_HD // 2)) < nfreq, -1.0, 1.0).astype(np.float32)
    return jnp.asarray(np.cos(ang), F32), jnp.asarray(np.sin(ang) * sign[None, :], F32)


def _mlstm_body(*refs, T, hb, rope, has_state, emit_state):
    qraw_ref, kraw_ref, v_ref, om_ref, g_ref, gb_ref, hn_ref, cwq_ref, cwk_ref = refs[:9]
    pos = 9
    if rope:
        cos_ref, sin_ref = refs[pos:pos + 2]
        pos += 2
    if has_state:
        c0_ref, n0_ref, m0_ref = refs[pos:pos + 3]
        pos += 3
    oa_ref = refs[pos]
    pos += 1
    if emit_state:
        co_ref, no_ref, mo_ref = refs[pos:pos + 3]
        pos += 3
    nch = 2 * hb
    hs_refs = refs[pos:pos + nch]
    c_refs = refs[pos + nch:pos + 2 * nch]
    n_refs = refs[pos + 2 * nch:pos + 3 * nch]
    q_refs = refs[pos + 3 * nch:pos + 3 * nch + hb]
    k_refs = refs[pos + 3 * nch + hb:pos + 3 * nch + 2 * hb]

    cos = cos_ref[...] if rope else None
    sin = sin_ref[...] if rope else None
    for hh in range(hb):
        cols = slice(hh * M_HD, (hh + 1) * M_HD)
        q_refs[hh][...] = _conv_act(qraw_ref[:, cols].astype(F32), cwq_ref[:, cols], cos, sin, 1.0)
        k_refs[hh][...] = _conv_act(kraw_ref[:, cols].astype(F32), cwk_ref[:, cols], cos, sin, M_HD ** -0.5)

    head0 = pl.program_id(1) * hb
    nc = T // CHUNK
    lane = lax.broadcasted_iota(I32, (1, LANES), 1)
    r_i = lax.broadcasted_iota(I32, (CHUNK, CHUNK), 0)
    c_i = lax.broadcasted_iota(I32, (CHUNK, CHUNK), 1)
    eye = r_i == c_i
    causal = (r_i >= c_i, r_i <= c_i)
    cum = tuple(jnp.where(m, 1.0, 0.0).astype(BF16) for m in causal)

    def to_row(col):
        return jnp.sum(jnp.where(eye, col, 0.0), axis=0, keepdims=True)

    def pick(mat, colidx):
        return jnp.sum(jnp.where(lane == colidx, mat, 0.0), axis=1, keepdims=True)

    def dir_step(d, c, m_prevs):
        rows = pl.ds(pl.multiple_of(c * CHUNK, CHUNK), CHUNK)
        g = g_ref[rows, :] + gb_ref[...]
        lf = jnp.minimum(g, 0.0) - jnp.log(1.0 + jnp.exp(-jnp.abs(g)))
        bmat = _dot_x3(cum[d], lf)
        m_news = []
        for hh in range(hb):
            ch = hh * 2 + d
            hs_ref, c_ref, n_ref = hs_refs[ch], c_refs[ch], n_refs[ch]
            cols = slice(hh * M_HD, (hh + 1) * M_HD)
            m_prev = m_prevs[hh]
            ig_col = pick(g, d * M_HEADS + head0 + hh)
            b_col = pick(bmat, 2 * M_HEADS + d * M_HEADS + head0 + hh)
            ig_row = to_row(ig_col)
            b_row = to_row(b_col)
            b_last = b_row[:, CHUNK - 1:CHUNK] if d == 0 else b_row[:, 0:1]
            logd = jnp.where(causal[d], b_col - b_row + ig_row, -jnp.inf)
            inter = b_col + m_prev
            m_row = jnp.maximum(inter, jnp.max(logd, axis=1, keepdims=True))
            dmat = jnp.exp(logd - m_row)
            s_inter = jnp.exp(inter - m_row)
            q = q_refs[hh][rows, :]
            k = k_refs[hh][rows, :]
            v = v_ref[rows, cols]
            s = _dot_nt(q, k) * dmat
            num = _dot(s.astype(BF16), v) + s_inter * _dot(q, c_ref[...].astype(BF16))
            den = (jnp.sum(s, axis=1, keepdims=True)
                   + s_inter * jnp.sum(q.astype(F32) * n_ref[...], axis=1, keepdims=True))
            hs_ref[rows, :] = num / jnp.maximum(jnp.abs(den), jnp.exp(-m_row))
            log_w = b_last - b_col + ig_col
            m_new = jnp.maximum(b_last + m_prev, jnp.max(log_w, axis=0, keepdims=True))
            w = jnp.exp(log_w - m_new)
            decay = jnp.exp(b_last + m_prev - m_new)
            kw = k.astype(F32) * w
            c_ref[...] = decay * c_ref[...] + _dot_tn(kw.astype(BF16), v)
            n_ref[...] = decay * n_ref[...] + jnp.sum(kw, axis=0, keepdims=True)
            m_news.append(m_new)
        return tuple(m_news)

    m_init = ([], [])
    for hh in range(hb):
        for d in range(2):
            ch = hh * 2 + d
            if has_state:
                c_refs[ch][...] = c0_ref[0, d, hh]
                n_refs[ch][...] = n0_ref[0, d, hh]
                m_init[d].append(m0_ref[0, d, hh])
            else:
                c_refs[ch][...] = jnp.zeros_like(c_refs[ch])
                n_refs[ch][...] = jnp.zeros_like(n_refs[ch])
                m_init[d].append(jnp.zeros((1, 1), F32))

    def body(ci, ms):
        return dir_step(0, ci, ms[0]), dir_step(1, nc - 1 - ci, ms[1])

    m_fin = lax.fori_loop(0, nc, body, (tuple(m_init[0]), tuple(m_init[1])))
    for hh in range(hb):
        cols = slice(hh * M_HD, (hh + 1) * M_HD)
        if emit_state:
            for d in range(2):
                co_ref[0, d, hh] = c_refs[hh * 2 + d][...]
                no_ref[0, d, hh] = n_refs[hh * 2 + d][...]
                mo_ref[0, d, hh] = m_fin[d][hh]
        hm = _rms(hs_refs[hh * 2][...] + hs_refs[hh * 2 + 1][...]) * hn_ref[:, cols]
        oa_ref[:, cols] = (_sigmoid(om_ref[:, cols].astype(F32)) * hm).astype(BF16)


def _mlstm(big, gates, gate_bias, head_norm, conv_w, rope_tabs, state, *, nb, T, row0, hb, emit_state):
    has_state = state is not None
    rope = rope_tabs is not None
    w = hb * M_HD
    npair = M_HEADS // hb
    in_specs = [pl.BlockSpec((T, w), lambda b, h: (row0 + b, h)),
                pl.BlockSpec((T, w), lambda b, h: (row0 + b, npair + h)),
                pl.BlockSpec((T, w), lambda b, h: (row0 + b, 2 * npair + h)),
                pl.BlockSpec((T, w), lambda b, h: (row0 + b, 3 * npair + h)),
                pl.BlockSpec((T, LANES), lambda b, h: (row0 + b, 0)),
                pl.BlockSpec((1, LANES), lambda b, h: (0, 0)),
                pl.BlockSpec((1, w), lambda b, h: (0, h)),
                pl.BlockSpec((3, w), lambda b, h: (0, h)),
                pl.BlockSpec((3, w), lambda b, h: (0, npair + h))]
    args = [big, big, big, big, gates, gate_bias, head_norm, conv_w, conv_w]
    if rope:
        in_specs += [pl.BlockSpec((T, M_HD), lambda b, h: (0, 0))] * 2
        args += list(rope_tabs)
    state_specs = [pl.BlockSpec((1, 2, hb, M_HD, M_HD), lambda b, h: (b, 0, h, 0, 0)),
                   pl.BlockSpec((1, 2, hb, 1, M_HD), lambda b, h: (b, 0, h, 0, 0)),
                   pl.BlockSpec((1, 2, hb, 1, 1), lambda b, h: (b, 0, h, 0, 0))]
    if has_state:
        in_specs += state_specs
        args += list(state)
    out_specs = [pl.BlockSpec((T, w), lambda b, h: (b, h))]
    out_shape = [jax.ShapeDtypeStruct((nb * T, M_HEADS * M_HD), BF16)]
    if emit_state:
        out_specs += state_specs
        out_shape += [jax.ShapeDtypeStruct((nb, 2, M_HEADS, M_HD, M_HD), F32),
                      jax.ShapeDtypeStruct((nb, 2, M_HEADS, 1, M_HD), F32),
                      jax.ShapeDtypeStruct((nb, 2, M_HEADS, 1, 1), F32)]
    nch = 2 * hb
    return pl.pallas_call(
        functools.partial(_mlstm_body, T=T, hb=hb, rope=rope, has_state=has_state, emit_state=emit_state),
        grid=(nb, npair),
        in_specs=in_specs,
        out_specs=out_specs,
        out_shape=out_shape,
        scratch_shapes=[pltpu.VMEM((T, M_HD), F32)] * nch + [pltpu.VMEM((M_HD, M_HD), F32)] * nch
        + [pltpu.VMEM((1, M_HD), F32)] * nch + [pltpu.VMEM((T, M_HD), BF16)] * (2 * hb),
        compiler_params=_cp(("arbitrary", "arbitrary"), 56),
        name="mlstm",
    )(*args)


def _pair_masks():
    lane = lax.broadcasted_iota(I32, (1, LANES), 1)
    first = lane < N_HD
    return first, jnp.logical_not(first)


DA_PAIRS = 4


def _attn_body(q_ref, k_ref, v_ref, o_ref):
    first, second = _pair_masks()
    for p in range(DA_PAIRS):
        sl = slice(p * LANES, (p + 1) * LANES)
        q = q_ref[:, sl]
        q2 = jnp.concatenate([jnp.where(first, q, jnp.zeros_like(q)), jnp.where(second, q, jnp.zeros_like(q))], axis=0)
        s = _dot_nt(q2, k_ref[:, sl]) * (N_HD ** -0.5)
        e = jnp.exp(s - jnp.max(s, axis=-1, keepdims=True))
        o2 = _dot(e.astype(BF16), v_ref[:, sl]) / jnp.sum(e, axis=-1, keepdims=True)
        o_ref[:, sl] = jnp.where(first, o2[0:SEQ], o2[SEQ:2 * SEQ]).astype(BF16)


def _dense_attention(big):
    w = DA_PAIRS * LANES
    cb = D // w
    return pl.pallas_call(
        _attn_body,
        grid=(BATCH, cb),
        in_specs=[pl.BlockSpec((SEQ, w), lambda b, p: (b, 4 * cb + p)),
                  pl.BlockSpec((SEQ, w), lambda b, p: (b, 5 * cb + p)),
                  pl.BlockSpec((SEQ, w), lambda b, p: (b, 6 * cb + p))],
        out_specs=pl.BlockSpec((SEQ, w), lambda b, p: (b, p)),
        out_shape=jax.ShapeDtypeStruct((NP_TOK, D), BF16),
        compiler_params=_cp(("arbitrary", "arbitrary"), 32),
        name="dense_attn",
    )(big, big, big)


NA_DR2 = 2 * MAX_WIN_H - 2
NA_LOCAL = MAX_WIN_H * GRID_W


def _natten_body(q_ref, k_ref, v_ref, kc_ref, vc_ref, tab_ref, mask_ref, o_ref):
    rb = pl.program_id(2)
    kc = kc_ref[0, 0].astype(BF16)
    vc = vc_ref[0, 0].astype(BF16)
    colmask = mask_ref[...]
    first, second = _pair_masks()
    for qr in range(NA_QROWS):
        r = NA_QROWS * rb + qr
        rstart = jnp.clip(r - MAX_WIN_H // 2, 0, GRID_H - MAX_WIN_H)
        ks = pl.multiple_of(rstart * GRID_W, GRID_W)
        kl = k_ref[pl.ds(ks, NA_LOCAL), :]
        vl = v_ref[pl.ds(ks, NA_LOCAL), :]
        dr0 = rstart - r + MAX_WIN_H - 1
        qb = q_ref[qr * GRID_W:(qr + 1) * GRID_W, :]
        q2 = jnp.concatenate([jnp.where(first, qb, jnp.zeros_like(qb)),
                              jnp.where(second, qb, jnp.zeros_like(qb))], axis=0)
        bias = jnp.concatenate(
            [jnp.concatenate([tab_ref[par, dr0 + 2 * kp] for kp in range(MAX_WIN_H // 2)], axis=1) + colmask
             for par in range(2)], axis=0)
        sl = _dot_nt(q2, kl) * (N_HD ** -0.5) + bias
        sc = _dot_nt(q2, kc) * (N_HD ** -0.5)
        mx = jnp.maximum(jnp.max(sl, axis=-1, keepdims=True), jnp.max(sc, axis=-1, keepdims=True))
        el = jnp.exp(sl - mx)
        ec = jnp.exp(sc - mx)
        den = jnp.sum(el, axis=-1, keepdims=True) + jnp.sum(ec, axis=-1, keepdims=True)
        o2 = (_dot(el.astype(BF16), vl) + _dot(ec.astype(BF16), vc)) / den
        o_ref[qr * GRID_W:(qr + 1) * GRID_W, :] = jnp.where(first, o2[0:GRID_W], o2[GRID_W:2 * GRID_W]).astype(BF16)


def _natten(big, cache_k, cache_v, bias_tab, window_mask, layer):
    cb = D // LANES
    nrb = GRID_H // NA_QROWS
    qrows = NA_QROWS * GRID_W
    q0 = NP_TOK // qrows
    b0 = NP_TOK // DEC_SEQ
    return pl.pallas_call(
        _natten_body,
        grid=(DEC_BATCH, N_HEADS // 2, nrb),
        in_specs=[pl.BlockSpec((qrows, LANES), lambda b, p, r: (q0 + b * nrb + r, 4 * cb + p)),
                  pl.BlockSpec((DEC_SEQ, LANES), lambda b, p, r: (b0 + b, 5 * cb + p)),
                  pl.BlockSpec((DEC_SEQ, LANES), lambda b, p, r: (b0 + b, 6 * cb + p)),
                  pl.BlockSpec((1, 1, PAST_LEN, LANES), lambda b, p, r: (b, layer, 0, p)),
                  pl.BlockSpec((1, 1, PAST_LEN, LANES), lambda b, p, r: (b, layer, 0, p)),
                  pl.BlockSpec((2, NA_DR2, GRID_W, 2 * GRID_W), lambda b, p, r: (p, 0, 0, 0)),
                  pl.BlockSpec((GRID_W, NA_LOCAL), lambda b, p, r: (0, 0))],
        out_specs=pl.BlockSpec((qrows, LANES), lambda b, p, r: (b * nrb + r, p)),
        out_shape=jax.ShapeDtypeStruct((NS_TOK, D), BF16),
        compiler_params=_cp(("arbitrary", "arbitrary", "arbitrary"), 40),
        name="natten",
    )(big, big, big, cache_k, cache_v, bias_tab, window_mask)


def _na_bias_table(rpb):
    n = GRID_W
    n_dr = 2 * MAX_WIN_H - 1
    padl = n - WIN_W
    row = jnp.pad(rpb.astype(F32), ((0, 0), (0, 0), (padl, 2 * n - 1 - padl - (2 * WIN_W - 1))), constant_values=NEG)
    flat = jnp.broadcast_to(row[:, :, None, :], (N_HEADS, n_dr, n, 2 * n - 1)).reshape(N_HEADS, n_dr, n * (2 * n - 1))
    toep = flat[:, :, n - 1:n - 1 + n * (2 * n - 2)].reshape(N_HEADS, n_dr, n, 2 * n - 2)[..., :n]
    return jnp.concatenate([toep[:, :-1], toep[:, 1:]], axis=-1)


def _na_window_mask():
    qcol = np.arange(GRID_W)[:, None]
    kcol = (np.arange(NA_LOCAL) % GRID_W)[None, :]
    cstart = np.clip(qcol - WIN_W // 2, 0, GRID_W - WIN_W)
    valid = (kcol >= cstart) & (kcol < cstart + WIN_W)
    return jnp.asarray(np.where(valid, 0.0, NEG).astype(np.float32))


def _postmix_body(oap_ref, oas_ref, obp_ref, obs_ref, ga_ref, gb_ref, x_ref, g1_ref, nf_ref, sc2_ref, sh2_ref,
                  wa_ref, wb_ref, wo_ref, wr_ref, xo_ref, h2_ref, aff_ref):
    prompt = pl.program_id(0) < NP_TILES
    a = _dot(jnp.where(prompt, oap_ref[...], oas_ref[...]), wa_ref[...])
    b = _dot(jnp.where(prompt, obp_ref[...], obs_ref[...]), wb_ref[...])
    merged = _sigmoid(ga_ref[...].astype(F32)) * a + _sigmoid(gb_ref[...].astype(F32)) * b
    xn = x_ref[...] + g1_ref[0] * _dot(merged.astype(BF16), wo_ref[...])
    xo_ref[...] = xn
    h2 = (_rms(xn) * nf_ref[...]) * (1.0 + sc2_ref[0]) + sh2_ref[0]
    h2_ref[...] = h2
    h1, h2m, _ = _split3(h2)
    w1, w2, _ = _split3(wr_ref[...])
    lg = _dot_nt(w1, h1) + _dot_nt(w1, h2m) + _dot_nt(w2, h1)
    e = jnp.exp(lg - jnp.max(lg, axis=0, keepdims=True))
    aff_ref[...] = e / jnp.sum(e, axis=0, keepdims=True)


def _postmix(oa_p, oa_s, ob_p, ob_s, big, x, mod, norm_ffn, wa, wb, wo, wr_t):
    row = lambda i: (i, 0)
    const = lambda i: (0, 0)
    prow = lambda i: (jnp.minimum(i, NP_TILES - 1), 0)
    srow = lambda i: (jnp.maximum(i - NP_TILES, 0), 0)
    return pl.pallas_call(
        _postmix_body,
        grid=(N_TILES,),
        in_specs=[pl.BlockSpec((TM, D), prow), pl.BlockSpec((TM, D), srow),
                  pl.BlockSpec((TM, D), prow), pl.BlockSpec((TM, D), srow),
                  pl.BlockSpec((TM, D), lambda i: (i, 7)), pl.BlockSpec((TM, D), lambda i: (i, 8)),
                  pl.BlockSpec((TM, D), row),
                  _mod_spec(2), pl.BlockSpec((1, D), const), _mod_spec(4), _mod_spec(3),
                  pl.BlockSpec((D, D), const), pl.BlockSpec((D, D), const), pl.BlockSpec((D, D), const),
                  pl.BlockSpec((N_EXPERTS, D), const)],
        out_specs=[pl.BlockSpec((TM, D), row), pl.BlockSpec((TM, D), row),
                   pl.BlockSpec((N_EXPERTS, TM), lambda i: (0, i))],
        out_shape=[jax.ShapeDtypeStruct((N_TOK, D), F32), jax.ShapeDtypeStruct((N_TOK, D), F32),
                   jax.ShapeDtypeStruct((N_EXPERTS, N_TOK), F32)],
        compiler_params=_cp(("arbitrary",), 48),
        name="postmix",
    )(oa_p, oa_s, ob_p, ob_s, big, big, x, mod, norm_ffn, mod, mod, wa, wb, wo, wr_t)


def _route_body(a_ref, idx_ref, gv_ref, off_ref, pos_ref, *, R, cap):
    a = a_ref[...]
    capf = float(cap)

    def count_ge(thr):
        c = jnp.sum(jnp.where(a >= thr, 1.0, 0.0), axis=1, keepdims=True)
        return jnp.sum(c, axis=2, keepdims=True)

    def search(i, lo_bits):
        cand = lo_bits | jnp.left_shift(jnp.int32(1), 30 - i)
        return jnp.where(count_ge(pltpu.bitcast(cand, F32)) >= capf, cand, lo_bits)

    tau_bits = lax.fori_loop(0, 31, search, jnp.zeros((N_EXPERTS, 1, 1), I32))
    lo = pltpu.bitcast(tau_bits, F32)
    hi = pltpu.bitcast(jnp.maximum(tau_bits + 1, jnp.int32(MIN_NORMAL_BITS)), F32)

    def refine(i, lh):
        lo, hi = lh
        mid = lo + (hi - lo) * 0.5
        ok = count_ge(mid) >= capf
        return jnp.where(ok, mid, lo), jnp.where(ok, hi, mid)

    lo, hi = lax.fori_loop(0, 32, refine, (lo, hi))
    above = jnp.where(a >= hi, 1.0, 0.0)
    ties = jnp.where(a >= lo, 1.0, 0.0) - above
    need = capf - jnp.sum(jnp.sum(above, axis=1, keepdims=True), axis=2, keepdims=True)

    l0 = lax.broadcasted_iota(I32, (LANES, LANES), 0)
    l1 = lax.broadcasted_iota(I32, (LANES, LANES), 1)
    upper = jnp.where(l0 <= l1, 1.0, 0.0).astype(BF16)
    r0 = lax.broadcasted_iota(I32, (R, R), 0)
    r1 = lax.broadcasted_iota(I32, (R, R), 1)
    below = jnp.where(r1 < r0, 1.0, 0.0).astype(BF16)
    eye_r = r0 == r1
    r_row = lax.broadcasted_iota(I32, (1, R), 1).astype(F32)
    lane_row = lax.broadcasted_iota(I32, (1, LANES), 1).astype(F32)
    jcol = lax.broadcasted_iota(I32, (cap, 1), 0).astype(F32)

    def prefix(x):
        within = _dot(x.astype(BF16), upper)
        tot = jnp.broadcast_to(within[:, LANES - 1:LANES], (R, LANES))
        offs = _dot(below, tot.astype(BF16))
        return within + offs, offs

    for e in range(N_EXPERTS):
        eq = ties[e]
        cin_eq, _ = prefix(eq)
        sel = above[e] + eq * jnp.where(cin_eq - eq < need[e], 1.0, 0.0)
        cin, offs = prefix(sel)
        cend_row = jnp.sum(jnp.where(eye_r, cin[:, LANES - 1:LANES], 0.0), axis=0, keepdims=True)
        bcol = jnp.sum(jnp.where(cend_row <= jcol, 1.0, 0.0), axis=1, keepdims=True)
        onehot = jnp.where(bcol == r_row, 1.0, 0.0).astype(BF16)
        lcol = jnp.sum(jnp.where(_dot_x3(onehot, cin) <= jcol, 1.0, 0.0), axis=1, keepdims=True)
        idx_ref[e] = (bcol * LANES + lcol).astype(I32)
        gv_ref[e] = jnp.sum(jnp.where(lane_row == lcol, _dot_x3(onehot, a[e]), 0.0), axis=1, keepdims=True)
        off_ref[e] = offs[:, 0:1]
        pos_ref[e] = jnp.where(sel > 0.0, cin - 1.0, -1.0)


def _route(aff3, cap):
    R = aff3.shape[1]
    full = lambda s: pl.BlockSpec(s, lambda i: (0, 0, 0))
    return pl.pallas_call(
        functools.partial(_route_body, R=R, cap=cap),
        grid=(1,),
        in_specs=[full((N_EXPERTS, R, LANES))],
        out_specs=[full((N_EXPERTS, cap, 1)), full((N_EXPERTS, cap, 1)), full((N_EXPERTS, R, 1)),
                   full((N_EXPERTS, R, LANES))],
        out_shape=[jax.ShapeDtypeStruct((N_EXPERTS, cap, 1), I32),
                   jax.ShapeDtypeStruct((N_EXPERTS, cap, 1), F32),
                   jax.ShapeDtypeStruct((N_EXPERTS, R, 1), F32),
                   jax.ShapeDtypeStruct((N_EXPERTS, R, LANES), F32)],
        compiler_params=_cp(("arbitrary",), 48),
        name="route",
    )(aff3)


GATHER_UNROLL = 8


def _gather_copy(h_hbm, xs_ref, sem, src_row, dst_row, nrows):
    return pltpu.make_async_copy(h_hbm.at[pl.ds(src_row, nrows), :], xs_ref.at[pl.ds(dst_row, nrows), :], sem)


def _slab_copy(y_ref, ys_hbm, sem, e, s):
    return pltpu.make_async_copy(y_ref.at[:, s * LANES:(s + 1) * LANES], ys_hbm.at[e, :, s, :], sem)


def _ffn_body(idx_ref, h_hbm, wg_ref, wu_ref, wd_ref, gv_ref, ys_hbm, xs_ref, xb_ref, y_ref, gsem, osem):
    e = pl.program_id(0)
    f = pl.program_id(1)
    last_e = pl.num_programs(0) - 1

    @pl.when(f == 0)
    def _():
        def issue(j, carry):
            _gather_copy(h_hbm, xs_ref, gsem, idx_ref[e * CAP_TOT + j], j, 1).start()
            return carry
        lax.fori_loop(0, CAP_TOT, issue, 0, unroll=GATHER_UNROLL)

        @pl.when(e > 0)
        def _():
            for s in range(SLAB):
                _slab_copy(y_ref, ys_hbm, osem, e - 1, s).wait()

        _gather_copy(h_hbm, xs_ref, gsem, 0, 0, CAP_TOT).wait()
        xb_ref[...] = xs_ref[...].astype(BF16)

    xb = xb_ref[...]
    g = _dot(xb, wg_ref[0, 0].astype(BF16))
    u = _dot(xb, wu_ref[0, 0].astype(BF16))
    hid = (g * _sigmoid(g) * u).astype(BF16)
    y = _dot(hid, wd_ref[0, 0].astype(BF16))

    @pl.when(f == 0)
    def _():
        y_ref[...] = y

    @pl.when(f > 0)
    def _():
        y_ref[...] = y_ref[...] + y

    @pl.when(f == pl.num_programs(1) - 1)
    def _():
        y_ref[...] = y_ref[...] * gv_ref[0]
        for s in range(SLAB):
            _slab_copy(y_ref, ys_hbm, osem, e, s).start()

        @pl.when(e == last_e)
        def _():
            for s in range(SLAB):
                _slab_copy(y_ref, ys_hbm, osem, e, s).wait()


def _ffn(idx_flat, h2, wg, wu, wd, gv, layer):
    nf = EXPERT_FF // FF_CHUNK
    return pl.pallas_call(
        _ffn_body,
        grid_spec=pltpu.PrefetchScalarGridSpec(
            num_scalar_prefetch=1,
            grid=(N_EXPERTS, nf),
            in_specs=[pl.BlockSpec(memory_space=pl.ANY),
                      pl.BlockSpec((1, 1, D, FF_CHUNK), lambda e, f, idx: (layer, e, 0, f)),
                      pl.BlockSpec((1, 1, D, FF_CHUNK), lambda e, f, idx: (layer, e, 0, f)),
                      pl.BlockSpec((1, 1, FF_CHUNK, D), lambda e, f, idx: (layer, e, f, 0)),
                      pl.BlockSpec((1, CAP_TOT, 1), lambda e, f, idx: (e, 0, 0))],
            out_specs=pl.BlockSpec(memory_space=pl.ANY),
            scratch_shapes=[pltpu.VMEM((CAP_TOT, D), F32), pltpu.VMEM((CAP_TOT, D), BF16),
                            pltpu.VMEM((CAP_TOT, D), F32),
                            pltpu.SemaphoreType.DMA(()), pltpu.SemaphoreType.DMA(())]),
        out_shape=jax.ShapeDtypeStruct((N_EXPERTS, CAP_TOT, SLAB, LANES), F32),
        compiler_params=_cp(("arbitrary", "arbitrary"), 58),
        name="expert_ffn",
    )(idx_flat, h2, wg, wu, wd, gv)


MAX_PAIRS = N_EXPERTS * TM
SEG_BITS = TM.bit_length()


def _seg_copy(ys_hbm, w_ref, sem, slot, e, src_row, dst_row, nrows):
    src = pl.multiple_of(src_row * SLAB, SLAB)
    dst = pl.multiple_of(dst_row * SLAB, SLAB)
    return pltpu.make_async_copy(ys_hbm.at[e, pl.ds(src, nrows * SLAB), :],
                                 w_ref.at[slot, pl.ds(dst, nrows * SLAB), :], sem.at[slot])


def _combine_body(lo_ref, hi_ref, ys_hbm, pos_ref, x_ref, g2_ref, nf_ref, o_ref, w_ref, acc_ref, oh_ref, sem, *, last):
    b = pl.program_id(0)
    lane = lax.broadcasted_iota(I32, (1, LANES), 1)

    def layout(t):
        los, segs = [], []
        off = jnp.int32(0)
        for e in range(N_EXPERTS):
            lo = lo_ref[e * N_TILES + t]
            n = hi_ref[e * N_TILES + t] - lo
            los.append(lo)
            segs.append((off, off + n))
            off = off + n
        return los, segs

    def fetch(t, slot):
        los, segs = layout(t)
        for e in range(N_EXPERTS):
            n = segs[e][1] - segs[e][0]
            for bit in range(SEG_BITS - 1, -1, -1):
                done = (n >> (bit + 1)) << (bit + 1)

                @pl.when((n & (1 << bit)) != 0)
                def _(e=e, done=done, bit=bit):
                    _seg_copy(ys_hbm, w_ref, sem, slot, e, los[e] + done, segs[e][0] + done, 1 << bit).start()

    slot = b % 2

    @pl.when(b == 0)
    def _():
        fetch(b, slot)

    @pl.when(b + 1 < pl.num_programs(0))
    def _():
        fetch(b + 1, 1 - slot)

    los, seg = layout(b)
    npairs = seg[N_EXPERTS - 1][1]
    for bit in range(MAX_PAIRS.bit_length() - 1, -1, -1):
        @pl.when((npairs & (1 << bit)) != 0)
        def _(bit=bit):
            _seg_copy(ys_hbm, w_ref, sem, slot, 0, 0, 0, 1 << bit).wait()

    shift = jnp.zeros((1, LANES), F32)
    for e in range(N_EXPERTS):
        shift = jnp.where(lane == e, (seg[e][0] - los[e]).astype(F32), shift)
    acc_ref[...] = jnp.zeros_like(acc_ref)
    pos = pos_ref[...]
    prow = jnp.where(pos >= 0.0, pos + shift[:, 0:N_EXPERTS], -1.0)
    sub = lax.broadcasted_iota(I32, (TM, 1), 0)
    pair = lax.broadcasted_iota(I32, (1, TM), 1).astype(F32)

    def body(c, carry):
        valid = (c * TM + sub) < npairs
        local = prow - (c * TM).astype(F32)
        oh_ref[...] = jnp.zeros_like(oh_ref)
        for e in range(N_EXPERTS):
            @pl.when((seg[e][0] < (c + 1) * TM) & (seg[e][1] > c * TM))
            def _(e=e):
                oh_ref[...] += jnp.where(local[:, e:e + 1] == pair, 1.0, 0.0)
        onehot = oh_ref[...].astype(BF16)
        data = jnp.concatenate([w_ref[slot, pl.ds(c * TM * SLAB + s, TM, stride=SLAB), :] for s in range(SLAB)],
                               axis=1)
        hi, mid, _ = _split3(jnp.where(valid, data, 0.0))
        acc_ref[...] += _dot(onehot, hi) + _dot(onehot, mid)
        return carry

    lax.fori_loop(0, (npairs + TM - 1) // TM, body, 0)
    xn = x_ref[...] + g2_ref[0] * acc_ref[...]
    if last:
        xn = _rms(xn) * nf_ref[...]
    o_ref[...] = xn


def _combine(seg_lo, seg_hi, ys, pos_t, x, mod, norm_final, *, last):
    return pl.pallas_call(
        functools.partial(_combine_body, last=last),
        grid_spec=pltpu.PrefetchScalarGridSpec(
            num_scalar_prefetch=2,
            grid=(N_TILES,),
            in_specs=[pl.BlockSpec(memory_space=pl.ANY),
                      pl.BlockSpec((TM, N_EXPERTS), lambda i, lo, hi: (i, 0)),
                      pl.BlockSpec((TM, D), lambda i, lo, hi: (i, 0)),
                      pl.BlockSpec((1, 1, D), lambda i, lo, hi: (_mod_row(i), 0, 5)),
                      pl.BlockSpec((1, D), lambda i, lo, hi: (0, 0))],
            out_specs=pl.BlockSpec((TM, D), lambda i, lo, hi: (i, 0)),
            scratch_shapes=[pltpu.VMEM((2, MAX_PAIRS * SLAB, LANES), F32), pltpu.VMEM((TM, D), F32),
                            pltpu.VMEM((TM, TM), F32), pltpu.SemaphoreType.DMA((2,))]),
        out_shape=jax.ShapeDtypeStruct((N_TOK, D), F32),
        compiler_params=_cp(("arbitrary",), 56),
        name="combine",
    )(seg_lo, seg_hi, ys, pos_t, x, mod, norm_final)


def _tile_segments(off_p, off_s):
    per = TM // LANES
    lo_p = off_p[:, ::per, 0].astype(I32)
    lo_s = off_s[:, ::per, 0].astype(I32) + CAP_P
    hi_p = jnp.concatenate([lo_p[:, 1:], jnp.full((N_EXPERTS, 1), CAP_P, I32)], axis=1)
    hi_s = jnp.concatenate([lo_s[:, 1:], jnp.full((N_EXPERTS, 1), CAP_TOT, I32)], axis=1)
    lo = jnp.concatenate([lo_p, lo_s], axis=1)
    hi = jnp.concatenate([hi_p, hi_s], axis=1)
    return lo.reshape(-1), hi.reshape(-1)


def kernel(x_prompt, x_sample, cache_na_k, cache_na_v, state_mlstm_C, state_mlstm_n, state_mlstm_m, c, c_ctx, w_ada, b_ada, norm_mix, norm_ffn, w_in, conv_qk, mlstm_gate_bias, mlstm_head_norm, na_rpb, w_branch_a, w_branch_b, w_out, w_router, w_expert_gate, w_expert_up, w_expert_down, norm_final):
    depth = w_in.shape[0]
    m_width = M_HEADS * M_HD
    n_gates = 4 * M_HEADS

    x = jnp.concatenate([x_prompt.reshape(NP_TOK, D), x_sample.reshape(NS_TOK, D)], axis=0)
    cond8 = jnp.concatenate([c_ctx[None], c, jnp.zeros((8 - 1 - DEC_BATCH, D), F32)], axis=0)
    mods = _adaln(cond8, w_ada, b_ada).reshape(depth, 8, 1, 6 * D)
    cos_t, sin_t = _rope_tables()
    cache_k = cache_na_k.reshape(DEC_BATCH, depth, PAST_LEN, D)
    cache_v = cache_na_v.reshape(DEC_BATCH, depth, PAST_LEN, D)
    norm_final2 = norm_final.reshape(1, D)
    window_mask = _na_window_mask()

    ks_, vs_, cs_, ns_, ms_ = [], [], [], [], []
    for l in range(depth):
        mod = mods[l]
        w_big, w_gates = _pack_w_in(w_in, l)
        gate_bias = jnp.pad(mlstm_gate_bias[l], (0, LANES - n_gates)).reshape(1, LANES)
        head_norm = mlstm_head_norm[l].reshape(1, m_width)

        big, gates, kv = _inproj(x, norm_mix[l].reshape(1, D), mod, w_big, w_gates)

        oa_p, c_new, n_new, m_new = _mlstm(big, gates, gate_bias, head_norm, conv_qk[l], None, None,
                                           nb=BATCH, T=SEQ, row0=0, hb=M_HEADS, emit_state=True)
        state = (state_mlstm_C[:, l], state_mlstm_n[:, l].reshape(DEC_BATCH, 2, M_HEADS, 1, M_HD),
                 state_mlstm_m[:, l].reshape(DEC_BATCH, 2, M_HEADS, 1, 1))
        (oa_s,) = _mlstm(big, gates, gate_bias, head_norm, conv_qk[l], (cos_t, sin_t), state,
                         nb=DEC_BATCH, T=DEC_SEQ, row0=NP_TOK // DEC_SEQ, hb=M_HEADS // 2, emit_state=False)

        ob_p = _dense_attention(big)
        ob_s = _natten(big, cache_k, cache_v, _na_bias_table(na_rpb[l]), window_mask, l)

        x, h2, aff_t = _postmix(oa_p, oa_s, ob_p, ob_s, big, x, mod, norm_ffn[l].reshape(1, D),
                                w_branch_a[l].astype(BF16), w_branch_b[l].astype(BF16), w_out[l].astype(BF16),
                                w_router[l].T)

        idx_p, gv_p, off_p, pos_p = _route(aff_t[:, :NP_TOK].reshape(N_EXPERTS, NP_TOK // LANES, LANES), CAP_P)
        idx_s, gv_s, off_s, pos_s = _route(aff_t[:, NP_TOK:].reshape(N_EXPERTS, NS_TOK // LANES, LANES), CAP_S)
        idx = jnp.concatenate([idx_p, idx_s + NP_TOK], axis=1)
        gv = jnp.concatenate([gv_p, gv_s], axis=1)
        pos_s = jnp.where(pos_s >= 0.0, pos_s + CAP_P, pos_s)
        pos_t = jnp.concatenate([pos_p.reshape(N_EXPERTS, NP_TOK), pos_s.reshape(N_EXPERTS, NS_TOK)], axis=1).T
        ys = _ffn(idx.reshape(-1), h2, w_expert_gate, w_expert_up, w_expert_down, gv, l)
        ys = ys.reshape(N_EXPERTS, CAP_TOT * SLAB, LANES)
        seg_lo, seg_hi = _tile_segments(off_p, off_s)
        x = _combine(seg_lo, seg_hi, ys, pos_t, x, mod, norm_final2, last=(l == depth - 1))

        ks_.append(kv[:NP_TOK, :D].reshape(BATCH, SEQ, N_HEADS, N_HD))
        vs_.append(kv[:NP_TOK, D:].reshape(BATCH, SEQ, N_HEADS, N_HD))
        cs_.append(c_new)
        ns_.append(n_new.reshape(BATCH, 2, M_HEADS, M_HD))
        ms_.append(m_new.reshape(BATCH, 2, M_HEADS))

    y_prompt = x[:NP_TOK].reshape(BATCH, SEQ, D)
    y_sample = x[NP_TOK:].reshape(DEC_BATCH, DEC_SEQ, D)
    return (y_prompt, y_sample, jnp.stack(ks_, axis=1), jnp.stack(vs_, axis=1),
            jnp.stack(cs_, axis=1), jnp.stack(ns_, axis=1), jnp.stack(ms_, axis=1))
```

```python
import functools

import numpy as np
import jax
import jax.numpy as jnp
from jax import lax
from jax.experimental import pallas as pl
from jax.experimental.pallas import tpu as pltpu

F32 = jnp.float32
BF16 = jnp.bfloat16
I32 = jnp.int32

D = 1024
BATCH, SEQ = 16, 256
DEC_BATCH, DEC_SEQ = 4, 2048
PAST_LEN = 512
NP_TOK = BATCH * SEQ
NS_TOK = DEC_BATCH * DEC_SEQ
N_TOK = NP_TOK + NS_TOK
GRID_W = 64
GRID_H = DEC_SEQ // GRID_W
M_HEADS, M_HD = 4, 256
CHUNK = 128
N_HEADS, N_HD = 16, 64
MAX_WIN_H, WIN_W = 8, 16
N_EXPERTS, EXPERT_FF = 16, 2048
CAP_P = 2 * NP_TOK // N_EXPERTS
CAP_S = 2 * NS_TOK // N_EXPERTS
CAP_TOT = CAP_P + CAP_S
ROPE_BASE = 10000.0
EPS = 1e-6
NEG = -1e30
MIN_NORMAL_BITS = 0x00800000

TM = 256
N_TILES = N_TOK // TM
NP_TILES = NP_TOK // TM
LANES = 128
NA_QROWS = 8
FF_CHUNK = 512
SLAB = D // LANES


def _cp(sem, vmem_mb):
    return pltpu.CompilerParams(dimension_semantics=sem, vmem_limit_bytes=vmem_mb * 2 ** 20)


def _dot(a, b):
    return jnp.dot(a, b, preferred_element_type=F32)


def _dot_nt(a, b):
    return lax.dot_general(a, b, (((1,), (1,)), ((), ())), preferred_element_type=F32)


def _dot_tn(a, b):
    return lax.dot_general(a, b, (((0,), (0,)), ((), ())), preferred_element_type=F32)


def _split3(x):
    hi = x.astype(BF16)
    r = x - hi.astype(F32)
    mid = r.astype(BF16)
    lo = (r - mid.astype(F32)).astype(BF16)
    return hi, mid, lo


def _dot_x3(a_bf, x):
    hi, mid, lo = _split3(x)
    return _dot(a_bf, hi) + _dot(a_bf, mid) + _dot(a_bf, lo)


def _sigmoid(x):
    return 1.0 / (1.0 + jnp.exp(-x))


def _rms(x):
    return x * lax.rsqrt(jnp.mean(x * x, axis=-1, keepdims=True) + EPS)


def _mod_row(i, tm=TM):
    return jnp.where(i < NP_TOK // tm, 0, 1 + (i - NP_TOK // tm) // (DEC_SEQ // tm))


def _mod_spec(kind, tm=TM):
    return pl.BlockSpec((1, 1, D), lambda i: (_mod_row(i, tm), 0, kind))


def _adaln_body(c_ref, w_ref, b_ref, o_ref):
    c = c_ref[...]
    s = (c * _sigmoid(c)).astype(BF16)
    o_ref[0] = _dot(s, w_ref[0].astype(BF16)) + b_ref[0]


def _adaln(cond8, w_ada, b_ada):
    depth = w_ada.shape[0]
    tn = 1536
    return pl.pallas_call(
        _adaln_body,
        grid=(depth, 6 * D // tn),
        in_specs=[pl.BlockSpec((8, D), lambda l, j: (0, 0)),
                  pl.BlockSpec((1, D, tn), lambda l, j: (l, 0, j)),
                  pl.BlockSpec((1, 1, tn), lambda l, j: (l, 0, j))],
        out_specs=pl.BlockSpec((1, 8, tn), lambda l, j: (l, 0, j)),
        out_shape=jax.ShapeDtypeStruct((depth, 8, 6 * D), F32),
        compiler_params=_cp(("arbitrary", "arbitrary"), 40),
        name="adaln",
    )(cond8, w_ada, b_ada.reshape(depth, 1, 6 * D))


N_GATES = 4 * M_HEADS
GATE_COL = 4 * M_HEADS * M_HD


def _pack_body(w_ref, big_ref, gates_ref):
    w = w_ref[0]
    big_ref[:, 0:GATE_COL] = w[:, 0:GATE_COL].astype(BF16)
    big_ref[:, GATE_COL:9 * D] = w[:, GATE_COL + N_GATES:9 * D + N_GATES].astype(BF16)
    lane = lax.broadcasted_iota(I32, (1, LANES), 1)
    gates_ref[...] = jnp.where(lane < N_GATES, w[:, GATE_COL:GATE_COL + LANES], 0.0).astype(BF16)


def _pack_w_in(w, layer):
    tr = 128
    return pl.pallas_call(
        _pack_body,
        grid=(D // tr,),
        in_specs=[pl.BlockSpec((1, tr, w.shape[2]), lambda i: (layer, i, 0))],
        out_specs=[pl.BlockSpec((tr, 9 * D), lambda i: (i, 0)), pl.BlockSpec((tr, LANES), lambda i: (i, 0))],
        out_shape=[jax.ShapeDtypeStruct((D, 9 * D), BF16), jax.ShapeDtypeStruct((D, LANES), BF16)],
        compiler_params=_cp(("arbitrary",), 48),
        name="pack_w_in",
    )(w)


def _inproj_body(x_ref, g_ref, sc_ref, sh_ref, w_ref, wg_ref, big_ref, gates_ref, kv_ref):
    i = pl.program_id(0)
    h = (_rms(x_ref[...]) * g_ref[...]) * (1.0 + sc_ref[0]) + sh_ref[0]
    hb = h.astype(BF16)
    gates_ref[...] = _dot(hb, wg_ref[...])
    for c in range(9):
        r = _dot(hb, w_ref[:, c * D:(c + 1) * D])
        big_ref[:, c * D:(c + 1) * D] = r.astype(BF16)
        if c in (5, 6):
            @pl.when(i < NP_TOK // TM_IN)
            def _():
                kv_ref[:, (c - 5) * D:(c - 4) * D] = r

    @pl.when(i == NP_TOK // TM_IN)
    def _():
        kv_ref[...] = jnp.zeros_like(kv_ref)


TM_IN = 512


def _inproj(x, norm_g, mod, w_big, w_gates):
    np_tiles = NP_TOK // TM_IN
    return pl.pallas_call(
        _inproj_body,
        grid=(N_TOK // TM_IN,),
        in_specs=[pl.BlockSpec((TM_IN, D), lambda i: (i, 0)),
                  pl.BlockSpec((1, D), lambda i: (0, 0)),
                  _mod_spec(1, TM_IN), _mod_spec(0, TM_IN),
                  pl.BlockSpec((D, 9 * D), lambda i: (0, 0), pipeline_mode=pl.Buffered(1)),
                  pl.BlockSpec((D, LANES), lambda i: (0, 0))],
        out_specs=[pl.BlockSpec((TM_IN, 9 * D), lambda i: (i, 0)),
                   pl.BlockSpec((TM_IN, LANES), lambda i: (i, 0)),
                   pl.BlockSpec((TM_IN, 2 * D), lambda i: (jnp.minimum(i, np_tiles), 0))],
        out_shape=[jax.ShapeDtypeStruct((N_TOK, 9 * D), BF16),
                   jax.ShapeDtypeStruct((N_TOK, LANES), F32),
                   jax.ShapeDtypeStruct((NP_TOK + TM_IN, 2 * D), F32)],
        compiler_params=_cp(("arbitrary",), 60),
        name="inproj",
    )(x, norm_g, mod, mod, w_big, w_gates)


def _conv_act(x, w, cos, sin, scale):
    T = x.shape[0]
    row = lax.broadcasted_iota(I32, (T, 1), 0)
    xp = jnp.where(row == 0, 0.0, pltpu.roll(x, 1, 0))
    xn = jnp.where(row == T - 1, 0.0, pltpu.roll(x, T - 1, 0))
    y = xp * w[0:1] + x * w[1:2] + xn * w[2:3]
    y = y * _sigmoid(y)
    halves = []
    for hlf in range(2):
        sl = slice(hlf * LANES, (hlf + 1) * LANES)
        yh = y[:, sl]
        if cos is not None:
            yh = yh * cos[:, sl] + pltpu.roll(yh, LANES // 2, 1) * sin[:, sl]
        halves.append((yh * scale).astype(BF16))
    return jnp.concatenate(halves, axis=1)


def _rope_tables():
    pos = np.arange(DEC_SEQ)
    rows = (pos // GRID_W).astype(np.float32)
    cols = (pos % GRID_W).astype(np.float32)
    nfreq = M_HD // 4
    inv = (ROPE_BASE ** (-np.arange(nfreq, dtype=np.float32) / nfreq)).astype(np.float32)
    d = np.arange(M_HD)
    p = np.where(d[None, :] < M_HD // 2, rows[:, None], cols[:, None]).astype(np.float32)
    ang = (p * inv[d % nfreq][None, :]).astype(np.float32)
    sign = np.where((d % (M_HD // 2)) < nfreq, -1.0, 1.0).astype(np.float32)
    return jnp.asarray(np.cos(ang), F32), jnp.asarray(np.sin(ang) * sign[None, :], F32)


def _mlstm_body(*refs, T, hb, rope, has_state, emit_state):
    qraw_ref, kraw_ref, v_ref, om_ref, g_ref, gb_ref, hn_ref, cwq_ref, cwk_ref = refs[:9]
    pos = 9
    if rope:
        cos_ref, sin_ref = refs[pos:pos + 2]
        pos += 2
    if has_state:
        c0_ref, n0_ref, m0_ref = refs[pos:pos + 3]
        pos += 3
    oa_ref = refs[pos]
    pos += 1
    if emit_state:
        co_ref, no_ref, mo_ref = refs[pos:pos + 3]
        pos += 3
    nch = 2 * hb
    hs_refs = refs[pos:pos + nch]
    c_refs = refs[pos + nch:pos + 2 * nch]
    n_refs = refs[pos + 2 * nch:pos + 3 * nch]
    q_refs = refs[pos + 3 * nch:pos + 3 * nch + hb]
    k_refs = refs[pos + 3 * nch + hb:pos + 3 * nch + 2 * hb]

    cos = cos_ref[...] if rope else None
    sin = sin_ref[...] if rope else None
    for hh in range(hb):
        cols = slice(hh * M_HD, (hh + 1) * M_HD)
        q_refs[hh][...] = _conv_act(qraw_ref[:, cols].astype(F32), cwq_ref[:, cols], cos, sin, 1.0)
        k_refs[hh][...] = _conv_act(kraw_ref[:, cols].astype(F32), cwk_ref[:, cols], cos, sin, M_HD ** -0.5)

    head0 = pl.program_id(1) * hb
    nc = T // CHUNK
    lane = lax.broadcasted_iota(I32, (1, LANES), 1)
    r_i = lax.broadcasted_iota(I32, (CHUNK, CHUNK), 0)
    c_i = lax.broadcasted_iota(I32, (CHUNK, CHUNK), 1)
    eye = r_i == c_i
    causal = (r_i >= c_i, r_i <= c_i)
    cum = tuple(jnp.where(m, 1.0, 0.0).astype(BF16) for m in causal)

    def to_row(col):
        return jnp.sum(jnp.where(eye, col, 0.0), axis=0, keepdims=True)

    def pick(mat, colidx):
        return jnp.sum(jnp.where(lane == colidx, mat, 0.0), axis=1, keepdims=True)

    def dir_step(d, c, m_prevs):
        rows = pl.ds(pl.multiple_of(c * CHUNK, CHUNK), CHUNK)
        g = g_ref[rows, :] + gb_ref[...]
        lf = jnp.minimum(g, 0.0) - jnp.log(1.0 + jnp.exp(-jnp.abs(g)))
        bmat = _dot_x3(cum[d], lf)
        m_news = []
        for hh in range(hb):
            ch = hh * 2 + d
            hs_ref, c_ref, n_ref = hs_refs[ch], c_refs[ch], n_refs[ch]
            cols = slice(hh * M_HD, (hh + 1) * M_HD)
            m_prev = m_prevs[hh]
            ig_col = pick(g, d * M_HEADS + head0 + hh)
            b_col = pick(bmat, 2 * M_HEADS + d * M_HEADS + head0 + hh)
            ig_row = to_row(ig_col)
            b_row = to_row(b_col)
            b_last = b_row[:, CHUNK - 1:CHUNK] if d == 0 else b_row[:, 0:1]
            logd = jnp.where(causal[d], b_col - b_row + ig_row, -jnp.inf)
            inter = b_col + m_prev
            m_row = jnp.maximum(inter, jnp.max(logd, axis=1, keepdims=True))
            dmat = jnp.exp(logd - m_row)
            s_inter = jnp.exp(inter - m_row)
            q = q_refs[hh][rows, :]
            k = k_refs[hh][rows, :]
            v = v_ref[rows, cols]
            s = _dot_nt(q, k) * dmat
            num = _dot(s.astype(BF16), v) + s_inter * _dot(q, c_ref[...].astype(BF16))
            den = (jnp.sum(s, axis=1, keepdims=True)
                   + s_inter * jnp.sum(q.astype(F32) * n_ref[...], axis=1, keepdims=True))
            hs_ref[rows, :] = num / jnp.maximum(jnp.abs(den), jnp.exp(-m_row))
            log_w = b_last - b_col + ig_col
            m_new = jnp.maximum(b_last + m_prev, jnp.max(log_w, axis=0, keepdims=True))
            w = jnp.exp(log_w - m_new)
            decay = jnp.exp(b_last + m_prev - m_new)
            kw = k.astype(F32) * w
            c_ref[...] = decay * c_ref[...] + _dot_tn(kw.astype(BF16), v)
            n_ref[...] = decay * n_ref[...] + jnp.sum(kw, axis=0, keepdims=True)
            m_news.append(m_new)
        return tuple(m_news)

    m_init = ([], [])
    for hh in range(hb):
        for d in range(2):
            ch = hh * 2 + d
            if has_state:
                c_refs[ch][...] = c0_ref[0, d, hh]
                n_refs[ch][...] = n0_ref[0, d, hh]
                m_init[d].append(m0_ref[0, d, hh])
            else:
                c_refs[ch][...] = jnp.zeros_like(c_refs[ch])
                n_refs[ch][...] = jnp.zeros_like(n_refs[ch])
                m_init[d].append(jnp.zeros((1, 1), F32))

    def body(ci, ms):
        return dir_step(0, ci, ms[0]), dir_step(1, nc - 1 - ci, ms[1])

    m_fin = lax.fori_loop(0, nc, body, (tuple(m_init[0]), tuple(m_init[1])))
    for hh in range(hb):
        cols = slice(hh * M_HD, (hh + 1) * M_HD)
        if emit_state:
            for d in range(2):
                co_ref[0, d, hh] = c_refs[hh * 2 + d][...]
                no_ref[0, d, hh] = n_refs[hh * 2 + d][...]
                mo_ref[0, d, hh] = m_fin[d][hh]
        hm = _rms(hs_refs[hh * 2][...] + hs_refs[hh * 2 + 1][...]) * hn_ref[:, cols]
        oa_ref[:, cols] = (_sigmoid(om_ref[:, cols].astype(F32)) * hm).astype(BF16)


def _mlstm(big, gates, gate_bias, head_norm, conv_w, rope_tabs, state, *, nb, T, row0, hb, emit_state):
    has_state = state is not None
    rope = rope_tabs is not None
    w = hb * M_HD
    npair = M_HEADS // hb
    in_specs = [pl.BlockSpec((T, w), lambda b, h: (row0 + b, h)),
                pl.BlockSpec((T, w), lambda b, h: (row0 + b, npair + h)),
                pl.BlockSpec((T, w), lambda b, h: (row0 + b, 2 * npair + h)),
                pl.BlockSpec((T, w), lambda b, h: (row0 + b, 3 * npair + h)),
                pl.BlockSpec((T, LANES), lambda b, h: (row0 + b, 0)),
                pl.BlockSpec((1, LANES), lambda b, h: (0, 0)),
                pl.BlockSpec((1, w), lambda b, h: (0, h)),
                pl.BlockSpec((3, w), lambda b, h: (0, h)),
                pl.BlockSpec((3, w), lambda b, h: (0, npair + h))]
    args = [big, big, big, big, gates, gate_bias, head_norm, conv_w, conv_w]
    if rope:
        in_specs += [pl.BlockSpec((T, M_HD), lambda b, h: (0, 0))] * 2
        args += list(rope_tabs)
    state_specs = [pl.BlockSpec((1, 2, hb, M_HD, M_HD), lambda b, h: (b, 0, h, 0, 0)),
                   pl.BlockSpec((1, 2, hb, 1, M_HD), lambda b, h: (b, 0, h, 0, 0)),
                   pl.BlockSpec((1, 2, hb, 1, 1), lambda b, h: (b, 0, h, 0, 0))]
    if has_state:
        in_specs += state_specs
        args += list(state)
    out_specs = [pl.BlockSpec((T, w), lambda b, h: (b, h))]
    out_shape = [jax.ShapeDtypeStruct((nb * T, M_HEADS * M_HD), BF16)]
    if emit_state:
        out_specs += state_specs
        out_shape += [jax.ShapeDtypeStruct((nb, 2, M_HEADS, M_HD, M_HD), F32),
                      jax.ShapeDtypeStruct((nb, 2, M_HEADS, 1, M_HD), F32),
                      jax.ShapeDtypeStruct((nb, 2, M_HEADS, 1, 1), F32)]
    nch = 2 * hb
    return pl.pallas_call(
        functools.partial(_mlstm_body, T=T, hb=hb, rope=rope, has_state=has_state, emit_state=emit_state),
        grid=(nb, npair),
        in_specs=in_specs,
        out_specs=out_specs,
        out_shape=out_shape,
        scratch_shapes=[pltpu.VMEM((T, M_HD), F32)] * nch + [pltpu.VMEM((M_HD, M_HD), F32)] * nch
        + [pltpu.VMEM((1, M_HD), F32)] * nch + [pltpu.VMEM((T, M_HD), BF16)] * (2 * hb),
        compiler_params=_cp(("arbitrary", "arbitrary"), 56),
        name="mlstm",
    )(*args)


def _pair_masks():
    lane = lax.broadcasted_iota(I32, (1, LANES), 1)
    first = lane < N_HD
    return first, jnp.logical_not(first)


DA_PAIRS = 4


def _attn_body(q_ref, k_ref, v_ref, o_ref):
    first, second = _pair_masks()
    for p in range(DA_PAIRS):
        sl = slice(p * LANES, (p + 1) * LANES)
        q = q_ref[:, sl]
        q2 = jnp.concatenate([jnp.where(first, q, jnp.zeros_like(q)), jnp.where(second, q, jnp.zeros_like(q))], axis=0)
        s = _dot_nt(q2, k_ref[:, sl]) * (N_HD ** -0.5)
        e = jnp.exp(s - jnp.max(s, axis=-1, keepdims=True))
        o2 = _dot(e.astype(BF16), v_ref[:, sl]) / jnp.sum(e, axis=-1, keepdims=True)
        o_ref[:, sl] = jnp.where(first, o2[0:SEQ], o2[SEQ:2 * SEQ]).astype(BF16)


def _dense_attention(big):
    w = DA_PAIRS * LANES
    cb = D // w
    return pl.pallas_call(
        _attn_body,
        grid=(BATCH, cb),
        in_specs=[pl.BlockSpec((SEQ, w), lambda b, p: (b, 4 * cb + p)),
                  pl.BlockSpec((SEQ, w), lambda b, p: (b, 5 * cb + p)),
                  pl.BlockSpec((SEQ, w), lambda b, p: (b, 6 * cb + p))],
        out_specs=pl.BlockSpec((SEQ, w), lambda b, p: (b, p)),
        out_shape=jax.ShapeDtypeStruct((NP_TOK, D), BF16),
        compiler_params=_cp(("arbitrary", "arbitrary"), 32),
        name="dense_attn",
    )(big, big, big)


NA_DR2 = 2 * MAX_WIN_H - 2
NA_LOCAL = MAX_WIN_H * GRID_W


def _natten_body(q_ref, k_ref, v_ref, kc_ref, vc_ref, tab_ref, mask_ref, o_ref):
    rb = pl.program_id(2)
    kc = kc_ref[0, 0].astype(BF16)
    vc = vc_ref[0, 0].astype(BF16)
    colmask = mask_ref[...]
    first, second = _pair_masks()
    for qr in range(NA_QROWS):
        r = NA_QROWS * rb + qr
        rstart = jnp.clip(r - MAX_WIN_H // 2, 0, GRID_H - MAX_WIN_H)
        ks = pl.multiple_of(rstart * GRID_W, GRID_W)
        kl = k_ref[pl.ds(ks, NA_LOCAL), :]
        vl = v_ref[pl.ds(ks, NA_LOCAL), :]
        dr0 = rstart - r + MAX_WIN_H - 1
        qb = q_ref[qr * GRID_W:(qr + 1) * GRID_W, :] * jnp.asarray(N_HD ** -0.5, BF16)
        q2 = jnp.concatenate([jnp.where(first, qb, jnp.zeros_like(qb)),
                              jnp.where(second, qb, jnp.zeros_like(qb))], axis=0)
        bias = jnp.concatenate(
            [jnp.concatenate([tab_ref[par, dr0 + 2 * kp] for kp in range(MAX_WIN_H // 2)], axis=1) + colmask
             for par in range(2)], axis=0)
        s = _dot_nt(q2, jnp.concatenate([kl, kc], axis=0))
        sl = s[:, 0:NA_LOCAL] + bias
        sc = s[:, NA_LOCAL:NA_LOCAL + PAST_LEN]
        mx = jnp.maximum(jnp.max(sl, axis=-1, keepdims=True), jnp.max(sc, axis=-1, keepdims=True))
        el = jnp.exp(sl - mx)
        ec = jnp.exp(sc - mx)
        den = jnp.sum(el, axis=-1, keepdims=True) + jnp.sum(ec, axis=-1, keepdims=True)
        p = jnp.concatenate([el.astype(BF16), ec.astype(BF16)], axis=1)
        o2 = _dot(p, jnp.concatenate([vl, vc], axis=0)) / den
        o_ref[qr * GRID_W:(qr + 1) * GRID_W, :] = jnp.where(first, o2[0:GRID_W], o2[GRID_W:2 * GRID_W]).astype(BF16)


def _natten(big, cache_k, cache_v, bias_tab, window_mask, layer):
    cb = D // LANES
    nrb = GRID_H // NA_QROWS
    qrows = NA_QROWS * GRID_W
    q0 = NP_TOK // qrows
    b0 = NP_TOK // DEC_SEQ
    return pl.pallas_call(
        _natten_body,
        grid=(DEC_BATCH, N_HEADS // 2, nrb),
        in_specs=[pl.BlockSpec((qrows, LANES), lambda b, p, r: (q0 + b * nrb + r, 4 * cb + p)),
                  pl.BlockSpec((DEC_SEQ, LANES), lambda b, p, r: (b0 + b, 5 * cb + p)),
                  pl.BlockSpec((DEC_SEQ, LANES), lambda b, p, r: (b0 + b, 6 * cb + p)),
                  pl.BlockSpec((1, 1, PAST_LEN, LANES), lambda b, p, r: (b, layer, 0, p)),
                  pl.BlockSpec((1, 1, PAST_LEN, LANES), lambda b, p, r: (b, layer, 0, p)),
                  pl.BlockSpec((2, NA_DR2, GRID_W, 2 * GRID_W), lambda b, p, r: (p, 0, 0, 0)),
                  pl.BlockSpec((GRID_W, NA_LOCAL), lambda b, p, r: (0, 0))],
        out_specs=pl.BlockSpec((qrows, LANES), lambda b, p, r: (b * nrb + r, p)),
        out_shape=jax.ShapeDtypeStruct((NS_TOK, D), BF16),
        compiler_params=_cp(("arbitrary", "arbitrary", "arbitrary"), 40),
        name="natten",
    )(big, big, big, cache_k, cache_v, bias_tab, window_mask)


def _na_bias_table(rpb):
    n = GRID_W
    n_dr = 2 * MAX_WIN_H - 1
    padl = n - WIN_W
    row = jnp.pad(rpb.astype(F32), ((0, 0), (0, 0), (padl, 2 * n - 1 - padl - (2 * WIN_W - 1))), constant_values=NEG)
    flat = jnp.broadcast_to(row[:, :, None, :], (N_HEADS, n_dr, n, 2 * n - 1)).reshape(N_HEADS, n_dr, n * (2 * n - 1))
    toep = flat[:, :, n - 1:n - 1 + n * (2 * n - 2)].reshape(N_HEADS, n_dr, n, 2 * n - 2)[..., :n]
    return jnp.concatenate([toep[:, :-1], toep[:, 1:]], axis=-1)


def _na_window_mask():
    qcol = np.arange(GRID_W)[:, None]
    kcol = (np.arange(NA_LOCAL) % GRID_W)[None, :]
    cstart = np.clip(qcol - WIN_W // 2, 0, GRID_W - WIN_W)
    valid = (kcol >= cstart) & (kcol < cstart + WIN_W)
    return jnp.asarray(np.where(valid, 0.0, NEG).astype(np.float32))


TM_POST = 512


def _postmix_body(oap_ref, oas_ref, obp_ref, obs_ref, ga_ref, gb_ref, x_ref, g1_ref, nf_ref, sc2_ref, sh2_ref,
                  wa_ref, wb_ref, wo_ref, wr_ref, xo_ref, h2_ref, aff_ref, wab_ref, wbb_ref, wob_ref):
    i = pl.program_id(0)

    @pl.when(i == 0)
    def _():
        wab_ref[...] = wa_ref[0].astype(BF16)
        wbb_ref[...] = wb_ref[0].astype(BF16)
        wob_ref[...] = wo_ref[0].astype(BF16)

    prompt = i < NP_TOK // TM_POST
    a = _dot(jnp.where(prompt, oap_ref[...], oas_ref[...]), wab_ref[...])
    b = _dot(jnp.where(prompt, obp_ref[...], obs_ref[...]), wbb_ref[...])
    merged = _sigmoid(ga_ref[...].astype(F32)) * a + _sigmoid(gb_ref[...].astype(F32)) * b
    xn = x_ref[...] + g1_ref[0] * _dot(merged.astype(BF16), wob_ref[...])
    xo_ref[...] = xn
    h2 = (_rms(xn) * nf_ref[...]) * (1.0 + sc2_ref[0]) + sh2_ref[0]
    h2_ref[...] = h2
    h1, h2m, _ = _split3(h2)
    w1, w2, _ = _split3(wr_ref[...])
    lg = _dot_nt(w1, h1) + _dot_nt(w1, h2m) + _dot_nt(w2, h1)
    e = jnp.exp(lg - jnp.max(lg, axis=0, keepdims=True))
    aff_ref[...] = e / jnp.sum(e, axis=0, keepdims=True)


def _postmix(oa_p, oa_s, ob_p, ob_s, big, x, mod, norm_ffn, wa, wb, wo, wr_t, layer):
    tm = TM_POST
    np_tiles = NP_TOK // tm
    row = lambda i: (i, 0)
    const = lambda i: (0, 0)
    prow = lambda i: (jnp.minimum(i, np_tiles - 1), 0)
    srow = lambda i: (jnp.maximum(i - np_tiles, 0), 0)
    wspec = pl.BlockSpec((1, D, D), lambda i: (layer, 0, 0), pipeline_mode=pl.Buffered(1))
    return pl.pallas_call(
        _postmix_body,
        grid=(N_TOK // tm,),
        in_specs=[pl.BlockSpec((tm, D), prow), pl.BlockSpec((tm, D), srow),
                  pl.BlockSpec((tm, D), prow), pl.BlockSpec((tm, D), srow),
                  pl.BlockSpec((tm, D), lambda i: (i, 7)), pl.BlockSpec((tm, D), lambda i: (i, 8)),
                  pl.BlockSpec((tm, D), row),
                  _mod_spec(2, tm), pl.BlockSpec((1, D), const), _mod_spec(4, tm), _mod_spec(3, tm),
                  wspec, wspec, wspec,
                  pl.BlockSpec((N_EXPERTS, D), const)],
        out_specs=[pl.BlockSpec((tm, D), row), pl.BlockSpec((tm, D), row),
                   pl.BlockSpec((N_EXPERTS, tm), lambda i: (0, i))],
        out_shape=[jax.ShapeDtypeStruct((N_TOK, D), F32), jax.ShapeDtypeStruct((N_TOK, D), F32),
                   jax.ShapeDtypeStruct((N_EXPERTS, N_TOK), F32)],
        scratch_shapes=[pltpu.VMEM((D, D), BF16)] * 3,
        compiler_params=_cp(("arbitrary",), 60),
        name="postmix",
    )(oa_p, oa_s, ob_p, ob_s, big, big, x, mod, norm_ffn, mod, mod, wa, wb, wo, wr_t)


def _route_body(a_ref, idx_ref, gv_ref, off_ref, pos_ref, *, R, cap):
    a = a_ref[...]
    capf = float(cap)

    def count_ge(thr):
        c = jnp.sum(jnp.where(a >= thr, 1.0, 0.0), axis=1, keepdims=True)
        return jnp.sum(c, axis=2, keepdims=True)

    def search(i, lo_bits):
        cand = lo_bits | jnp.left_shift(jnp.int32(1), 30 - i)
        return jnp.where(count_ge(pltpu.bitcast(cand, F32)) >= capf, cand, lo_bits)

    tau_bits = lax.fori_loop(0, 31, search, jnp.zeros((N_EXPERTS, 1, 1), I32))
    lo = pltpu.bitcast(tau_bits, F32)
    hi = pltpu.bitcast(jnp.maximum(tau_bits + 1, jnp.int32(MIN_NORMAL_BITS)), F32)

    def refine(i, lh):
        lo, hi = lh
        mid = lo + (hi - lo) * 0.5
        ok = count_ge(mid) >= capf
        return jnp.where(ok, mid, lo), jnp.where(ok, hi, mid)

    lo, hi = lax.fori_loop(0, 32, refine, (lo, hi))
    above = jnp.where(a >= hi, 1.0, 0.0)
    ties = jnp.where(a >= lo, 1.0, 0.0) - above
    need = capf - jnp.sum(jnp.sum(above, axis=1, keepdims=True), axis=2, keepdims=True)

    l0 = lax.broadcasted_iota(I32, (LANES, LANES), 0)
    l1 = lax.broadcasted_iota(I32, (LANES, LANES), 1)
    upper = jnp.where(l0 <= l1, 1.0, 0.0).astype(BF16)
    r0 = lax.broadcasted_iota(I32, (R, R), 0)
    r1 = lax.broadcasted_iota(I32, (R, R), 1)
    below = jnp.where(r1 < r0, 1.0, 0.0).astype(BF16)
    eye_r = r0 == r1
    r_row = lax.broadcasted_iota(I32, (1, R), 1).astype(F32)
    lane_row = lax.broadcasted_iota(I32, (1, LANES), 1).astype(F32)
    jcol = lax.broadcasted_iota(I32, (cap, 1), 0).astype(F32)

    def prefix(x):
        within = _dot(x.astype(BF16), upper)
        tot = jnp.broadcast_to(within[:, LANES - 1:LANES], (R, LANES))
        offs = _dot(below, tot.astype(BF16))
        return within + offs, offs

    for e in range(N_EXPERTS):
        eq = ties[e]
        cin_eq, _ = prefix(eq)
        sel = above[e] + eq * jnp.where(cin_eq - eq < need[e], 1.0, 0.0)
        cin, offs = prefix(sel)
        cend_row = jnp.sum(jnp.where(eye_r, cin[:, LANES - 1:LANES], 0.0), axis=0, keepdims=True)
        bcol = jnp.sum(jnp.where(cend_row <= jcol, 1.0, 0.0), axis=1, keepdims=True)
        onehot = jnp.where(bcol == r_row, 1.0, 0.0).astype(BF16)
        lcol = jnp.sum(jnp.where(_dot_x3(onehot, cin) <= jcol, 1.0, 0.0), axis=1, keepdims=True)
        idx_ref[e] = (bcol * LANES + lcol).astype(I32)
        gv_ref[e] = jnp.sum(jnp.where(lane_row == lcol, _dot_x3(onehot, a[e]), 0.0), axis=1, keepdims=True)
        off_ref[e] = offs[:, 0:1]
        pos_ref[e] = jnp.where(sel > 0.0, cin - 1.0, -1.0)


def _route(aff3, cap):
    R = aff3.shape[1]
    full = lambda s: pl.BlockSpec(s, lambda i: (0, 0, 0))
    return pl.pallas_call(
        functools.partial(_route_body, R=R, cap=cap),
        grid=(1,),
        in_specs=[full((N_EXPERTS, R, LANES))],
        out_specs=[full((N_EXPERTS, cap, 1)), full((N_EXPERTS, cap, 1)), full((N_EXPERTS, R, 1)),
                   full((N_EXPERTS, R, LANES))],
        out_shape=[jax.ShapeDtypeStruct((N_EXPERTS, cap, 1), I32),
                   jax.ShapeDtypeStruct((N_EXPERTS, cap, 1), F32),
                   jax.ShapeDtypeStruct((N_EXPERTS, R, 1), F32),
                   jax.ShapeDtypeStruct((N_EXPERTS, R, LANES), F32)],
        compiler_params=_cp(("arbitrary",), 48),
        name="route",
    )(aff3)


GATHER_UNROLL = 8
N_FF = EXPERT_FF // FF_CHUNK
ROWS_PER_STEP = CAP_TOT // N_FF


def _gather_copy(h_hbm, xs_ref, sem, slot, src_row, dst_row, nrows):
    return pltpu.make_async_copy(h_hbm.at[pl.ds(src_row, nrows), :], xs_ref.at[slot, pl.ds(dst_row, nrows), :],
                                 sem.at[slot])


def _slab_copy(y_ref, ys_hbm, sem, e, s):
    return pltpu.make_async_copy(y_ref.at[:, s * LANES:(s + 1) * LANES], ys_hbm.at[e, :, s, :], sem)


def _ffn_body(idx_ref, h_hbm, wg_ref, wu_ref, wd_ref, gv_ref, ys_hbm, xs_ref, xb_ref, y_ref, gsem, osem):
    e = pl.program_id(0)
    f = pl.program_id(1)
    n_e = pl.num_programs(0)
    slot = e % 2

    @pl.when((e == 0) & (f == 0))
    def _():
        def issue(j, carry):
            _gather_copy(h_hbm, xs_ref, gsem, 0, idx_ref[j], j, 1).start()
            return carry
        lax.fori_loop(0, CAP_TOT, issue, 0, unroll=GATHER_UNROLL)

    @pl.when(f == 0)
    def _():
        @pl.when(e > 0)
        def _():
            for s in range(SLAB):
                _slab_copy(y_ref, ys_hbm, osem, e - 1, s).wait()

        _gather_copy(h_hbm, xs_ref, gsem, slot, 0, 0, CAP_TOT).wait()
        xb_ref[...] = xs_ref[slot].astype(BF16)

    nxt = (e + 1) % n_e
    for j in range(ROWS_PER_STEP):
        r = f * ROWS_PER_STEP + j
        _gather_copy(h_hbm, xs_ref, gsem, 1 - slot, idx_ref[nxt * CAP_TOT + r], r, 1).start()

    xb = xb_ref[...]
    g = _dot(xb, wg_ref[0, 0].astype(BF16))
    u = _dot(xb, wu_ref[0, 0].astype(BF16))
    hid = (g * _sigmoid(g) * u).astype(BF16)
    y = _dot(hid, wd_ref[0, 0].astype(BF16))

    @pl.when(f == 0)
    def _():
        y_ref[...] = y

    @pl.when(f > 0)
    def _():
        y_ref[...] = y_ref[...] + y

    @pl.when(f == pl.num_programs(1) - 1)
    def _():
        y_ref[...] = y_ref[...] * gv_ref[0]
        for s in range(SLAB):
            _slab_copy(y_ref, ys_hbm, osem, e, s).start()

        @pl.when(e == n_e - 1)
        def _():
            for s in range(SLAB):
                _slab_copy(y_ref, ys_hbm, osem, e, s).wait()
            _gather_copy(h_hbm, xs_ref, gsem, 1 - slot, 0, 0, CAP_TOT).wait()


def _ffn(idx_flat, h2, wg, wu, wd, gv, layer):
    return pl.pallas_call(
        _ffn_body,
        grid_spec=pltpu.PrefetchScalarGridSpec(
            num_scalar_prefetch=1,
            grid=(N_EXPERTS, N_FF),
            in_specs=[pl.BlockSpec(memory_space=pl.ANY),
                      pl.BlockSpec((1, 1, D, FF_CHUNK), lambda e, f, idx: (layer, e, 0, f)),
                      pl.BlockSpec((1, 1, D, FF_CHUNK), lambda e, f, idx: (layer, e, 0, f)),
                      pl.BlockSpec((1, 1, FF_CHUNK, D), lambda e, f, idx: (layer, e, f, 0)),
                      pl.BlockSpec((1, CAP_TOT, 1), lambda e, f, idx: (e, 0, 0))],
            out_specs=pl.BlockSpec(memory_space=pl.ANY),
            scratch_shapes=[pltpu.VMEM((2, CAP_TOT, D), F32), pltpu.VMEM((CAP_TOT, D), BF16),
                            pltpu.VMEM((CAP_TOT, D), F32),
                            pltpu.SemaphoreType.DMA((2,)), pltpu.SemaphoreType.DMA(())]),
        out_shape=jax.ShapeDtypeStruct((N_EXPERTS, CAP_TOT, SLAB, LANES), F32),
        compiler_params=_cp(("arbitrary", "arbitrary"), 60),
        name="expert_ffn",
    )(idx_flat, h2, wg, wu, wd, gv)


MAX_PAIRS = N_EXPERTS * TM
SEG_BITS = TM.bit_length()


def _seg_copy(ys_hbm, w_ref, sem, slot, e, src_row, dst_row, nrows):
    src = pl.multiple_of(src_row * SLAB, SLAB)
    dst = pl.multiple_of(dst_row * SLAB, SLAB)
    return pltpu.make_async_copy(ys_hbm.at[e, pl.ds(src, nrows * SLAB), :],
                                 w_ref.at[slot, pl.ds(dst, nrows * SLAB), :], sem.at[slot])


def _combine_body(lo_ref, hi_ref, ys_hbm, pos_ref, x_ref, g2_ref, nf_ref, o_ref, w_ref, acc_ref, oh_ref, sem, *, last):
    b = pl.program_id(0)
    lane = lax.broadcasted_iota(I32, (1, LANES), 1)

    def layout(t):
        los, segs = [], []
        off = jnp.int32(0)
        for e in range(N_EXPERTS):
            lo = lo_ref[e * N_TILES + t]
            n = hi_ref[e * N_TILES + t] - lo
            los.append(lo)
            segs.append((off, off + n))
            off = off + n
        return los, segs

    def fetch(t, slot):
        los, segs = layout(t)
        for e in range(N_EXPERTS):
            n = segs[e][1] - segs[e][0]
            for bit in range(SEG_BITS - 1, -1, -1):
                done = (n >> (bit + 1)) << (bit + 1)

                @pl.when((n & (1 << bit)) != 0)
                def _(e=e, done=done, bit=bit):
                    _seg_copy(ys_hbm, w_ref, sem, slot, e, los[e] + done, segs[e][0] + done, 1 << bit).start()

    slot = b % 2

    @pl.when(b == 0)
    def _():
        fetch(b, slot)

    @pl.when(b + 1 < pl.num_programs(0))
    def _():
        fetch(b + 1, 1 - slot)

    los, seg = layout(b)
    npairs = seg[N_EXPERTS - 1][1]
    for bit in range(MAX_PAIRS.bit_length() - 1, -1, -1):
        @pl.when((npairs & (1 << bit)) != 0)
        def _(bit=bit):
            _seg_copy(ys_hbm, w_ref, sem, slot, 0, 0, 0, 1 << bit).wait()

    shift = jnp.zeros((1, LANES), F32)
    for e in range(N_EXPERTS):
        shift = jnp.where(lane == e, (seg[e][0] - los[e]).astype(F32), shift)
    acc_ref[...] = jnp.zeros_like(acc_ref)
    pos = pos_ref[...]
    prow = jnp.where(pos >= 0.0, pos + shift[:, 0:N_EXPERTS], -1.0)
    sub = lax.broadcasted_iota(I32, (TM, 1), 0)
    pair = lax.broadcasted_iota(I32, (1, TM), 1).astype(F32)

    def body(c, carry):
        valid = (c * TM + sub) < npairs
        local = prow - (c * TM).astype(F32)
        oh_ref[...] = jnp.zeros_like(oh_ref)
        for e in range(N_EXPERTS):
            @pl.when((seg[e][0] < (c + 1) * TM) & (seg[e][1] > c * TM))
            def _(e=e):
                oh_ref[...] += jnp.where(local[:, e:e + 1] == pair, 1.0, 0.0)
        onehot = oh_ref[...].astype(BF16)
        data = jnp.concatenate([w_ref[slot, pl.ds(c * TM * SLAB + s, TM, stride=SLAB), :] for s in range(SLAB)],
                               axis=1)
        hi, mid, _ = _split3(jnp.where(valid, data, 0.0))
        acc_ref[...] += _dot(onehot, hi) + _dot(onehot, mid)
        return carry

    lax.fori_loop(0, (npairs + TM - 1) // TM, body, 0)
    xn = x_ref[...] + g2_ref[0] * acc_ref[...]
    if last:
        xn = _rms(xn) * nf_ref[...]
    o_ref[...] = xn


def _combine(seg_lo, seg_hi, ys, pos_t, x, mod, norm_final, *, last):
    return pl.pallas_call(
        functools.partial(_combine_body, last=last),
        grid_spec=pltpu.PrefetchScalarGridSpec(
            num_scalar_prefetch=2,
            grid=(N_TILES,),
            in_specs=[pl.BlockSpec(memory_space=pl.ANY),
                      pl.BlockSpec((TM, N_EXPERTS), lambda i, lo, hi: (i, 0)),
                      pl.BlockSpec((TM, D), lambda i, lo, hi: (i, 0)),
                      pl.BlockSpec((1, 1, D), lambda i, lo, hi: (_mod_row(i), 0, 5)),
                      pl.BlockSpec((1, D), lambda i, lo, hi: (0, 0))],
            out_specs=pl.BlockSpec((TM, D), lambda i, lo, hi: (i, 0)),
            scratch_shapes=[pltpu.VMEM((2, MAX_PAIRS * SLAB, LANES), F32), pltpu.VMEM((TM, D), F32),
                            pltpu.VMEM((TM, TM), F32), pltpu.SemaphoreType.DMA((2,))]),
        out_shape=jax.ShapeDtypeStruct((N_TOK, D), F32),
        compiler_params=_cp(("arbitrary",), 56),
        name="combine",
    )(seg_lo, seg_hi, ys, pos_t, x, mod, norm_final)


def _tile_segments(off_p, off_s):
    per = TM // LANES
    lo_p = off_p[:, ::per, 0].astype(I32)
    lo_s = off_s[:, ::per, 0].astype(I32) + CAP_P
    hi_p = jnp.concatenate([lo_p[:, 1:], jnp.full((N_EXPERTS, 1), CAP_P, I32)], axis=1)
    hi_s = jnp.concatenate([lo_s[:, 1:], jnp.full((N_EXPERTS, 1), CAP_TOT, I32)], axis=1)
    lo = jnp.concatenate([lo_p, lo_s], axis=1)
    hi = jnp.concatenate([hi_p, hi_s], axis=1)
    return lo.reshape(-1), hi.reshape(-1)


def kernel(x_prompt, x_sample, cache_na_k, cache_na_v, state_mlstm_C, state_mlstm_n, state_mlstm_m, c, c_ctx, w_ada, b_ada, norm_mix, norm_ffn, w_in, conv_qk, mlstm_gate_bias, mlstm_head_norm, na_rpb, w_branch_a, w_branch_b, w_out, w_router, w_expert_gate, w_expert_up, w_expert_down, norm_final):
    depth = w_in.shape[0]
    m_width = M_HEADS * M_HD
    n_gates = 4 * M_HEADS

    x = jnp.concatenate([x_prompt.reshape(NP_TOK, D), x_sample.reshape(NS_TOK, D)], axis=0)
    cond8 = jnp.concatenate([c_ctx[None], c, jnp.zeros((8 - 1 - DEC_BATCH, D), F32)], axis=0)
    mods = _adaln(cond8, w_ada, b_ada).reshape(depth, 8, 1, 6 * D)
    cos_t, sin_t = _rope_tables()
    cache_k = cache_na_k.reshape(DEC_BATCH, depth, PAST_LEN, D)
    cache_v = cache_na_v.reshape(DEC_BATCH, depth, PAST_LEN, D)
    norm_final2 = norm_final.reshape(1, D)
    window_mask = _na_window_mask()

    ks_, vs_, cs_, ns_, ms_ = [], [], [], [], []
    for l in range(depth):
        mod = mods[l]
        w_big, w_gates = _pack_w_in(w_in, l)
        gate_bias = jnp.pad(mlstm_gate_bias[l], (0, LANES - n_gates)).reshape(1, LANES)
        head_norm = mlstm_head_norm[l].reshape(1, m_width)

        big, gates, kv = _inproj(x, norm_mix[l].reshape(1, D), mod, w_big, w_gates)

        oa_p, c_new, n_new, m_new = _mlstm(big, gates, gate_bias, head_norm, conv_qk[l], None, None,
                                           nb=BATCH, T=SEQ, row0=0, hb=M_HEADS, emit_state=True)
        state = (state_mlstm_C[:, l], state_mlstm_n[:, l].reshape(DEC_BATCH, 2, M_HEADS, 1, M_HD),
                 state_mlstm_m[:, l].reshape(DEC_BATCH, 2, M_HEADS, 1, 1))
        (oa_s,) = _mlstm(big, gates, gate_bias, head_norm, conv_qk[l], (cos_t, sin_t), state,
                         nb=DEC_BATCH, T=DEC_SEQ, row0=NP_TOK // DEC_SEQ, hb=M_HEADS // 2, emit_state=False)

        ob_p = _dense_attention(big)
        ob_s = _natten(big, cache_k, cache_v, _na_bias_table(na_rpb[l]), window_mask, l)

        x, h2, aff_t = _postmix(oa_p, oa_s, ob_p, ob_s, big, x, mod, norm_ffn[l].reshape(1, D),
                                w_branch_a, w_branch_b, w_out, w_router[l].T, l)

        idx_p, gv_p, off_p, pos_p = _route(aff_t[:, :NP_TOK].reshape(N_EXPERTS, NP_TOK // LANES, LANES), CAP_P)
        idx_s, gv_s, off_s, pos_s = _route(aff_t[:, NP_TOK:].reshape(N_EXPERTS, NS_TOK // LANES, LANES), CAP_S)
        idx = jnp.concatenate([idx_p, idx_s + NP_TOK], axis=1)
        gv = jnp.concatenate([gv_p, gv_s], axis=1)
        pos_s = jnp.where(pos_s >= 0.0, pos_s + CAP_P, pos_s)
        pos_t = jnp.concatenate([pos_p.reshape(N_EXPERTS, NP_TOK), pos_s.reshape(N_EXPERTS, NS_TOK)], axis=1).T
        ys = _ffn(idx.reshape(-1), h2, w_expert_gate, w_expert_up, w_expert_down, gv, l)
        ys = ys.reshape(N_EXPERTS, CAP_TOT * SLAB, LANES)
        seg_lo, seg_hi = _tile_segments(off_p, off_s)
        x = _combine(seg_lo, seg_hi, ys, pos_t, x, mod, norm_final2, last=(l == depth - 1))

        ks_.append(kv[:NP_TOK, :D].reshape(BATCH, SEQ, N_HEADS, N_HD))
        vs_.append(kv[:NP_TOK, D:].reshape(BATCH, SEQ, N_HEADS, N_HD))
        cs_.append(c_new)
        ns_.append(n_new.reshape(BATCH, 2, M_HEADS, M_HD))
        ms_.append(m_new.reshape(BATCH, 2, M_HEADS))

    y_prompt = x[:NP_TOK].reshape(BATCH, SEQ, D)
    y_sample = x[NP_TOK:].reshape(DEC_BATCH, DEC_SEQ, D)
    return (y_prompt, y_sample, jnp.stack(ks_, axis=1), jnp.stack(vs_, axis=1),
            jnp.stack(cs_, axis=1), jnp.stack(ns_, axis=1), jnp.stack(ms_, axis=1))
```

```python
import functools

import numpy as np
import jax
import jax.numpy as jnp
from jax import lax
from jax.experimental import pallas as pl
from jax.experimental.pallas import tpu as pltpu

F32 = jnp.float32
BF16 = jnp.bfloat16
I32 = jnp.int32

D = 1024
BATCH, SEQ = 16, 256
DEC_BATCH, DEC_SEQ = 4, 2048
PAST_LEN = 512
NP_TOK = BATCH * SEQ
NS_TOK = DEC_BATCH * DEC_SEQ
N_TOK = NP_TOK + NS_TOK
GRID_W = 64
GRID_H = DEC_SEQ // GRID_W
M_HEADS, M_HD = 4, 256
CHUNK = 128
N_HEADS, N_HD = 16, 64
MAX_WIN_H, WIN_W = 8, 16
N_EXPERTS, EXPERT_FF = 16, 2048
CAP_P = 2 * NP_TOK // N_EXPERTS
CAP_S = 2 * NS_TOK // N_EXPERTS
CAP_TOT = CAP_P + CAP_S
ROPE_BASE = 10000.0
EPS = 1e-6
NEG = -1e30
MIN_NORMAL_BITS = 0x00800000

TM = 256
N_TILES = N_TOK // TM
NP_TILES = NP_TOK // TM
LANES = 128
NA_QROWS = 8
FF_CHUNK = 512
SLAB = D // LANES


def _cp(sem, vmem_mb):
    return pltpu.CompilerParams(dimension_semantics=sem, vmem_limit_bytes=vmem_mb * 2 ** 20)


def _dot(a, b):
    return jnp.dot(a, b, preferred_element_type=F32)


def _dot_nt(a, b):
    return lax.dot_general(a, b, (((1,), (1,)), ((), ())), preferred_element_type=F32)


def _dot_tn(a, b):
    return lax.dot_general(a, b, (((0,), (0,)), ((), ())), preferred_element_type=F32)


def _split3(x):
    hi = x.astype(BF16)
    r = x - hi.astype(F32)
    mid = r.astype(BF16)
    lo = (r - mid.astype(F32)).astype(BF16)
    return hi, mid, lo


def _dot_x3(a_bf, x):
    hi, mid, lo = _split3(x)
    return _dot(a_bf, hi) + _dot(a_bf, mid) + _dot(a_bf, lo)


def _sigmoid(x):
    return 1.0 / (1.0 + jnp.exp(-x))


def _rms(x):
    return x * lax.rsqrt(jnp.mean(x * x, axis=-1, keepdims=True) + EPS)


def _mod_row(i, tm=TM):
    return jnp.where(i < NP_TOK // tm, 0, 1 + (i - NP_TOK // tm) // (DEC_SEQ // tm))


def _mod_spec(kind, tm=TM):
    return pl.BlockSpec((1, 1, D), lambda i: (_mod_row(i, tm), 0, kind))


def _adaln_body(c_ref, w_ref, b_ref, o_ref):
    c = c_ref[...]
    s = (c * _sigmoid(c)).astype(BF16)
    o_ref[0] = _dot(s, w_ref[0].astype(BF16)) + b_ref[0]


def _adaln(cond8, w_ada, b_ada):
    depth = w_ada.shape[0]
    tn = 1536
    return pl.pallas_call(
        _adaln_body,
        grid=(depth, 6 * D // tn),
        in_specs=[pl.BlockSpec((8, D), lambda l, j: (0, 0)),
                  pl.BlockSpec((1, D, tn), lambda l, j: (l, 0, j)),
                  pl.BlockSpec((1, 1, tn), lambda l, j: (l, 0, j))],
        out_specs=pl.BlockSpec((1, 8, tn), lambda l, j: (l, 0, j)),
        out_shape=jax.ShapeDtypeStruct((depth, 8, 6 * D), F32),
        compiler_params=_cp(("arbitrary", "arbitrary"), 40),
        name="adaln",
    )(cond8, w_ada, b_ada.reshape(depth, 1, 6 * D))


N_GATES = 4 * M_HEADS
GATE_COL = 4 * M_HEADS * M_HD


def _pack_body(w_ref, big_ref, gates_ref):
    w = w_ref[0]
    big_ref[:, 0:GATE_COL] = w[:, 0:GATE_COL].astype(BF16)
    big_ref[:, GATE_COL:9 * D] = w[:, GATE_COL + N_GATES:9 * D + N_GATES].astype(BF16)
    lane = lax.broadcasted_iota(I32, (1, LANES), 1)
    gates_ref[...] = jnp.where(lane < N_GATES, w[:, GATE_COL:GATE_COL + LANES], 0.0).astype(BF16)


def _pack_w_in(w, layer):
    tr = 128
    return pl.pallas_call(
        _pack_body,
        grid=(D // tr,),
        in_specs=[pl.BlockSpec((1, tr, w.shape[2]), lambda i: (layer, i, 0))],
        out_specs=[pl.BlockSpec((tr, 9 * D), lambda i: (i, 0)), pl.BlockSpec((tr, LANES), lambda i: (i, 0))],
        out_shape=[jax.ShapeDtypeStruct((D, 9 * D), BF16), jax.ShapeDtypeStruct((D, LANES), BF16)],
        compiler_params=_cp(("arbitrary",), 48),
        name="pack_w_in",
    )(w)


def _inproj_body(x_ref, g_ref, sc_ref, sh_ref, w_ref, wg_ref, big_ref, gates_ref, kv_ref):
    i = pl.program_id(0)
    h = (_rms(x_ref[...]) * g_ref[...]) * (1.0 + sc_ref[0]) + sh_ref[0]
    hb = h.astype(BF16)
    gates_ref[...] = _dot(hb, wg_ref[...])
    for c in range(9):
        r = _dot(hb, w_ref[:, c * D:(c + 1) * D])
        big_ref[:, c * D:(c + 1) * D] = r.astype(BF16)
        if c in (5, 6):
            @pl.when(i < NP_TOK // TM_IN)
            def _():
                kv_ref[:, (c - 5) * D:(c - 4) * D] = r

    @pl.when(i == NP_TOK // TM_IN)
    def _():
        kv_ref[...] = jnp.zeros_like(kv_ref)


TM_IN = 512


def _inproj(x, norm_g, mod, w_big, w_gates):
    np_tiles = NP_TOK // TM_IN
    return pl.pallas_call(
        _inproj_body,
        grid=(N_TOK // TM_IN,),
        in_specs=[pl.BlockSpec((TM_IN, D), lambda i: (i, 0)),
                  pl.BlockSpec((1, D), lambda i: (0, 0)),
                  _mod_spec(1, TM_IN), _mod_spec(0, TM_IN),
                  pl.BlockSpec((D, 9 * D), lambda i: (0, 0), pipeline_mode=pl.Buffered(1)),
                  pl.BlockSpec((D, LANES), lambda i: (0, 0))],
        out_specs=[pl.BlockSpec((TM_IN, 9 * D), lambda i: (i, 0)),
                   pl.BlockSpec((TM_IN, LANES), lambda i: (i, 0)),
                   pl.BlockSpec((TM_IN, 2 * D), lambda i: (jnp.minimum(i, np_tiles), 0))],
        out_shape=[jax.ShapeDtypeStruct((N_TOK, 9 * D), BF16),
                   jax.ShapeDtypeStruct((N_TOK, LANES), F32),
                   jax.ShapeDtypeStruct((NP_TOK + TM_IN, 2 * D), F32)],
        compiler_params=_cp(("arbitrary",), 60),
        name="inproj",
    )(x, norm_g, mod, mod, w_big, w_gates)


def _conv_act(x, w, cos, sin, scale):
    T = x.shape[0]
    row = lax.broadcasted_iota(I32, (T, 1), 0)
    xp = jnp.where(row == 0, 0.0, pltpu.roll(x, 1, 0))
    xn = jnp.where(row == T - 1, 0.0, pltpu.roll(x, T - 1, 0))
    y = xp * w[0:1] + x * w[1:2] + xn * w[2:3]
    y = y * _sigmoid(y)
    halves = []
    for hlf in range(2):
        sl = slice(hlf * LANES, (hlf + 1) * LANES)
        yh = y[:, sl]
        if cos is not None:
            yh = yh * cos[:, sl] + pltpu.roll(yh, LANES // 2, 1) * sin[:, sl]
        halves.append((yh * scale).astype(BF16))
    return jnp.concatenate(halves, axis=1)


def _rope_tables():
    pos = np.arange(DEC_SEQ)
    rows = (pos // GRID_W).astype(np.float32)
    cols = (pos % GRID_W).astype(np.float32)
    nfreq = M_HD // 4
    inv = (ROPE_BASE ** (-np.arange(nfreq, dtype=np.float32) / nfreq)).astype(np.float32)
    d = np.arange(M_HD)
    p = np.where(d[None, :] < M_HD // 2, rows[:, None], cols[:, None]).astype(np.float32)
    ang = (p * inv[d % nfreq][None, :]).astype(np.float32)
    sign = np.where((d % (M_HD // 2)) < nfreq, -1.0, 1.0).astype(np.float32)
    return jnp.asarray(np.cos(ang), F32), jnp.asarray(np.sin(ang) * sign[None, :], F32)


def _mlstm_body(*refs, T, hb, rope, has_state, emit_state):
    qraw_ref, kraw_ref, v_ref, om_ref, g_ref, gb_ref, hn_ref, cwq_ref, cwk_ref = refs[:9]
    pos = 9
    if rope:
        cos_ref, sin_ref = refs[pos:pos + 2]
        pos += 2
    if has_state:
        c0_ref, n0_ref, m0_ref = refs[pos:pos + 3]
        pos += 3
    oa_ref = refs[pos]
    pos += 1
    if emit_state:
        co_ref, no_ref, mo_ref = refs[pos:pos + 3]
        pos += 3
    nch = 2 * hb
    hs_refs = refs[pos:pos + nch]
    c_refs = refs[pos + nch:pos + 2 * nch]
    n_refs = refs[pos + 2 * nch:pos + 3 * nch]
    q_refs = refs[pos + 3 * nch:pos + 3 * nch + hb]
    k_refs = refs[pos + 3 * nch + hb:pos + 3 * nch + 2 * hb]

    cos = cos_ref[...] if rope else None
    sin = sin_ref[...] if rope else None
    for hh in range(hb):
        cols = slice(hh * M_HD, (hh + 1) * M_HD)
        q_refs[hh][...] = _conv_act(qraw_ref[:, cols].astype(F32), cwq_ref[:, cols], cos, sin, 1.0)
        k_refs[hh][...] = _conv_act(kraw_ref[:, cols].astype(F32), cwk_ref[:, cols], cos, sin, M_HD ** -0.5)

    head0 = pl.program_id(1) * hb
    nc = T // CHUNK
    lane = lax.broadcasted_iota(I32, (1, LANES), 1)
    r_i = lax.broadcasted_iota(I32, (CHUNK, CHUNK), 0)
    c_i = lax.broadcasted_iota(I32, (CHUNK, CHUNK), 1)
    eye = r_i == c_i
    causal = (r_i >= c_i, r_i <= c_i)
    cum = tuple(jnp.where(m, 1.0, 0.0).astype(BF16) for m in causal)

    def to_row(col):
        return jnp.sum(jnp.where(eye, col, 0.0), axis=0, keepdims=True)

    def pick(mat, colidx):
        return jnp.sum(jnp.where(lane == colidx, mat, 0.0), axis=1, keepdims=True)

    def dir_step(d, c, m_prevs):
        rows = pl.ds(pl.multiple_of(c * CHUNK, CHUNK), CHUNK)
        g = g_ref[rows, :] + gb_ref[...]
        lf = jnp.minimum(g, 0.0) - jnp.log(1.0 + jnp.exp(-jnp.abs(g)))
        bmat = _dot_x3(cum[d], lf)
        m_news = []
        for hh in range(hb):
            ch = hh * 2 + d
            hs_ref, c_ref, n_ref = hs_refs[ch], c_refs[ch], n_refs[ch]
            cols = slice(hh * M_HD, (hh + 1) * M_HD)
            m_prev = m_prevs[hh]
            ig_col = pick(g, d * M_HEADS + head0 + hh)
            b_col = pick(bmat, 2 * M_HEADS + d * M_HEADS + head0 + hh)
            ig_row = to_row(ig_col)
            b_row = to_row(b_col)
            b_last = b_row[:, CHUNK - 1:CHUNK] if d == 0 else b_row[:, 0:1]
            logd = jnp.where(causal[d], b_col - b_row + ig_row, -jnp.inf)
            inter = b_col + m_prev
            m_row = jnp.maximum(inter, jnp.max(logd, axis=1, keepdims=True))
            dmat = jnp.exp(logd - m_row)
            s_inter = jnp.exp(inter - m_row)
            q = q_refs[hh][rows, :]
            k = k_refs[hh][rows, :]
            v = v_ref[rows, cols]
            s = _dot_nt(q, k) * dmat
            num = _dot(s.astype(BF16), v) + s_inter * _dot(q, c_ref[...].astype(BF16))
            den = (jnp.sum(s, axis=1, keepdims=True)
                   + s_inter * jnp.sum(q.astype(F32) * n_ref[...], axis=1, keepdims=True))
            hs_ref[rows, :] = num / jnp.maximum(jnp.abs(den), jnp.exp(-m_row))
            log_w = b_last - b_col + ig_col
            m_new = jnp.maximum(b_last + m_prev, jnp.max(log_w, axis=0, keepdims=True))
            w = jnp.exp(log_w - m_new)
            decay = jnp.exp(b_last + m_prev - m_new)
            kw = k.astype(F32) * w
            c_ref[...] = decay * c_ref[...] + _dot_tn(kw.astype(BF16), v)
            n_ref[...] = decay * n_ref[...] + jnp.sum(kw, axis=0, keepdims=True)
            m_news.append(m_new)
        return tuple(m_news)

    m_init = ([], [])
    for hh in range(hb):
        for d in range(2):
            ch = hh * 2 + d
            if has_state:
                c_refs[ch][...] = c0_ref[0, d, hh]
                n_refs[ch][...] = n0_ref[0, d, hh]
                m_init[d].append(m0_ref[0, d, hh])
            else:
                c_refs[ch][...] = jnp.zeros_like(c_refs[ch])
                n_refs[ch][...] = jnp.zeros_like(n_refs[ch])
                m_init[d].append(jnp.zeros((1, 1), F32))

    def body(ci, ms):
        return dir_step(0, ci, ms[0]), dir_step(1, nc - 1 - ci, ms[1])

    m_fin = lax.fori_loop(0, nc, body, (tuple(m_init[0]), tuple(m_init[1])))
    for hh in range(hb):
        cols = slice(hh * M_HD, (hh + 1) * M_HD)
        if emit_state:
            for d in range(2):
                co_ref[0, d, hh] = c_refs[hh * 2 + d][...]
                no_ref[0, d, hh] = n_refs[hh * 2 + d][...]
                mo_ref[0, d, hh] = m_fin[d][hh]
        hm = _rms(hs_refs[hh * 2][...] + hs_refs[hh * 2 + 1][...]) * hn_ref[:, cols]
        oa_ref[:, cols] = (_sigmoid(om_ref[:, cols].astype(F32)) * hm).astype(BF16)


def _mlstm(big, gates, gate_bias, head_norm, conv_w, rope_tabs, state, *, nb, T, row0, hb, emit_state):
    has_state = state is not None
    rope = rope_tabs is not None
    w = hb * M_HD
    npair = M_HEADS // hb
    in_specs = [pl.BlockSpec((T, w), lambda b, h: (row0 + b, h)),
                pl.BlockSpec((T, w), lambda b, h: (row0 + b, npair + h)),
                pl.BlockSpec((T, w), lambda b, h: (row0 + b, 2 * npair + h)),
                pl.BlockSpec((T, w), lambda b, h: (row0 + b, 3 * npair + h)),
                pl.BlockSpec((T, LANES), lambda b, h: (row0 + b, 0)),
                pl.BlockSpec((1, LANES), lambda b, h: (0, 0)),
                pl.BlockSpec((1, w), lambda b, h: (0, h)),
                pl.BlockSpec((3, w), lambda b, h: (0, h)),
                pl.BlockSpec((3, w), lambda b, h: (0, npair + h))]
    args = [big, big, big, big, gates, gate_bias, head_norm, conv_w, conv_w]
    if rope:
        in_specs += [pl.BlockSpec((T, M_HD), lambda b, h: (0, 0))] * 2
        args += list(rope_tabs)
    state_specs = [pl.BlockSpec((1, 2, hb, M_HD, M_HD), lambda b, h: (b, 0, h, 0, 0)),
                   pl.BlockSpec((1, 2, hb, 1, M_HD), lambda b, h: (b, 0, h, 0, 0)),
                   pl.BlockSpec((1, 2, hb, 1, 1), lambda b, h: (b, 0, h, 0, 0))]
    if has_state:
        in_specs += state_specs
        args += list(state)
    out_specs = [pl.BlockSpec((T, w), lambda b, h: (b, h))]
    out_shape = [jax.ShapeDtypeStruct((nb * T, M_HEADS * M_HD), BF16)]
    if emit_state:
        out_specs += state_specs
        out_shape += [jax.ShapeDtypeStruct((nb, 2, M_HEADS, M_HD, M_HD), F32),
                      jax.ShapeDtypeStruct((nb, 2, M_HEADS, 1, M_HD), F32),
                      jax.ShapeDtypeStruct((nb, 2, M_HEADS, 1, 1), F32)]
    nch = 2 * hb
    return pl.pallas_call(
        functools.partial(_mlstm_body, T=T, hb=hb, rope=rope, has_state=has_state, emit_state=emit_state),
        grid=(nb, npair),
        in_specs=in_specs,
        out_specs=out_specs,
        out_shape=out_shape,
        scratch_shapes=[pltpu.VMEM((T, M_HD), F32)] * nch + [pltpu.VMEM((M_HD, M_HD), F32)] * nch
        + [pltpu.VMEM((1, M_HD), F32)] * nch + [pltpu.VMEM((T, M_HD), BF16)] * (2 * hb),
        compiler_params=_cp(("arbitrary", "arbitrary"), 56),
        name="mlstm",
    )(*args)


def _pair_masks():
    lane = lax.broadcasted_iota(I32, (1, LANES), 1)
    first = lane < N_HD
    return first, jnp.logical_not(first)


DA_PAIRS = 4


def _attn_body(q_ref, k_ref, v_ref, o_ref):
    first, second = _pair_masks()
    for p in range(DA_PAIRS):
        sl = slice(p * LANES, (p + 1) * LANES)
        q = q_ref[:, sl]
        q2 = jnp.concatenate([jnp.where(first, q, jnp.zeros_like(q)), jnp.where(second, q, jnp.zeros_like(q))], axis=0)
        s = _dot_nt(q2, k_ref[:, sl]) * (N_HD ** -0.5)
        e = jnp.exp(s - jnp.max(s, axis=-1, keepdims=True))
        o2 = _dot(e.astype(BF16), v_ref[:, sl]) / jnp.sum(e, axis=-1, keepdims=True)
        o_ref[:, sl] = jnp.where(first, o2[0:SEQ], o2[SEQ:2 * SEQ]).astype(BF16)


def _dense_attention(big):
    w = DA_PAIRS * LANES
    cb = D // w
    return pl.pallas_call(
        _attn_body,
        grid=(BATCH, cb),
        in_specs=[pl.BlockSpec((SEQ, w), lambda b, p: (b, 4 * cb + p)),
                  pl.BlockSpec((SEQ, w), lambda b, p: (b, 5 * cb + p)),
                  pl.BlockSpec((SEQ, w), lambda b, p: (b, 6 * cb + p))],
        out_specs=pl.BlockSpec((SEQ, w), lambda b, p: (b, p)),
        out_shape=jax.ShapeDtypeStruct((NP_TOK, D), BF16),
        compiler_params=_cp(("arbitrary", "arbitrary"), 32),
        name="dense_attn",
    )(big, big, big)


NA_DR2 = 2 * MAX_WIN_H - 2
NA_LOCAL = MAX_WIN_H * GRID_W


NA_PAIRS = 2


def _natten_body(q_ref, k_ref, v_ref, kc_ref, vc_ref, tab_ref, mask_ref, o_ref):
    rb = pl.program_id(2)
    colmask = mask_ref[...]
    first, second = _pair_masks()
    for pp in range(NA_PAIRS):
        lanes = slice(pp * LANES, (pp + 1) * LANES)
        kc = kc_ref[0, 0, :, lanes].astype(BF16)
        vc = vc_ref[0, 0, :, lanes].astype(BF16)
        for qr in range(NA_QROWS):
            rows = slice(qr * GRID_W, (qr + 1) * GRID_W)
            r = NA_QROWS * rb + qr
            rstart = jnp.clip(r - MAX_WIN_H // 2, 0, GRID_H - MAX_WIN_H)
            ks = pl.multiple_of(rstart * GRID_W, GRID_W)
            kl = k_ref[pl.ds(ks, NA_LOCAL), lanes]
            vl = v_ref[pl.ds(ks, NA_LOCAL), lanes]
            dr0 = rstart - r + MAX_WIN_H - 1
            qb = q_ref[rows, lanes] * jnp.asarray(N_HD ** -0.5, BF16)
            q2 = jnp.concatenate([jnp.where(first, qb, jnp.zeros_like(qb)),
                                  jnp.where(second, qb, jnp.zeros_like(qb))], axis=0)
            bias = jnp.concatenate(
                [jnp.concatenate([tab_ref[2 * pp + par, dr0 + 2 * kp] for kp in range(MAX_WIN_H // 2)], axis=1)
                 + colmask for par in range(2)], axis=0)
            s = _dot_nt(q2, jnp.concatenate([kl, kc], axis=0))
            sl = s[:, 0:NA_LOCAL] + bias
            sc = s[:, NA_LOCAL:NA_LOCAL + PAST_LEN]
            mx = jnp.maximum(jnp.max(sl, axis=-1, keepdims=True), jnp.max(sc, axis=-1, keepdims=True))
            el = jnp.exp(sl - mx)
            ec = jnp.exp(sc - mx)
            den = jnp.sum(el, axis=-1, keepdims=True) + jnp.sum(ec, axis=-1, keepdims=True)
            p = jnp.concatenate([el.astype(BF16), ec.astype(BF16)], axis=1)
            o2 = _dot(p, jnp.concatenate([vl, vc], axis=0)) / den
            o_ref[rows, lanes] = jnp.where(first, o2[0:GRID_W], o2[GRID_W:2 * GRID_W]).astype(BF16)


def _natten(big, cache_k, cache_v, bias_tab, window_mask, layer):
    w = NA_PAIRS * LANES
    cb = D // w
    nrb = GRID_H // NA_QROWS
    qrows = NA_QROWS * GRID_W
    q0 = NP_TOK // qrows
    b0 = NP_TOK // DEC_SEQ
    return pl.pallas_call(
        _natten_body,
        grid=(DEC_BATCH, cb, nrb),
        in_specs=[pl.BlockSpec((qrows, w), lambda b, p, r: (q0 + b * nrb + r, 4 * cb + p)),
                  pl.BlockSpec((DEC_SEQ, w), lambda b, p, r: (b0 + b, 5 * cb + p)),
                  pl.BlockSpec((DEC_SEQ, w), lambda b, p, r: (b0 + b, 6 * cb + p)),
                  pl.BlockSpec((1, 1, PAST_LEN, w), lambda b, p, r: (b, layer, 0, p)),
                  pl.BlockSpec((1, 1, PAST_LEN, w), lambda b, p, r: (b, layer, 0, p)),
                  pl.BlockSpec((2 * NA_PAIRS, NA_DR2, GRID_W, 2 * GRID_W), lambda b, p, r: (p, 0, 0, 0)),
                  pl.BlockSpec((GRID_W, NA_LOCAL), lambda b, p, r: (0, 0))],
        out_specs=pl.BlockSpec((qrows, w), lambda b, p, r: (b * nrb + r, p)),
        out_shape=jax.ShapeDtypeStruct((NS_TOK, D), BF16),
        compiler_params=_cp(("arbitrary", "arbitrary", "arbitrary"), 40),
        name="natten",
    )(big, big, big, cache_k, cache_v, bias_tab, window_mask)


def _na_bias_table(rpb):
    n = GRID_W
    n_dr = 2 * MAX_WIN_H - 1
    padl = n - WIN_W
    row = jnp.pad(rpb.astype(F32), ((0, 0), (0, 0), (padl, 2 * n - 1 - padl - (2 * WIN_W - 1))), constant_values=NEG)
    flat = jnp.broadcast_to(row[:, :, None, :], (N_HEADS, n_dr, n, 2 * n - 1)).reshape(N_HEADS, n_dr, n * (2 * n - 1))
    toep = flat[:, :, n - 1:n - 1 + n * (2 * n - 2)].reshape(N_HEADS, n_dr, n, 2 * n - 2)[..., :n]
    return jnp.concatenate([toep[:, :-1], toep[:, 1:]], axis=-1)


def _na_window_mask():
    qcol = np.arange(GRID_W)[:, None]
    kcol = (np.arange(NA_LOCAL) % GRID_W)[None, :]
    cstart = np.clip(qcol - WIN_W // 2, 0, GRID_W - WIN_W)
    valid = (kcol >= cstart) & (kcol < cstart + WIN_W)
    return jnp.asarray(np.where(valid, 0.0, NEG).astype(np.float32))


TM_POST = 512


def _postmix_body(oap_ref, oas_ref, obp_ref, obs_ref, ga_ref, gb_ref, x_ref, g1_ref, nf_ref, sc2_ref, sh2_ref,
                  wa_ref, wb_ref, wo_ref, wr_ref, xo_ref, h2_ref, aff_ref, wab_ref, wbb_ref, wob_ref):
    i = pl.program_id(0)

    @pl.when(i == 0)
    def _():
        wab_ref[...] = wa_ref[0].astype(BF16)
        wbb_ref[...] = wb_ref[0].astype(BF16)
        wob_ref[...] = wo_ref[0].astype(BF16)

    prompt = i < NP_TOK // TM_POST
    a = _dot(jnp.where(prompt, oap_ref[...], oas_ref[...]), wab_ref[...])
    b = _dot(jnp.where(prompt, obp_ref[...], obs_ref[...]), wbb_ref[...])
    merged = _sigmoid(ga_ref[...].astype(F32)) * a + _sigmoid(gb_ref[...].astype(F32)) * b
    xn = x_ref[...] + g1_ref[0] * _dot(merged.astype(BF16), wob_ref[...])
    xo_ref[...] = xn
    h2 = (_rms(xn) * nf_ref[...]) * (1.0 + sc2_ref[0]) + sh2_ref[0]
    h2_ref[...] = h2
    h1, h2m, _ = _split3(h2)
    w1, w2, _ = _split3(wr_ref[...])
    lg = _dot_nt(w1, h1) + _dot_nt(w1, h2m) + _dot_nt(w2, h1)
    e = jnp.exp(lg - jnp.max(lg, axis=0, keepdims=True))
    aff_ref[...] = e / jnp.sum(e, axis=0, keepdims=True)


def _postmix(oa_p, oa_s, ob_p, ob_s, big, x, mod, norm_ffn, wa, wb, wo, wr_t, layer):
    tm = TM_POST
    np_tiles = NP_TOK // tm
    row = lambda i: (i, 0)
    const = lambda i: (0, 0)
    prow = lambda i: (jnp.minimum(i, np_tiles - 1), 0)
    srow = lambda i: (jnp.maximum(i - np_tiles, 0), 0)
    wspec = pl.BlockSpec((1, D, D), lambda i: (layer, 0, 0), pipeline_mode=pl.Buffered(1))
    return pl.pallas_call(
        _postmix_body,
        grid=(N_TOK // tm,),
        in_specs=[pl.BlockSpec((tm, D), prow), pl.BlockSpec((tm, D), srow),
                  pl.BlockSpec((tm, D), prow), pl.BlockSpec((tm, D), srow),
                  pl.BlockSpec((tm, D), lambda i: (i, 7)), pl.BlockSpec((tm, D), lambda i: (i, 8)),
                  pl.BlockSpec((tm, D), row),
                  _mod_spec(2, tm), pl.BlockSpec((1, D), const), _mod_spec(4, tm), _mod_spec(3, tm),
                  wspec, wspec, wspec,
                  pl.BlockSpec((N_EXPERTS, D), const)],
        out_specs=[pl.BlockSpec((tm, D), row), pl.BlockSpec((tm, D), row),
                   pl.BlockSpec((N_EXPERTS, tm), lambda i: (0, i))],
        out_shape=[jax.ShapeDtypeStruct((N_TOK, D), F32), jax.ShapeDtypeStruct((N_TOK, D), F32),
                   jax.ShapeDtypeStruct((N_EXPERTS, N_TOK), F32)],
        scratch_shapes=[pltpu.VMEM((D, D), BF16)] * 3,
        compiler_params=_cp(("arbitrary",), 60),
        name="postmix",
    )(oa_p, oa_s, ob_p, ob_s, big, big, x, mod, norm_ffn, mod, mod, wa, wb, wo, wr_t)


def _route_body(a_ref, idx_ref, gv_ref, off_ref, pos_ref, *, R, cap):
    a = a_ref[...]
    capf = float(cap)

    def count_ge(thr):
        c = jnp.sum(jnp.where(a >= thr, 1.0, 0.0), axis=1, keepdims=True)
        return jnp.sum(c, axis=2, keepdims=True)

    def search(i, lo_bits):
        cand = lo_bits | jnp.left_shift(jnp.int32(1), 30 - i)
        return jnp.where(count_ge(pltpu.bitcast(cand, F32)) >= capf, cand, lo_bits)

    tau_bits = lax.fori_loop(0, 31, search, jnp.zeros((N_EXPERTS, 1, 1), I32))
    lo = pltpu.bitcast(tau_bits, F32)
    hi = pltpu.bitcast(jnp.maximum(tau_bits + 1, jnp.int32(MIN_NORMAL_BITS)), F32)

    def refine(i, lh):
        lo, hi = lh
        mid = lo + (hi - lo) * 0.5
        ok = count_ge(mid) >= capf
        return jnp.where(ok, mid, lo), jnp.where(ok, hi, mid)

    lo, hi = lax.fori_loop(0, 32, refine, (lo, hi))
    above = jnp.where(a >= hi, 1.0, 0.0)
    ties = jnp.where(a >= lo, 1.0, 0.0) - above
    need = capf - jnp.sum(jnp.sum(above, axis=1, keepdims=True), axis=2, keepdims=True)

    l0 = lax.broadcasted_iota(I32, (LANES, LANES), 0)
    l1 = lax.broadcasted_iota(I32, (LANES, LANES), 1)
    upper = jnp.where(l0 <= l1, 1.0, 0.0).astype(BF16)
    r0 = lax.broadcasted_iota(I32, (R, R), 0)
    r1 = lax.broadcasted_iota(I32, (R, R), 1)
    below = jnp.where(r1 < r0, 1.0, 0.0).astype(BF16)
    eye_r = r0 == r1
    r_row = lax.broadcasted_iota(I32, (1, R), 1).astype(F32)
    lane_row = lax.broadcasted_iota(I32, (1, LANES), 1).astype(F32)
    jcol = lax.broadcasted_iota(I32, (cap, 1), 0).astype(F32)

    def prefix(x):
        within = _dot(x.astype(BF16), upper)
        tot = jnp.broadcast_to(within[:, LANES - 1:LANES], (R, LANES))
        offs = _dot(below, tot.astype(BF16))
        return within + offs, offs

    for e in range(N_EXPERTS):
        eq = ties[e]
        cin_eq, _ = prefix(eq)
        sel = above[e] + eq * jnp.where(cin_eq - eq < need[e], 1.0, 0.0)
        cin, offs = prefix(sel)
        cend_row = jnp.sum(jnp.where(eye_r, cin[:, LANES - 1:LANES], 0.0), axis=0, keepdims=True)
        bcol = jnp.sum(jnp.where(cend_row <= jcol, 1.0, 0.0), axis=1, keepdims=True)
        onehot = jnp.where(bcol == r_row, 1.0, 0.0).astype(BF16)
        lcol = jnp.sum(jnp.where(_dot_x3(onehot, cin) <= jcol, 1.0, 0.0), axis=1, keepdims=True)
        idx_ref[e] = (bcol * LANES + lcol).astype(I32)
        gv_ref[e] = jnp.sum(jnp.where(lane_row == lcol, _dot_x3(onehot, a[e]), 0.0), axis=1, keepdims=True)
        off_ref[e] = offs[:, 0:1]
        pos_ref[e] = jnp.where(sel > 0.0, cin - 1.0, -1.0)


def _route(aff3, cap):
    R = aff3.shape[1]
    full = lambda s: pl.BlockSpec(s, lambda i: (0, 0, 0))
    return pl.pallas_call(
        functools.partial(_route_body, R=R, cap=cap),
        grid=(1,),
        in_specs=[full((N_EXPERTS, R, LANES))],
        out_specs=[full((N_EXPERTS, cap, 1)), full((N_EXPERTS, cap, 1)), full((N_EXPERTS, R, 1)),
                   full((N_EXPERTS, R, LANES))],
        out_shape=[jax.ShapeDtypeStruct((N_EXPERTS, cap, 1), I32),
                   jax.ShapeDtypeStruct((N_EXPERTS, cap, 1), F32),
                   jax.ShapeDtypeStruct((N_EXPERTS, R, 1), F32),
                   jax.ShapeDtypeStruct((N_EXPERTS, R, LANES), F32)],
        compiler_params=_cp(("arbitrary",), 48),
        name="route",
    )(aff3)


GATHER_UNROLL = 8
N_FF = EXPERT_FF // FF_CHUNK
ROWS_PER_STEP = CAP_TOT // N_FF


def _gather_copy(h_hbm, xs_ref, sem, slot, src_row, dst_row, nrows):
    return pltpu.make_async_copy(h_hbm.at[pl.ds(src_row, nrows), :], xs_ref.at[slot, pl.ds(dst_row, nrows), :],
                                 sem.at[slot])


def _slab_copy(y_ref, ys_hbm, sem, e, s):
    return pltpu.make_async_copy(y_ref.at[:, s * LANES:(s + 1) * LANES], ys_hbm.at[e, :, s, :], sem)


def _ffn_body(idx_ref, h_hbm, wg_ref, wu_ref, wd_ref, gv_ref, ys_hbm, xs_ref, xb_ref, y_ref, gsem, osem):
    e = pl.program_id(0)
    f = pl.program_id(1)
    n_e = pl.num_programs(0)
    slot = e % 2

    @pl.when((e == 0) & (f == 0))
    def _():
        def issue(j, carry):
            _gather_copy(h_hbm, xs_ref, gsem, 0, idx_ref[j], j, 1).start()
            return carry
        lax.fori_loop(0, CAP_TOT, issue, 0, unroll=GATHER_UNROLL)

    @pl.when(f == 0)
    def _():
        @pl.when(e > 0)
        def _():
            for s in range(SLAB):
                _slab_copy(y_ref, ys_hbm, osem, e - 1, s).wait()

        _gather_copy(h_hbm, xs_ref, gsem, slot, 0, 0, CAP_TOT).wait()
        xb_ref[...] = xs_ref[slot].astype(BF16)

    nxt = (e + 1) % n_e
    for j in range(ROWS_PER_STEP):
        r = f * ROWS_PER_STEP + j
        _gather_copy(h_hbm, xs_ref, gsem, 1 - slot, idx_ref[nxt * CAP_TOT + r], r, 1).start()

    xb = xb_ref[...]
    g = _dot(xb, wg_ref[0, 0].astype(BF16))
    u = _dot(xb, wu_ref[0, 0].astype(BF16))
    hid = (g * _sigmoid(g) * u).astype(BF16)
    y = _dot(hid, wd_ref[0, 0].astype(BF16))

    @pl.when(f == 0)
    def _():
        y_ref[...] = y

    @pl.when(f > 0)
    def _():
        y_ref[...] = y_ref[...] + y

    @pl.when(f == pl.num_programs(1) - 1)
    def _():
        y_ref[...] = y_ref[...] * gv_ref[0]
        for s in range(SLAB):
            _slab_copy(y_ref, ys_hbm, osem, e, s).start()

        @pl.when(e == n_e - 1)
        def _():
            for s in range(SLAB):
                _slab_copy(y_ref, ys_hbm, osem, e, s).wait()
            _gather_copy(h_hbm, xs_ref, gsem, 1 - slot, 0, 0, CAP_TOT).wait()


def _ffn(idx_flat, h2, wg, wu, wd, gv, layer):
    return pl.pallas_call(
        _ffn_body,
        grid_spec=pltpu.PrefetchScalarGridSpec(
            num_scalar_prefetch=1,
            grid=(N_EXPERTS, N_FF),
            in_specs=[pl.BlockSpec(memory_space=pl.ANY),
                      pl.BlockSpec((1, 1, D, FF_CHUNK), lambda e, f, idx: (layer, e, 0, f)),
                      pl.BlockSpec((1, 1, D, FF_CHUNK), lambda e, f, idx: (layer, e, 0, f)),
                      pl.BlockSpec((1, 1, FF_CHUNK, D), lambda e, f, idx: (layer, e, f, 0)),
                      pl.BlockSpec((1, CAP_TOT, 1), lambda e, f, idx: (e, 0, 0))],
            out_specs=pl.BlockSpec(memory_space=pl.ANY),
            scratch_shapes=[pltpu.VMEM((2, CAP_TOT, D), F32), pltpu.VMEM((CAP_TOT, D), BF16),
                            pltpu.VMEM((CAP_TOT, D), F32),
                            pltpu.SemaphoreType.DMA((2,)), pltpu.SemaphoreType.DMA(())]),
        out_shape=jax.ShapeDtypeStruct((N_EXPERTS, CAP_TOT, SLAB, LANES), F32),
        compiler_params=_cp(("arbitrary", "arbitrary"), 60),
        name="expert_ffn",
    )(idx_flat, h2, wg, wu, wd, gv)


MAX_PAIRS = N_EXPERTS * TM
SEG_BITS = TM.bit_length()


def _seg_copy(ys_hbm, w_ref, sem, slot, e, src_row, dst_row, nrows):
    src = pl.multiple_of(src_row * SLAB, SLAB)
    dst = pl.multiple_of(dst_row * SLAB, SLAB)
    return pltpu.make_async_copy(ys_hbm.at[e, pl.ds(src, nrows * SLAB), :],
                                 w_ref.at[slot, pl.ds(dst, nrows * SLAB), :], sem.at[slot])


def _combine_body(lo_ref, hi_ref, ys_hbm, pos_ref, x_ref, g2_ref, nf_ref, o_ref, w_ref, acc_ref, oh_ref, sem, *, last):
    b = pl.program_id(0)

    def layout(t):
        los, segs = [], []
        off = jnp.int32(0)
        for e in range(N_EXPERTS):
            lo = lo_ref[e * N_TILES + t]
            n = hi_ref[e * N_TILES + t] - lo
            los.append(lo)
            segs.append((off, off + n))
            off = off + n
        return los, segs

    def fetch(t, slot):
        los, segs = layout(t)
        tail = jnp.minimum(segs[N_EXPERTS - 1][1] // TM, MAX_PAIRS // TM - 1)
        w_ref[slot, pl.ds(pl.multiple_of(tail * (TM * SLAB), TM * SLAB), TM * SLAB), :] = (
            jnp.zeros((TM * SLAB, LANES), F32))
        for e in range(N_EXPERTS):
            n = segs[e][1] - segs[e][0]
            for bit in range(SEG_BITS - 1, -1, -1):
                done = (n >> (bit + 1)) << (bit + 1)

                @pl.when((n & (1 << bit)) != 0)
                def _(e=e, done=done, bit=bit):
                    _seg_copy(ys_hbm, w_ref, sem, slot, e, los[e] + done, segs[e][0] + done, 1 << bit).start()

    slot = b % 2

    @pl.when(b == 0)
    def _():
        fetch(b, slot)

    @pl.when(b + 1 < pl.num_programs(0))
    def _():
        fetch(b + 1, 1 - slot)

    los, seg = layout(b)
    npairs = seg[N_EXPERTS - 1][1]
    for bit in range(MAX_PAIRS.bit_length() - 1, -1, -1):
        @pl.when((npairs & (1 << bit)) != 0)
        def _(bit=bit):
            _seg_copy(ys_hbm, w_ref, sem, slot, 0, 0, 0, 1 << bit).wait()

    shift = jnp.zeros((N_EXPERTS, 1), F32)
    erow = lax.broadcasted_iota(I32, (N_EXPERTS, 1), 0)
    for e in range(N_EXPERTS):
        shift = jnp.where(erow == e, (seg[e][0] - los[e]).astype(F32), shift)
    acc_ref[...] = jnp.zeros_like(acc_ref)
    pos = pos_ref[...]
    prow = jnp.where(pos >= 0.0, pos + shift, -1.0)
    pair = lax.broadcasted_iota(I32, (TM, 1), 0).astype(F32)

    def body(c, carry):
        local = prow - (c * TM).astype(F32)
        oh_ref[...] = jnp.zeros_like(oh_ref)
        for e in range(N_EXPERTS):
            @pl.when((seg[e][0] < (c + 1) * TM) & (seg[e][1] > c * TM))
            def _(e=e):
                oh_ref[...] += jnp.where(local[e:e + 1, :] == pair, 1.0, 0.0)
        onehot = oh_ref[...].astype(BF16)
        data = jnp.concatenate([w_ref[slot, pl.ds(c * TM * SLAB + s, TM, stride=SLAB), :] for s in range(SLAB)],
                               axis=1)
        hi, mid, _ = _split3(data)
        acc_ref[...] += _dot_tn(onehot, hi) + _dot_tn(onehot, mid)
        return carry

    lax.fori_loop(0, (npairs + TM - 1) // TM, body, 0)
    xn = x_ref[...] + g2_ref[0] * acc_ref[...]
    if last:
        xn = _rms(xn) * nf_ref[...]
    o_ref[...] = xn


def _combine(seg_lo, seg_hi, ys, pos_t, x, mod, norm_final, *, last):
    return pl.pallas_call(
        functools.partial(_combine_body, last=last),
        grid_spec=pltpu.PrefetchScalarGridSpec(
            num_scalar_prefetch=2,
            grid=(N_TILES,),
            in_specs=[pl.BlockSpec(memory_space=pl.ANY),
                      pl.BlockSpec((N_EXPERTS, TM), lambda i, lo, hi: (0, i)),
                      pl.BlockSpec((TM, D), lambda i, lo, hi: (i, 0)),
                      pl.BlockSpec((1, 1, D), lambda i, lo, hi: (_mod_row(i), 0, 5)),
                      pl.BlockSpec((1, D), lambda i, lo, hi: (0, 0))],
            out_specs=pl.BlockSpec((TM, D), lambda i, lo, hi: (i, 0)),
            scratch_shapes=[pltpu.VMEM((2, MAX_PAIRS * SLAB, LANES), F32), pltpu.VMEM((TM, D), F32),
                            pltpu.VMEM((TM, TM), F32), pltpu.SemaphoreType.DMA((2,))]),
        out_shape=jax.ShapeDtypeStruct((N_TOK, D), F32),
        compiler_params=_cp(("arbitrary",), 56),
        name="combine",
    )(seg_lo, seg_hi, ys, pos_t, x, mod, norm_final)


def _tile_segments(off_p, off_s):
    per = TM // LANES
    lo_p = off_p[:, ::per, 0].astype(I32)
    lo_s = off_s[:, ::per, 0].astype(I32) + CAP_P
    hi_p = jnp.concatenate([lo_p[:, 1:], jnp.full((N_EXPERTS, 1), CAP_P, I32)], axis=1)
    hi_s = jnp.concatenate([lo_s[:, 1:], jnp.full((N_EXPERTS, 1), CAP_TOT, I32)], axis=1)
    lo = jnp.concatenate([lo_p, lo_s], axis=1)
    hi = jnp.concatenate([hi_p, hi_s], axis=1)
    return lo.reshape(-1), hi.reshape(-1)


def kernel(x_prompt, x_sample, cache_na_k, cache_na_v, state_mlstm_C, state_mlstm_n, state_mlstm_m, c, c_ctx, w_ada, b_ada, norm_mix, norm_ffn, w_in, conv_qk, mlstm_gate_bias, mlstm_head_norm, na_rpb, w_branch_a, w_branch_b, w_out, w_router, w_expert_gate, w_expert_up, w_expert_down, norm_final):
    depth = w_in.shape[0]
    m_width = M_HEADS * M_HD
    n_gates = 4 * M_HEADS

    x = jnp.concatenate([x_prompt.reshape(NP_TOK, D), x_sample.reshape(NS_TOK, D)], axis=0)
    cond8 = jnp.concatenate([c_ctx[None], c, jnp.zeros((8 - 1 - DEC_BATCH, D), F32)], axis=0)
    mods = _adaln(cond8, w_ada, b_ada).reshape(depth, 8, 1, 6 * D)
    cos_t, sin_t = _rope_tables()
    cache_k = cache_na_k.reshape(DEC_BATCH, depth, PAST_LEN, D)
    cache_v = cache_na_v.reshape(DEC_BATCH, depth, PAST_LEN, D)
    norm_final2 = norm_final.reshape(1, D)
    window_mask = _na_window_mask()

    ks_, vs_, cs_, ns_, ms_ = [], [], [], [], []
    for l in range(depth):
        mod = mods[l]
        w_big, w_gates = _pack_w_in(w_in, l)
        gate_bias = jnp.pad(mlstm_gate_bias[l], (0, LANES - n_gates)).reshape(1, LANES)
        head_norm = mlstm_head_norm[l].reshape(1, m_width)

        big, gates, kv = _inproj(x, norm_mix[l].reshape(1, D), mod, w_big, w_gates)

        oa_p, c_new, n_new, m_new = _mlstm(big, gates, gate_bias, head_norm, conv_qk[l], None, None,
                                           nb=BATCH, T=SEQ, row0=0, hb=M_HEADS, emit_state=True)
        state = (state_mlstm_C[:, l], state_mlstm_n[:, l].reshape(DEC_BATCH, 2, M_HEADS, 1, M_HD),
                 state_mlstm_m[:, l].reshape(DEC_BATCH, 2, M_HEADS, 1, 1))
        (oa_s,) = _mlstm(big, gates, gate_bias, head_norm, conv_qk[l], (cos_t, sin_t), state,
                         nb=DEC_BATCH, T=DEC_SEQ, row0=NP_TOK // DEC_SEQ, hb=M_HEADS // 2, emit_state=False)

        ob_p = _dense_attention(big)
        ob_s = _natten(big, cache_k, cache_v, _na_bias_table(na_rpb[l]), window_mask, l)

        x, h2, aff_t = _postmix(oa_p, oa_s, ob_p, ob_s, big, x, mod, norm_ffn[l].reshape(1, D),
                                w_branch_a, w_branch_b, w_out, w_router[l].T, l)

        idx_p, gv_p, off_p, pos_p = _route(aff_t[:, :NP_TOK].reshape(N_EXPERTS, NP_TOK // LANES, LANES), CAP_P)
        idx_s, gv_s, off_s, pos_s = _route(aff_t[:, NP_TOK:].reshape(N_EXPERTS, NS_TOK // LANES, LANES), CAP_S)
        idx = jnp.concatenate([idx_p, idx_s + NP_TOK], axis=1)
        gv = jnp.concatenate([gv_p, gv_s], axis=1)
        pos_s = jnp.where(pos_s >= 0.0, pos_s + CAP_P, pos_s)
        pos_t = jnp.concatenate([pos_p.reshape(N_EXPERTS, NP_TOK), pos_s.reshape(N_EXPERTS, NS_TOK)], axis=1)
        ys = _ffn(idx.reshape(-1), h2, w_expert_gate, w_expert_up, w_expert_down, gv, l)
        ys = ys.reshape(N_EXPERTS, CAP_TOT * SLAB, LANES)
        seg_lo, seg_hi = _tile_segments(off_p, off_s)
        x = _combine(seg_lo, seg_hi, ys, pos_t, x, mod, norm_final2, last=(l == depth - 1))

        ks_.append(kv[:NP_TOK, :D].reshape(BATCH, SEQ, N_HEADS, N_HD))
        vs_.append(kv[:NP_TOK, D:].reshape(BATCH, SEQ, N_HEADS, N_HD))
        cs_.append(c_new)
        ns_.append(n_new.reshape(BATCH, 2, M_HEADS, M_HD))
        ms_.append(m_new.reshape(BATCH, 2, M_HEADS))

    y_prompt = x[:NP_TOK].reshape(BATCH, SEQ, D)
    y_sample = x[NP_TOK:].reshape(DEC_BATCH, DEC_SEQ, D)
    return (y_prompt, y_sample, jnp.stack(ks_, axis=1), jnp.stack(vs_, axis=1),
            jnp.stack(cs_, axis=1), jnp.stack(ns_, axis=1), jnp.stack(ms_, axis=1))
```

```python
import functools

import numpy as np
import jax
import jax.numpy as jnp
from jax import lax
from jax.experimental import pallas as pl
from jax.experimental.pallas import tpu as pltpu

F32 = jnp.float32
BF16 = jnp.bfloat16
I32 = jnp.int32

D = 1024
BATCH, SEQ = 16, 256
DEC_BATCH, DEC_SEQ = 4, 2048
PAST_LEN = 512
NP_TOK = BATCH * SEQ
NS_TOK = DEC_BATCH * DEC_SEQ
N_TOK = NP_TOK + NS_TOK
GRID_W = 64
GRID_H = DEC_SEQ // GRID_W
M_HEADS, M_HD = 4, 256
CHUNK = 128
N_HEADS, N_HD = 16, 64
MAX_WIN_H, WIN_W = 8, 16
N_EXPERTS, EXPERT_FF = 16, 2048
CAP_P = 2 * NP_TOK // N_EXPERTS
CAP_S = 2 * NS_TOK // N_EXPERTS
CAP_TOT = CAP_P + CAP_S
ROPE_BASE = 10000.0
EPS = 1e-6
NEG = -1e30
MIN_NORMAL_BITS = 0x00800000

TM = 256
N_TILES = N_TOK // TM
NP_TILES = NP_TOK // TM
LANES = 128
NA_QROWS = 8
FF_CHUNK = 512
SLAB = D // LANES


def _cp(sem, vmem_mb):
    return pltpu.CompilerParams(dimension_semantics=sem, vmem_limit_bytes=vmem_mb * 2 ** 20)


def _dot(a, b):
    return jnp.dot(a, b, preferred_element_type=F32)


def _dot_nt(a, b):
    return lax.dot_general(a, b, (((1,), (1,)), ((), ())), preferred_element_type=F32)


def _dot_tn(a, b):
    return lax.dot_general(a, b, (((0,), (0,)), ((), ())), preferred_element_type=F32)


def _split3(x):
    hi = x.astype(BF16)
    r = x - hi.astype(F32)
    mid = r.astype(BF16)
    lo = (r - mid.astype(F32)).astype(BF16)
    return hi, mid, lo


def _dot_x3(a_bf, x):
    hi, mid, lo = _split3(x)
    return _dot(a_bf, hi) + _dot(a_bf, mid) + _dot(a_bf, lo)


def _sigmoid(x):
    return 1.0 / (1.0 + jnp.exp(-x))


def _rms(x):
    return x * lax.rsqrt(jnp.mean(x * x, axis=-1, keepdims=True) + EPS)


def _mod_row(i, tm=TM):
    return jnp.where(i < NP_TOK // tm, 0, 1 + (i - NP_TOK // tm) // (DEC_SEQ // tm))


def _mod_spec(kind, tm=TM):
    return pl.BlockSpec((1, 1, D), lambda i: (_mod_row(i, tm), 0, kind))


def _adaln_body(c_ref, w_ref, b_ref, o_ref):
    c = c_ref[...]
    s = (c * _sigmoid(c)).astype(BF16)
    o_ref[0] = _dot(s, w_ref[0].astype(BF16)) + b_ref[0]


def _adaln(cond8, w_ada, b_ada):
    depth = w_ada.shape[0]
    tn = 1536
    return pl.pallas_call(
        _adaln_body,
        grid=(depth, 6 * D // tn),
        in_specs=[pl.BlockSpec((8, D), lambda l, j: (0, 0)),
                  pl.BlockSpec((1, D, tn), lambda l, j: (l, 0, j)),
                  pl.BlockSpec((1, 1, tn), lambda l, j: (l, 0, j))],
        out_specs=pl.BlockSpec((1, 8, tn), lambda l, j: (l, 0, j)),
        out_shape=jax.ShapeDtypeStruct((depth, 8, 6 * D), F32),
        compiler_params=_cp(("arbitrary", "arbitrary"), 40),
        name="adaln",
    )(cond8, w_ada, b_ada.reshape(depth, 1, 6 * D))


N_GATES = 4 * M_HEADS
GATE_COL = 4 * M_HEADS * M_HD


def _pack_body(w_ref, big_ref, gates_ref):
    w = w_ref[0]
    big_ref[:, 0:GATE_COL] = w[:, 0:GATE_COL].astype(BF16)
    big_ref[:, GATE_COL:9 * D] = w[:, GATE_COL + N_GATES:9 * D + N_GATES].astype(BF16)
    lane = lax.broadcasted_iota(I32, (1, LANES), 1)
    gates_ref[...] = jnp.where(lane < N_GATES, w[:, GATE_COL:GATE_COL + LANES], 0.0).astype(BF16)


def _pack_w_in(w, layer):
    tr = 128
    return pl.pallas_call(
        _pack_body,
        grid=(D // tr,),
        in_specs=[pl.BlockSpec((1, tr, w.shape[2]), lambda i: (layer, i, 0))],
        out_specs=[pl.BlockSpec((tr, 9 * D), lambda i: (i, 0)), pl.BlockSpec((tr, LANES), lambda i: (i, 0))],
        out_shape=[jax.ShapeDtypeStruct((D, 9 * D), BF16), jax.ShapeDtypeStruct((D, LANES), BF16)],
        compiler_params=_cp(("arbitrary",), 48),
        name="pack_w_in",
    )(w)


def _inproj_body(x_ref, g_ref, sc_ref, sh_ref, w_ref, wg_ref, big_ref, gates_ref, k_ref, v_ref):
    i = pl.program_id(0)
    h = (_rms(x_ref[...]) * g_ref[...]) * (1.0 + sc_ref[0]) + sh_ref[0]
    hb = h.astype(BF16)
    gates_ref[...] = _dot(hb, wg_ref[...])
    for c in range(9):
        r = _dot(hb, w_ref[:, c * D:(c + 1) * D])
        big_ref[:, c * D:(c + 1) * D] = r.astype(BF16)
        if c in (5, 6):
            @pl.when(i < NP_TOK // TM_IN)
            def _(c=c, r=r):
                (k_ref if c == 5 else v_ref)[...] = r


TM_IN = 512


def _inproj(x, norm_g, mod, w_big, w_gates):
    np_tiles = NP_TOK // TM_IN
    return pl.pallas_call(
        _inproj_body,
        grid=(N_TOK // TM_IN,),
        in_specs=[pl.BlockSpec((TM_IN, D), lambda i: (i, 0)),
                  pl.BlockSpec((1, D), lambda i: (0, 0)),
                  _mod_spec(1, TM_IN), _mod_spec(0, TM_IN),
                  pl.BlockSpec((D, 9 * D), lambda i: (0, 0), pipeline_mode=pl.Buffered(1)),
                  pl.BlockSpec((D, LANES), lambda i: (0, 0))],
        out_specs=[pl.BlockSpec((TM_IN, 9 * D), lambda i: (i, 0)),
                   pl.BlockSpec((TM_IN, LANES), lambda i: (i, 0)),
                   pl.BlockSpec((TM_IN, D), lambda i: (jnp.minimum(i, np_tiles - 1), 0)),
                   pl.BlockSpec((TM_IN, D), lambda i: (jnp.minimum(i, np_tiles - 1), 0))],
        out_shape=[jax.ShapeDtypeStruct((N_TOK, 9 * D), BF16),
                   jax.ShapeDtypeStruct((N_TOK, LANES), F32),
                   jax.ShapeDtypeStruct((NP_TOK, D), F32),
                   jax.ShapeDtypeStruct((NP_TOK, D), F32)],
        compiler_params=_cp(("arbitrary",), 60),
        name="inproj",
    )(x, norm_g, mod, mod, w_big, w_gates)


def _conv_act(x, w, cos, sin, scale):
    T = x.shape[0]
    row = lax.broadcasted_iota(I32, (T, 1), 0)
    xp = jnp.where(row == 0, 0.0, pltpu.roll(x, 1, 0))
    xn = jnp.where(row == T - 1, 0.0, pltpu.roll(x, T - 1, 0))
    y = xp * w[0:1] + x * w[1:2] + xn * w[2:3]
    y = y * _sigmoid(y)
    halves = []
    for hlf in range(2):
        sl = slice(hlf * LANES, (hlf + 1) * LANES)
        yh = y[:, sl]
        if cos is not None:
            yh = yh * cos[:, sl] + pltpu.roll(yh, LANES // 2, 1) * sin[:, sl]
        halves.append((yh * scale).astype(BF16))
    return jnp.concatenate(halves, axis=1)


def _rope_tables():
    pos = np.arange(DEC_SEQ)
    rows = (pos // GRID_W).astype(np.float32)
    cols = (pos % GRID_W).astype(np.float32)
    nfreq = M_HD // 4
    inv = (ROPE_BASE ** (-np.arange(nfreq, dtype=np.float32) / nfreq)).astype(np.float32)
    d = np.arange(M_HD)
    p = np.where(d[None, :] < M_HD // 2, rows[:, None], cols[:, None]).astype(np.float32)
    ang = (p * inv[d % nfreq][None, :]).astype(np.float32)
    sign = np.where((d % (M_HD // 2)) < nfreq, -1.0, 1.0).astype(np.float32)
    return jnp.asarray(np.cos(ang), F32), jnp.asarray(np.sin(ang) * sign[None, :], F32)


def _mlstm_body(*refs, T, hb, rope, has_state, emit_state):
    qraw_ref, kraw_ref, v_ref, om_ref, g_ref, gb_ref, hn_ref, cwq_ref, cwk_ref = refs[:9]
    pos = 9
    if rope:
        cos_ref, sin_ref = refs[pos:pos + 2]
        pos += 2
    if has_state:
        c0_ref, n0_ref, m0_ref = refs[pos:pos + 3]
        pos += 3
    oa_ref = refs[pos]
    pos += 1
    if emit_state:
        co_ref, no_ref, mo_ref = refs[pos:pos + 3]
        pos += 3
    nch = 2 * hb
    hs_refs = refs[pos:pos + nch]
    c_refs = refs[pos + nch:pos + 2 * nch]
    n_refs = refs[pos + 2 * nch:pos + 3 * nch]
    q_refs = refs[pos + 3 * nch:pos + 3 * nch + hb]
    k_refs = refs[pos + 3 * nch + hb:pos + 3 * nch + 2 * hb]

    cos = cos_ref[...] if rope else None
    sin = sin_ref[...] if rope else None
    for hh in range(hb):
        cols = slice(hh * M_HD, (hh + 1) * M_HD)
        q_refs[hh][...] = _conv_act(qraw_ref[:, cols].astype(F32), cwq_ref[:, cols], cos, sin, 1.0)
        k_refs[hh][...] = _conv_act(kraw_ref[:, cols].astype(F32), cwk_ref[:, cols], cos, sin, M_HD ** -0.5)

    head0 = pl.program_id(1) * hb
    nc = T // CHUNK
    lane = lax.broadcasted_iota(I32, (1, LANES), 1)
    r_i = lax.broadcasted_iota(I32, (CHUNK, CHUNK), 0)
    c_i = lax.broadcasted_iota(I32, (CHUNK, CHUNK), 1)
    eye = r_i == c_i
    causal = (r_i >= c_i, r_i <= c_i)
    cum = tuple(jnp.where(m, 1.0, 0.0).astype(BF16) for m in causal)

    def to_row(col):
        return jnp.sum(jnp.where(eye, col, 0.0), axis=0, keepdims=True)

    def pick(mat, colidx):
        return jnp.sum(jnp.where(lane == colidx, mat, 0.0), axis=1, keepdims=True)

    def dir_step(d, c, m_prevs):
        rows = pl.ds(pl.multiple_of(c * CHUNK, CHUNK), CHUNK)
        g = g_ref[rows, :] + gb_ref[...]
        lf = jnp.minimum(g, 0.0) - jnp.log(1.0 + jnp.exp(-jnp.abs(g)))
        bmat = _dot_x3(cum[d], lf)
        m_news = []
        for hh in range(hb):
            ch = hh * 2 + d
            hs_ref, c_ref, n_ref = hs_refs[ch], c_refs[ch], n_refs[ch]
            cols = slice(hh * M_HD, (hh + 1) * M_HD)
            m_prev = m_prevs[hh]
            ig_col = pick(g, d * M_HEADS + head0 + hh)
            b_col = pick(bmat, 2 * M_HEADS + d * M_HEADS + head0 + hh)
            ig_row = to_row(ig_col)
            b_row = to_row(b_col)
            b_last = b_row[:, CHUNK - 1:CHUNK] if d == 0 else b_row[:, 0:1]
            logd = jnp.where(causal[d], b_col - b_row + ig_row, -jnp.inf)
            inter = b_col + m_prev
            m_row = jnp.maximum(inter, jnp.max(logd, axis=1, keepdims=True))
            dmat = jnp.exp(logd - m_row)
            s_inter = jnp.exp(inter - m_row)
            q = q_refs[hh][rows, :]
            k = k_refs[hh][rows, :]
            v = v_ref[rows, cols]
            s = _dot_nt(q, k) * dmat
            num = _dot(s.astype(BF16), v) + s_inter * _dot(q, c_ref[...].astype(BF16))
            den = (jnp.sum(s, axis=1, keepdims=True)
                   + s_inter * jnp.sum(q.astype(F32) * n_ref[...], axis=1, keepdims=True))
            hs_ref[rows, :] = num / jnp.maximum(jnp.abs(den), jnp.exp(-m_row))
            log_w = b_last - b_col + ig_col
            m_new = jnp.maximum(b_last + m_prev, jnp.max(log_w, axis=0, keepdims=True))
            w = jnp.exp(log_w - m_new)
            decay = jnp.exp(b_last + m_prev - m_new)
            kw = k.astype(F32) * w
            c_ref[...] = decay * c_ref[...] + _dot_tn(kw.astype(BF16), v)
            n_ref[...] = decay * n_ref[...] + jnp.sum(kw, axis=0, keepdims=True)
            m_news.append(m_new)
        return tuple(m_news)

    m_init = ([], [])
    for hh in range(hb):
        for d in range(2):
            ch = hh * 2 + d
            if has_state:
                c_refs[ch][...] = c0_ref[0, d, hh]
                n_refs[ch][...] = n0_ref[0, d, hh]
                m_init[d].append(m0_ref[0, d, hh])
            else:
                c_refs[ch][...] = jnp.zeros_like(c_refs[ch])
                n_refs[ch][...] = jnp.zeros_like(n_refs[ch])
                m_init[d].append(jnp.zeros((1, 1), F32))

    def body(ci, ms):
        return dir_step(0, ci, ms[0]), dir_step(1, nc - 1 - ci, ms[1])

    m_fin = lax.fori_loop(0, nc, body, (tuple(m_init[0]), tuple(m_init[1])))
    for hh in range(hb):
        cols = slice(hh * M_HD, (hh + 1) * M_HD)
        if emit_state:
            for d in range(2):
                co_ref[0, d, hh] = c_refs[hh * 2 + d][...]
                no_ref[0, d, hh] = n_refs[hh * 2 + d][...]
                mo_ref[0, d, hh] = m_fin[d][hh]
        hm = _rms(hs_refs[hh * 2][...] + hs_refs[hh * 2 + 1][...]) * hn_ref[:, cols]
        oa_ref[:, cols] = (_sigmoid(om_ref[:, cols].astype(F32)) * hm).astype(BF16)


def _mlstm(big, gates, gate_bias, head_norm, conv_w, rope_tabs, state, *, nb, T, row0, hb, emit_state):
    has_state = state is not None
    rope = rope_tabs is not None
    w = hb * M_HD
    npair = M_HEADS // hb
    in_specs = [pl.BlockSpec((T, w), lambda b, h: (row0 + b, h)),
                pl.BlockSpec((T, w), lambda b, h: (row0 + b, npair + h)),
                pl.BlockSpec((T, w), lambda b, h: (row0 + b, 2 * npair + h)),
                pl.BlockSpec((T, w), lambda b, h: (row0 + b, 3 * npair + h)),
                pl.BlockSpec((T, LANES), lambda b, h: (row0 + b, 0)),
                pl.BlockSpec((1, LANES), lambda b, h: (0, 0)),
                pl.BlockSpec((1, w), lambda b, h: (0, h)),
                pl.BlockSpec((3, w), lambda b, h: (0, h)),
                pl.BlockSpec((3, w), lambda b, h: (0, npair + h))]
    args = [big, big, big, big, gates, gate_bias, head_norm, conv_w, conv_w]
    if rope:
        in_specs += [pl.BlockSpec((T, M_HD), lambda b, h: (0, 0))] * 2
        args += list(rope_tabs)
    state_specs = [pl.BlockSpec((1, 2, hb, M_HD, M_HD), lambda b, h: (b, 0, h, 0, 0)),
                   pl.BlockSpec((1, 2, hb, 1, M_HD), lambda b, h: (b, 0, h, 0, 0)),
                   pl.BlockSpec((1, 2, hb, 1, 1), lambda b, h: (b, 0, h, 0, 0))]
    if has_state:
        in_specs += state_specs
        args += list(state)
    out_specs = [pl.BlockSpec((T, w), lambda b, h: (b, h))]
    out_shape = [jax.ShapeDtypeStruct((nb * T, M_HEADS * M_HD), BF16)]
    if emit_state:
        out_specs += state_specs
        out_shape += [jax.ShapeDtypeStruct((nb, 2, M_HEADS, M_HD, M_HD), F32),
                      jax.ShapeDtypeStruct((nb, 2, M_HEADS, 1, M_HD), F32),
                      jax.ShapeDtypeStruct((nb, 2, M_HEADS, 1, 1), F32)]
    nch = 2 * hb
    return pl.pallas_call(
        functools.partial(_mlstm_body, T=T, hb=hb, rope=rope, has_state=has_state, emit_state=emit_state),
        grid=(nb, npair),
        in_specs=in_specs,
        out_specs=out_specs,
        out_shape=out_shape,
        scratch_shapes=[pltpu.VMEM((T, M_HD), F32)] * nch + [pltpu.VMEM((M_HD, M_HD), F32)] * nch
        + [pltpu.VMEM((1, M_HD), F32)] * nch + [pltpu.VMEM((T, M_HD), BF16)] * (2 * hb),
        compiler_params=_cp(("arbitrary", "arbitrary"), 56),
        name="mlstm",
    )(*args)


def _pair_masks():
    lane = lax.broadcasted_iota(I32, (1, LANES), 1)
    first = lane < N_HD
    return first, jnp.logical_not(first)


DA_PAIRS = 4


def _attn_body(q_ref, k_ref, v_ref, o_ref):
    first, second = _pair_masks()
    for p in range(DA_PAIRS):
        sl = slice(p * LANES, (p + 1) * LANES)
        q = q_ref[:, sl]
        q2 = jnp.concatenate([jnp.where(first, q, jnp.zeros_like(q)), jnp.where(second, q, jnp.zeros_like(q))], axis=0)
        s = _dot_nt(q2, k_ref[:, sl]) * (N_HD ** -0.5)
        e = jnp.exp(s - jnp.max(s, axis=-1, keepdims=True))
        o2 = _dot(e.astype(BF16), v_ref[:, sl]) / jnp.sum(e, axis=-1, keepdims=True)
        o_ref[:, sl] = jnp.where(first, o2[0:SEQ], o2[SEQ:2 * SEQ]).astype(BF16)


def _dense_attention(big):
    w = DA_PAIRS * LANES
    cb = D // w
    return pl.pallas_call(
        _attn_body,
        grid=(BATCH, cb),
        in_specs=[pl.BlockSpec((SEQ, w), lambda b, p: (b, 4 * cb + p)),
                  pl.BlockSpec((SEQ, w), lambda b, p: (b, 5 * cb + p)),
                  pl.BlockSpec((SEQ, w), lambda b, p: (b, 6 * cb + p))],
        out_specs=pl.BlockSpec((SEQ, w), lambda b, p: (b, p)),
        out_shape=jax.ShapeDtypeStruct((NP_TOK, D), BF16),
        compiler_params=_cp(("arbitrary", "arbitrary"), 32),
        name="dense_attn",
    )(big, big, big)


NA_DR2 = 2 * MAX_WIN_H - 2
NA_LOCAL = MAX_WIN_H * GRID_W


NA_PAIRS = 2


def _natten_body(q_ref, k_ref, v_ref, kc_ref, vc_ref, tab_ref, mask_ref, o_ref):
    rb = pl.program_id(2)
    colmask = mask_ref[...]
    first, second = _pair_masks()
    for pp in range(NA_PAIRS):
        lanes = slice(pp * LANES, (pp + 1) * LANES)
        kc = kc_ref[0, 0, :, lanes].astype(BF16)
        vc = vc_ref[0, 0, :, lanes].astype(BF16)
        for qr in range(NA_QROWS):
            rows = slice(qr * GRID_W, (qr + 1) * GRID_W)
            r = NA_QROWS * rb + qr
            rstart = jnp.clip(r - MAX_WIN_H // 2, 0, GRID_H - MAX_WIN_H)
            ks = pl.multiple_of(rstart * GRID_W, GRID_W)
            kl = k_ref[pl.ds(ks, NA_LOCAL), lanes]
            vl = v_ref[pl.ds(ks, NA_LOCAL), lanes]
            dr0 = rstart - r + MAX_WIN_H - 1
            qb = q_ref[rows, lanes] * jnp.asarray(N_HD ** -0.5, BF16)
            q2 = jnp.concatenate([jnp.where(first, qb, jnp.zeros_like(qb)),
                                  jnp.where(second, qb, jnp.zeros_like(qb))], axis=0)
            bias = jnp.concatenate(
                [jnp.concatenate([tab_ref[2 * pp + par, dr0 + 2 * kp] for kp in range(MAX_WIN_H // 2)], axis=1)
                 + colmask for par in range(2)], axis=0)
            s = _dot_nt(q2, jnp.concatenate([kl, kc], axis=0))
            sl = s[:, 0:NA_LOCAL] + bias
            sc = s[:, NA_LOCAL:NA_LOCAL + PAST_LEN]
            mx = jnp.maximum(jnp.max(sl, axis=-1, keepdims=True), jnp.max(sc, axis=-1, keepdims=True))
            el = jnp.exp(sl - mx)
            ec = jnp.exp(sc - mx)
            den = jnp.sum(el, axis=-1, keepdims=True) + jnp.sum(ec, axis=-1, keepdims=True)
            p = jnp.concatenate([el.astype(BF16), ec.astype(BF16)], axis=1)
            o2 = _dot(p, jnp.concatenate([vl, vc], axis=0)) / den
            o_ref[rows, lanes] = jnp.where(first, o2[0:GRID_W], o2[GRID_W:2 * GRID_W]).astype(BF16)


def _natten(big, cache_k, cache_v, bias_tab, window_mask, layer):
    w = NA_PAIRS * LANES
    cb = D // w
    nrb = GRID_H // NA_QROWS
    qrows = NA_QROWS * GRID_W
    q0 = NP_TOK // qrows
    b0 = NP_TOK // DEC_SEQ
    return pl.pallas_call(
        _natten_body,
        grid=(DEC_BATCH, cb, nrb),
        in_specs=[pl.BlockSpec((qrows, w), lambda b, p, r: (q0 + b * nrb + r, 4 * cb + p)),
                  pl.BlockSpec((DEC_SEQ, w), lambda b, p, r: (b0 + b, 5 * cb + p)),
                  pl.BlockSpec((DEC_SEQ, w), lambda b, p, r: (b0 + b, 6 * cb + p)),
                  pl.BlockSpec((1, 1, PAST_LEN, w), lambda b, p, r: (b, layer, 0, p)),
                  pl.BlockSpec((1, 1, PAST_LEN, w), lambda b, p, r: (b, layer, 0, p)),
                  pl.BlockSpec((2 * NA_PAIRS, NA_DR2, GRID_W, 2 * GRID_W), lambda b, p, r: (p, 0, 0, 0)),
                  pl.BlockSpec((GRID_W, NA_LOCAL), lambda b, p, r: (0, 0))],
        out_specs=pl.BlockSpec((qrows, w), lambda b, p, r: (b * nrb + r, p)),
        out_shape=jax.ShapeDtypeStruct((NS_TOK, D), BF16),
        compiler_params=_cp(("arbitrary", "arbitrary", "arbitrary"), 40),
        name="natten",
    )(big, big, big, cache_k, cache_v, bias_tab, window_mask)


def _bias_table_body(r_ref, o_ref):
    lane = lax.broadcasted_iota(I32, (1, LANES), 1)
    rp = r_ref[0]
    for i in range(NA_DR2):
        ra = jnp.broadcast_to(rp[i:i + 1, :], (GRID_W, LANES))
        rb = jnp.broadcast_to(rp[i + 1:i + 2, :], (GRID_W, LANES))
        ta = pltpu.roll(ra, LANES - (WIN_W - 1), 1, stride=1, stride_axis=0)
        tb = pltpu.roll(rb, GRID_W - (WIN_W - 1), 1, stride=1, stride_axis=0)
        o_ref[0, i] = jnp.where(lane < GRID_W, ta, tb)


def _na_bias_table(rpb):
    n_dr = 2 * MAX_WIN_H - 1
    rows = n_dr + 1
    rp = jnp.pad(rpb.astype(F32), ((0, 0), (0, rows - n_dr), (0, LANES - (2 * WIN_W - 1))), constant_values=NEG)
    return pl.pallas_call(
        _bias_table_body,
        grid=(N_HEADS,),
        in_specs=[pl.BlockSpec((1, rows, LANES), lambda h: (h, 0, 0))],
        out_specs=pl.BlockSpec((1, NA_DR2, GRID_W, 2 * GRID_W), lambda h: (h, 0, 0, 0)),
        out_shape=jax.ShapeDtypeStruct((N_HEADS, NA_DR2, GRID_W, 2 * GRID_W), F32),
        compiler_params=_cp(("arbitrary",), 32),
        name="na_bias_table",
    )(rp)


def _na_window_mask():
    qcol = np.arange(GRID_W)[:, None]
    kcol = (np.arange(NA_LOCAL) % GRID_W)[None, :]
    cstart = np.clip(qcol - WIN_W // 2, 0, GRID_W - WIN_W)
    valid = (kcol >= cstart) & (kcol < cstart + WIN_W)
    return jnp.asarray(np.where(valid, 0.0, NEG).astype(np.float32))


TM_POST = 512


def _postmix_body(oap_ref, oas_ref, obp_ref, obs_ref, ga_ref, gb_ref, x_ref, g1_ref, nf_ref, sc2_ref, sh2_ref,
                  wa_ref, wb_ref, wo_ref, wr_ref, xo_ref, h2_ref, aff_ref, wab_ref, wbb_ref, wob_ref):
    i = pl.program_id(0)

    @pl.when(i == 0)
    def _():
        wab_ref[...] = wa_ref[0].astype(BF16)
        wbb_ref[...] = wb_ref[0].astype(BF16)
        wob_ref[...] = wo_ref[0].astype(BF16)

    prompt = i < NP_TOK // TM_POST
    a = _dot(jnp.where(prompt, oap_ref[...], oas_ref[...]), wab_ref[...])
    b = _dot(jnp.where(prompt, obp_ref[...], obs_ref[...]), wbb_ref[...])
    merged = _sigmoid(ga_ref[...].astype(F32)) * a + _sigmoid(gb_ref[...].astype(F32)) * b
    xn = x_ref[...] + g1_ref[0] * _dot(merged.astype(BF16), wob_ref[...])
    xo_ref[...] = xn
    h2 = (_rms(xn) * nf_ref[...]) * (1.0 + sc2_ref[0]) + sh2_ref[0]
    h2_ref[...] = h2
    h1, h2m, _ = _split3(h2)
    w1, w2, _ = _split3(wr_ref[...])
    lg = _dot_nt(w1, h1) + _dot_nt(w1, h2m) + _dot_nt(w2, h1)
    e = jnp.exp(lg - jnp.max(lg, axis=0, keepdims=True))
    aff_ref[...] = e / jnp.sum(e, axis=0, keepdims=True)


def _postmix(oa_p, oa_s, ob_p, ob_s, big, x, mod, norm_ffn, wa, wb, wo, wr_t, layer):
    tm = TM_POST
    np_tiles = NP_TOK // tm
    row = lambda i: (i, 0)
    const = lambda i: (0, 0)
    prow = lambda i: (jnp.minimum(i, np_tiles - 1), 0)
    srow = lambda i: (jnp.maximum(i - np_tiles, 0), 0)
    wspec = pl.BlockSpec((1, D, D), lambda i: (layer, 0, 0), pipeline_mode=pl.Buffered(1))
    return pl.pallas_call(
        _postmix_body,
        grid=(N_TOK // tm,),
        in_specs=[pl.BlockSpec((tm, D), prow), pl.BlockSpec((tm, D), srow),
                  pl.BlockSpec((tm, D), prow), pl.BlockSpec((tm, D), srow),
                  pl.BlockSpec((tm, D), lambda i: (i, 7)), pl.BlockSpec((tm, D), lambda i: (i, 8)),
                  pl.BlockSpec((tm, D), row),
                  _mod_spec(2, tm), pl.BlockSpec((1, D), const), _mod_spec(4, tm), _mod_spec(3, tm),
                  wspec, wspec, wspec,
                  pl.BlockSpec((N_EXPERTS, D), const)],
        out_specs=[pl.BlockSpec((tm, D), row), pl.BlockSpec((tm, D), row),
                   pl.BlockSpec((N_EXPERTS, tm), lambda i: (0, i))],
        out_shape=[jax.ShapeDtypeStruct((N_TOK, D), F32), jax.ShapeDtypeStruct((N_TOK, D), F32),
                   jax.ShapeDtypeStruct((N_EXPERTS, N_TOK), F32)],
        scratch_shapes=[pltpu.VMEM((D, D), BF16)] * 3,
        compiler_params=_cp(("arbitrary",), 60),
        name="postmix",
    )(oa_p, oa_s, ob_p, ob_s, big, big, x, mod, norm_ffn, mod, mod, wa, wb, wo, wr_t)


def _route_body(a_ref, idx_ref, gv_ref, off_ref, pos_ref, *, R, cap):
    a = a_ref[...]
    capf = float(cap)

    def count_ge(thr):
        c = jnp.sum(jnp.where(a >= thr, 1.0, 0.0), axis=1, keepdims=True)
        return jnp.sum(c, axis=2, keepdims=True)

    def search(i, lo_bits):
        cand = lo_bits | jnp.left_shift(jnp.int32(1), 30 - i)
        return jnp.where(count_ge(pltpu.bitcast(cand, F32)) >= capf, cand, lo_bits)

    tau_bits = lax.fori_loop(0, 31, search, jnp.zeros((N_EXPERTS, 1, 1), I32))
    lo = pltpu.bitcast(tau_bits, F32)
    hi = pltpu.bitcast(jnp.maximum(tau_bits + 1, jnp.int32(MIN_NORMAL_BITS)), F32)

    def refine(i, lh):
        lo, hi = lh
        mid = lo + (hi - lo) * 0.5
        ok = count_ge(mid) >= capf
        return jnp.where(ok, mid, lo), jnp.where(ok, hi, mid)

    lo, hi = lax.fori_loop(0, 32, refine, (lo, hi))
    above = jnp.where(a >= hi, 1.0, 0.0)
    ties = jnp.where(a >= lo, 1.0, 0.0) - above
    need = capf - jnp.sum(jnp.sum(above, axis=1, keepdims=True), axis=2, keepdims=True)

    l0 = lax.broadcasted_iota(I32, (LANES, LANES), 0)
    l1 = lax.broadcasted_iota(I32, (LANES, LANES), 1)
    upper = jnp.where(l0 <= l1, 1.0, 0.0).astype(BF16)
    r0 = lax.broadcasted_iota(I32, (R, R), 0)
    r1 = lax.broadcasted_iota(I32, (R, R), 1)
    below = jnp.where(r1 < r0, 1.0, 0.0).astype(BF16)
    eye_r = r0 == r1
    r_row = lax.broadcasted_iota(I32, (1, R), 1).astype(F32)
    lane_row = lax.broadcasted_iota(I32, (1, LANES), 1).astype(F32)
    jcol = lax.broadcasted_iota(I32, (cap, 1), 0).astype(F32)

    def prefix(x):
        within = _dot(x.astype(BF16), upper)
        tot = jnp.broadcast_to(within[:, LANES - 1:LANES], (R, LANES))
        offs = _dot(below, tot.astype(BF16))
        return within + offs, offs

    for e in range(N_EXPERTS):
        eq = ties[e]
        cin_eq, _ = prefix(eq)
        sel = above[e] + eq * jnp.where(cin_eq - eq < need[e], 1.0, 0.0)
        cin, offs = prefix(sel)
        cend_row = jnp.sum(jnp.where(eye_r, cin[:, LANES - 1:LANES], 0.0), axis=0, keepdims=True)
        bcol = jnp.sum(jnp.where(cend_row <= jcol, 1.0, 0.0), axis=1, keepdims=True)
        onehot = jnp.where(bcol == r_row, 1.0, 0.0).astype(BF16)
        lcol = jnp.sum(jnp.where(_dot_x3(onehot, cin) <= jcol, 1.0, 0.0), axis=1, keepdims=True)
        idx_ref[e] = (bcol * LANES + lcol).astype(I32)
        gv_ref[e] = jnp.sum(jnp.where(lane_row == lcol, _dot_x3(onehot, a[e]), 0.0), axis=1, keepdims=True)
        off_ref[e] = offs[:, 0:1]
        pos_ref[e] = jnp.where(sel > 0.0, cin - 1.0, -1.0)


def _route(aff3, cap):
    R = aff3.shape[1]
    full = lambda s: pl.BlockSpec(s, lambda i: (0, 0, 0))
    return pl.pallas_call(
        functools.partial(_route_body, R=R, cap=cap),
        grid=(1,),
        in_specs=[full((N_EXPERTS, R, LANES))],
        out_specs=[full((N_EXPERTS, cap, 1)), full((N_EXPERTS, cap, 1)), full((N_EXPERTS, R, 1)),
                   full((N_EXPERTS, R, LANES))],
        out_shape=[jax.ShapeDtypeStruct((N_EXPERTS, cap, 1), I32),
                   jax.ShapeDtypeStruct((N_EXPERTS, cap, 1), F32),
                   jax.ShapeDtypeStruct((N_EXPERTS, R, 1), F32),
                   jax.ShapeDtypeStruct((N_EXPERTS, R, LANES), F32)],
        compiler_params=_cp(("arbitrary",), 48),
        name="route",
    )(aff3)


GATHER_UNROLL = 8
N_FF = EXPERT_FF // FF_CHUNK
ROWS_PER_STEP = CAP_TOT // N_FF


def _gather_copy(h_hbm, xs_ref, sem, slot, src_row, dst_row, nrows):
    return pltpu.make_async_copy(h_hbm.at[pl.ds(src_row, nrows), :], xs_ref.at[slot, pl.ds(dst_row, nrows), :],
                                 sem.at[slot])


def _slab_copy(y_ref, ys_hbm, sem, e, s):
    return pltpu.make_async_copy(y_ref.at[:, s * LANES:(s + 1) * LANES], ys_hbm.at[e, :, s, :], sem)


def _ffn_body(idx_ref, h_hbm, wg_ref, wu_ref, wd_ref, gv_ref, ys_hbm, xs_ref, xb_ref, y_ref, gsem, osem):
    e = pl.program_id(0)
    f = pl.program_id(1)
    n_e = pl.num_programs(0)
    slot = e % 2

    @pl.when((e == 0) & (f == 0))
    def _():
        def issue(j, carry):
            _gather_copy(h_hbm, xs_ref, gsem, 0, idx_ref[j], j, 1).start()
            return carry
        lax.fori_loop(0, CAP_TOT, issue, 0, unroll=GATHER_UNROLL)

    @pl.when(f == 0)
    def _():
        @pl.when(e > 0)
        def _():
            for s in range(SLAB):
                _slab_copy(y_ref, ys_hbm, osem, e - 1, s).wait()

        _gather_copy(h_hbm, xs_ref, gsem, slot, 0, 0, CAP_TOT).wait()
        xb_ref[...] = xs_ref[slot].astype(BF16)

    nxt = (e + 1) % n_e
    for j in range(ROWS_PER_STEP):
        r = f * ROWS_PER_STEP + j
        _gather_copy(h_hbm, xs_ref, gsem, 1 - slot, idx_ref[nxt * CAP_TOT + r], r, 1).start()

    xb = xb_ref[...]
    g = _dot(xb, wg_ref[0, 0].astype(BF16))
    u = _dot(xb, wu_ref[0, 0].astype(BF16))
    hid = (g * _sigmoid(g) * u).astype(BF16)
    y = _dot(hid, wd_ref[0, 0].astype(BF16))

    @pl.when(f == 0)
    def _():
        y_ref[...] = y

    @pl.when(f > 0)
    def _():
        y_ref[...] = y_ref[...] + y

    @pl.when(f == pl.num_programs(1) - 1)
    def _():
        y_ref[...] = y_ref[...] * gv_ref[0]
        for s in range(SLAB):
            _slab_copy(y_ref, ys_hbm, osem, e, s).start()

        @pl.when(e == n_e - 1)
        def _():
            for s in range(SLAB):
                _slab_copy(y_ref, ys_hbm, osem, e, s).wait()
            _gather_copy(h_hbm, xs_ref, gsem, 1 - slot, 0, 0, CAP_TOT).wait()


def _ffn(idx_flat, h2, wg, wu, wd, gv, layer):
    return pl.pallas_call(
        _ffn_body,
        grid_spec=pltpu.PrefetchScalarGridSpec(
            num_scalar_prefetch=1,
            grid=(N_EXPERTS, N_FF),
            in_specs=[pl.BlockSpec(memory_space=pl.ANY),
                      pl.BlockSpec((1, 1, D, FF_CHUNK), lambda e, f, idx: (layer, e, 0, f)),
                      pl.BlockSpec((1, 1, D, FF_CHUNK), lambda e, f, idx: (layer, e, 0, f)),
                      pl.BlockSpec((1, 1, FF_CHUNK, D), lambda e, f, idx: (layer, e, f, 0)),
                      pl.BlockSpec((1, CAP_TOT, 1), lambda e, f, idx: (e, 0, 0))],
            out_specs=pl.BlockSpec(memory_space=pl.ANY),
            scratch_shapes=[pltpu.VMEM((2, CAP_TOT, D), F32), pltpu.VMEM((CAP_TOT, D), BF16),
                            pltpu.VMEM((CAP_TOT, D), F32),
                            pltpu.SemaphoreType.DMA((2,)), pltpu.SemaphoreType.DMA(())]),
        out_shape=jax.ShapeDtypeStruct((N_EXPERTS, CAP_TOT, SLAB, LANES), F32),
        compiler_params=_cp(("arbitrary", "arbitrary"), 60),
        name="expert_ffn",
    )(idx_flat, h2, wg, wu, wd, gv)


MAX_PAIRS = N_EXPERTS * TM
SEG_BITS = TM.bit_length()


def _seg_copy(ys_hbm, w_ref, sem, slot, e, src_row, dst_row, nrows):
    src = pl.multiple_of(src_row * SLAB, SLAB)
    dst = pl.multiple_of(dst_row * SLAB, SLAB)
    return pltpu.make_async_copy(ys_hbm.at[e, pl.ds(src, nrows * SLAB), :],
                                 w_ref.at[slot, pl.ds(dst, nrows * SLAB), :], sem.at[slot])


def _combine_body(lo_ref, hi_ref, ys_hbm, pos_ref, x_ref, g2_ref, nf_ref, *refs, last):
    out_refs, (w_ref, acc_ref, oh_ref, sem) = refs[:-4], refs[-4:]
    b = pl.program_id(0)

    def layout(t):
        los, segs = [], []
        off = jnp.int32(0)
        for e in range(N_EXPERTS):
            lo = lo_ref[e * N_TILES + t]
            n = hi_ref[e * N_TILES + t] - lo
            los.append(lo)
            segs.append((off, off + n))
            off = off + n
        return los, segs

    def fetch(t, slot):
        los, segs = layout(t)
        tail = jnp.minimum(segs[N_EXPERTS - 1][1] // TM, MAX_PAIRS // TM - 1)
        w_ref[slot, pl.ds(pl.multiple_of(tail * (TM * SLAB), TM * SLAB), TM * SLAB), :] = (
            jnp.zeros((TM * SLAB, LANES), F32))
        for e in range(N_EXPERTS):
            n = segs[e][1] - segs[e][0]
            for bit in range(SEG_BITS - 1, -1, -1):
                done = (n >> (bit + 1)) << (bit + 1)

                @pl.when((n & (1 << bit)) != 0)
                def _(e=e, done=done, bit=bit):
                    _seg_copy(ys_hbm, w_ref, sem, slot, e, los[e] + done, segs[e][0] + done, 1 << bit).start()

    slot = b % 2

    @pl.when(b == 0)
    def _():
        fetch(b, slot)

    @pl.when(b + 1 < pl.num_programs(0))
    def _():
        fetch(b + 1, 1 - slot)

    los, seg = layout(b)
    npairs = seg[N_EXPERTS - 1][1]
    for bit in range(MAX_PAIRS.bit_length() - 1, -1, -1):
        @pl.when((npairs & (1 << bit)) != 0)
        def _(bit=bit):
            _seg_copy(ys_hbm, w_ref, sem, slot, 0, 0, 0, 1 << bit).wait()

    shift = jnp.zeros((N_EXPERTS, 1), F32)
    erow = lax.broadcasted_iota(I32, (N_EXPERTS, 1), 0)
    for e in range(N_EXPERTS):
        shift = jnp.where(erow == e, (seg[e][0] - los[e]).astype(F32), shift)
    acc_ref[...] = jnp.zeros_like(acc_ref)
    pos = pos_ref[...]
    prow = jnp.where(pos >= 0.0, pos + shift, -1.0)
    pair = lax.broadcasted_iota(I32, (TM, 1), 0).astype(F32)

    def body(c, carry):
        local = prow - (c * TM).astype(F32)
        oh_ref[...] = jnp.zeros_like(oh_ref)
        for e in range(N_EXPERTS):
            @pl.when((seg[e][0] < (c + 1) * TM) & (seg[e][1] > c * TM))
            def _(e=e):
                oh_ref[...] += jnp.where(local[e:e + 1, :] == pair, 1.0, 0.0)
        onehot = oh_ref[...].astype(BF16)
        data = jnp.concatenate([w_ref[slot, pl.ds(c * TM * SLAB + s, TM, stride=SLAB), :] for s in range(SLAB)],
                               axis=1)
        hi, mid, _ = _split3(data)
        acc_ref[...] += _dot_tn(onehot, hi) + _dot_tn(onehot, mid)
        return carry

    lax.fori_loop(0, (npairs + TM - 1) // TM, body, 0)
    xn = x_ref[...] + g2_ref[0] * acc_ref[...]
    if last:
        xn = _rms(xn) * nf_ref[...]
        op_ref, os_ref = out_refs

        @pl.when(b < NP_TILES)
        def _():
            op_ref[...] = xn
        os_ref[...] = xn
    else:
        out_refs[0][...] = xn


def _combine(seg_lo, seg_hi, ys, pos_t, x, mod, norm_final, *, last):
    if last:
        out_specs = [pl.BlockSpec((TM, D), lambda i, lo, hi: (jnp.minimum(i, NP_TILES - 1), 0)),
                     pl.BlockSpec((TM, D), lambda i, lo, hi: (jnp.maximum(i - NP_TILES, 0), 0))]
        out_shape = [jax.ShapeDtypeStruct((NP_TOK, D), F32), jax.ShapeDtypeStruct((NS_TOK, D), F32)]
    else:
        out_specs = [pl.BlockSpec((TM, D), lambda i, lo, hi: (i, 0))]
        out_shape = [jax.ShapeDtypeStruct((N_TOK, D), F32)]
    return pl.pallas_call(
        functools.partial(_combine_body, last=last),
        grid_spec=pltpu.PrefetchScalarGridSpec(
            num_scalar_prefetch=2,
            grid=(N_TILES,),
            in_specs=[pl.BlockSpec(memory_space=pl.ANY),
                      pl.BlockSpec((N_EXPERTS, TM), lambda i, lo, hi: (0, i)),
                      pl.BlockSpec((TM, D), lambda i, lo, hi: (i, 0)),
                      pl.BlockSpec((1, 1, D), lambda i, lo, hi: (_mod_row(i), 0, 5)),
                      pl.BlockSpec((1, D), lambda i, lo, hi: (0, 0))],
            out_specs=out_specs,
            scratch_shapes=[pltpu.VMEM((2, MAX_PAIRS * SLAB, LANES), F32), pltpu.VMEM((TM, D), F32),
                            pltpu.VMEM((TM, TM), F32), pltpu.SemaphoreType.DMA((2,))]),
        out_shape=out_shape,
        compiler_params=_cp(("arbitrary",), 56),
        name="combine",
    )(seg_lo, seg_hi, ys, pos_t, x, mod, norm_final)


def _tile_segments(off_p, off_s):
    per = TM // LANES
    lo_p = off_p[:, ::per, 0].astype(I32)
    lo_s = off_s[:, ::per, 0].astype(I32) + CAP_P
    hi_p = jnp.concatenate([lo_p[:, 1:], jnp.full((N_EXPERTS, 1), CAP_P, I32)], axis=1)
    hi_s = jnp.concatenate([lo_s[:, 1:], jnp.full((N_EXPERTS, 1), CAP_TOT, I32)], axis=1)
    lo = jnp.concatenate([lo_p, lo_s], axis=1)
    hi = jnp.concatenate([hi_p, hi_s], axis=1)
    return lo.reshape(-1), hi.reshape(-1)


def kernel(x_prompt, x_sample, cache_na_k, cache_na_v, state_mlstm_C, state_mlstm_n, state_mlstm_m, c, c_ctx, w_ada, b_ada, norm_mix, norm_ffn, w_in, conv_qk, mlstm_gate_bias, mlstm_head_norm, na_rpb, w_branch_a, w_branch_b, w_out, w_router, w_expert_gate, w_expert_up, w_expert_down, norm_final):
    depth = w_in.shape[0]
    m_width = M_HEADS * M_HD
    n_gates = 4 * M_HEADS

    x = jnp.concatenate([x_prompt.reshape(NP_TOK, D), x_sample.reshape(NS_TOK, D)], axis=0)
    cond8 = jnp.concatenate([c_ctx[None], c, jnp.zeros((8 - 1 - DEC_BATCH, D), F32)], axis=0)
    mods = _adaln(cond8, w_ada, b_ada).reshape(depth, 8, 1, 6 * D)
    cos_t, sin_t = _rope_tables()
    cache_k = cache_na_k.reshape(DEC_BATCH, depth, PAST_LEN, D)
    cache_v = cache_na_v.reshape(DEC_BATCH, depth, PAST_LEN, D)
    norm_final2 = norm_final.reshape(1, D)
    window_mask = _na_window_mask()

    ks_, vs_, cs_, ns_, ms_ = [], [], [], [], []
    for l in range(depth):
        mod = mods[l]
        w_big, w_gates = _pack_w_in(w_in, l)
        gate_bias = jnp.pad(mlstm_gate_bias[l], (0, LANES - n_gates)).reshape(1, LANES)
        head_norm = mlstm_head_norm[l].reshape(1, m_width)

        big, gates, k_new, v_new = _inproj(x, norm_mix[l].reshape(1, D), mod, w_big, w_gates)

        oa_p, c_new, n_new, m_new = _mlstm(big, gates, gate_bias, head_norm, conv_qk[l], None, None,
                                           nb=BATCH, T=SEQ, row0=0, hb=M_HEADS, emit_state=True)
        state = (state_mlstm_C[:, l], state_mlstm_n[:, l].reshape(DEC_BATCH, 2, M_HEADS, 1, M_HD),
                 state_mlstm_m[:, l].reshape(DEC_BATCH, 2, M_HEADS, 1, 1))
        (oa_s,) = _mlstm(big, gates, gate_bias, head_norm, conv_qk[l], (cos_t, sin_t), state,
                         nb=DEC_BATCH, T=DEC_SEQ, row0=NP_TOK // DEC_SEQ, hb=M_HEADS // 2, emit_state=False)

        ob_p = _dense_attention(big)
        ob_s = _natten(big, cache_k, cache_v, _na_bias_table(na_rpb[l]), window_mask, l)

        x, h2, aff_t = _postmix(oa_p, oa_s, ob_p, ob_s, big, x, mod, norm_ffn[l].reshape(1, D),
                                w_branch_a, w_branch_b, w_out, w_router[l].T, l)

        idx_p, gv_p, off_p, pos_p = _route(aff_t[:, :NP_TOK].reshape(N_EXPERTS, NP_TOK // LANES, LANES), CAP_P)
        idx_s, gv_s, off_s, pos_s = _route(aff_t[:, NP_TOK:].reshape(N_EXPERTS, NS_TOK // LANES, LANES), CAP_S)
        idx = jnp.concatenate([idx_p, idx_s + NP_TOK], axis=1)
        gv = jnp.concatenate([gv_p, gv_s], axis=1)
        pos_s = jnp.where(pos_s >= 0.0, pos_s + CAP_P, pos_s)
        pos_t = jnp.concatenate([pos_p.reshape(N_EXPERTS, NP_TOK), pos_s.reshape(N_EXPERTS, NS_TOK)], axis=1)
        ys = _ffn(idx.reshape(-1), h2, w_expert_gate, w_expert_up, w_expert_down, gv, l)
        ys = ys.reshape(N_EXPERTS, CAP_TOT * SLAB, LANES)
        seg_lo, seg_hi = _tile_segments(off_p, off_s)
        outs = _combine(seg_lo, seg_hi, ys, pos_t, x, mod, norm_final2, last=(l == depth - 1))
        x = outs[0]

        ks_.append(k_new.reshape(BATCH, SEQ, N_HEADS, N_HD))
        vs_.append(v_new.reshape(BATCH, SEQ, N_HEADS, N_HD))
        cs_.append(c_new)
        ns_.append(n_new.reshape(BATCH, 2, M_HEADS, M_HD))
        ms_.append(m_new.reshape(BATCH, 2, M_HEADS))

    y_prompt = outs[0].reshape(BATCH, SEQ, D)
    y_sample = outs[1].reshape(DEC_BATCH, DEC_SEQ, D)
    return (y_prompt, y_sample, jnp.stack(ks_, axis=1), jnp.stack(vs_, axis=1),
            jnp.stack(cs_, axis=1), jnp.stack(ns_, axis=1), jnp.stack(ms_, axis=1))
```
